```python
import jax
import jax.numpy as jnp
from jax import lax
import numpy as np

D_MODEL = 2048
BATCH = 16
SEQ = 256
DEPTH = 1
DEC_BATCH = 8
DEC_SEQ = 1024
PAST_LEN = 512

GRID_W = 64
HG_HEADS = 8
HG_DK = 128
HG_DV = 128
HG_KW = HG_HEADS * HG_DK
HG_W = HG_HEADS * HG_DV
RET_HEADS = 8
RET_DK = 128
RET_DV = 256
RET_QK_W = RET_HEADS * RET_DK
RET_W = RET_HEADS * RET_DV
IN_COLS = 3 * HG_KW + 2 * HG_W + 2 * RET_QK_W + 2 * RET_W + 2 * D_MODEL
CHUNK = 32
ROPE_PAIRS = RET_DK // 4
ROPE_BASE = 10000.0
N_EXPERTS = 32
TOP_K = 4
D_FF = D_MODEL
SWIGLU_LIMIT = 7.0
SWIGLU_ALPHA = 1.702
MOE_BLOCK = 128
N_MOD = 6
EPS = 1e-6

kernel_name = "hybrid_hgrn2_retention_moe_dit_step"


def rmsnorm(x, g):
    xf = x.astype(jnp.float32)
    return xf * lax.rsqrt(jnp.mean(xf * xf, axis=-1, keepdims=True) + EPS) * g.astype(jnp.float32)


def head_layernorm(x, g):
    xf = x.astype(jnp.float32)
    xc = xf - jnp.mean(xf, axis=-1, keepdims=True)
    return xc * lax.rsqrt(jnp.mean(xc * xc, axis=-1, keepdims=True) + EPS) * g.astype(jnp.float32)


def to_heads(a, n_heads):
    b, t, _ = a.shape
    return a.reshape(b, t, n_heads, -1).transpose(0, 2, 1, 3)


def from_heads(a):
    b, h, t, d = a.shape
    return a.transpose(0, 2, 1, 3).reshape(b, t, h * d)


def chunk_gated_linear_scan(q, k, v, log_f, s0):
    f32 = jnp.float32
    q, k, v = q.astype(f32), k.astype(f32), v.astype(f32)
    b, h, t, _ = q.shape
    n = t // CHUNK

    def blocks(a):
        return a.reshape(a.shape[:2] + (n, CHUNK) + a.shape[3:])

    qc, kc, vc = blocks(q), blocks(k), blocks(v)
    g = jnp.cumsum(blocks(log_f.astype(f32)), axis=3)
    g_last = g[:, :, :, -1:]
    q_in = qc * jnp.exp(g)
    k_in = kc * jnp.exp(-g)
    k_end = kc * jnp.exp(g_last - g)
    tri = jnp.tril(jnp.ones((CHUNK, CHUNK), f32))
    scores = jnp.einsum("bhntd,bhnsd->bhnts", q_in, k_in) * tri
    o_intra = jnp.einsum("bhnts,bhnsv->bhntv", scores, vc)
    u = jnp.einsum("bhnsd,bhnsv->bhndv", k_end, vc)
    decay = jnp.exp(g_last[:, :, :, 0])

    def step(s, inp):
        d_c, u_c = inp
        return d_c[..., None] * s + u_c, s

    s_fin, s_start = lax.scan(step, s0.astype(f32),
                              (jnp.moveaxis(decay, 2, 0), jnp.moveaxis(u, 2, 0)))
    s_start = jnp.moveaxis(s_start, 0, 2)
    o_inter = jnp.einsum("bhntd,bhndv->bhntv", q_in, s_start)
    return (o_intra + o_inter).reshape(b, h, t, -1), s_fin


def latent_rope_tables(n_tokens):
    n_rows = n_tokens // GRID_W
    rows = jnp.repeat(jnp.arange(n_rows, dtype=jnp.float32), GRID_W)
    cols = jnp.tile(jnp.arange(GRID_W, dtype=jnp.float32), n_rows)
    inv = ROPE_BASE ** (-jnp.arange(ROPE_PAIRS, dtype=jnp.float32) / ROPE_PAIRS)
    ang = jnp.stack([rows, cols], axis=-1)[:, :, None] * inv
    return jnp.cos(ang), jnp.sin(ang)


def apply_axial_rope(x, cos, sin):
    b, h, t, _ = x.shape
    xa = x.astype(jnp.float32).reshape(b, h, t, 2, 2, ROPE_PAIRS)
    x1, x2 = xa[..., 0, :], xa[..., 1, :]
    return jnp.stack([x1 * cos - x2 * sin, x2 * cos + x1 * sin], axis=-2).reshape(b, h, t, RET_DK)


def token_mixer(xm, s_hf, s_hb, s_rf, s_rb, rope, w_in, lb_f, lb_b, hg_norm_g,
                ret_pf, ret_pb, ret_norm_g, w_pa, w_pb, w_out):
    b, t, _ = xm.shape
    sizes = (HG_KW, HG_KW, HG_KW, HG_W, HG_W, RET_QK_W, RET_QK_W, RET_W, RET_W, D_MODEL, D_MODEL)
    cuts = []
    acc = 0
    for s in sizes[:-1]:
        acc += s
        cuts.append(acc)
    proj = jnp.einsum("btd,dn->btn", xm, w_in)
    (hq, hzf, hzb, hi, hgate, rq, rk, rv, rgate, mga, mgb) = jnp.split(proj, cuts, axis=-1)

    q_h = to_heads(jax.nn.silu(hq), HG_HEADS)
    v_h = to_heads(hi, HG_HEADS)

    def hgrn_direction(zf, lb, s0, reverse):
        f = lb + (1.0 - lb) * jax.nn.sigmoid(zf.astype(jnp.float32))
        k_h = to_heads(1.0 - f, HG_HEADS)
        lf = to_heads(jnp.log(f), HG_HEADS)
        qq, vv = q_h, v_h
        if reverse:
            qq, k_h, vv, lf = (jnp.flip(a, axis=2) for a in (qq, k_h, vv, lf))
        o, s = chunk_gated_linear_scan(qq, k_h, vv, lf, s0)
        return (jnp.flip(o, axis=2) if reverse else o), s

    oa_f, f_hf = hgrn_direction(hzf, lb_f, s_hf, False)
    oa_b, f_hb = hgrn_direction(hzb, lb_b, s_hb, True)
    oa_sum = oa_f + oa_b
    oa = oa_sum * lax.rsqrt(jnp.mean(oa_sum * oa_sum, axis=-1, keepdims=True) + EPS) * hg_norm_g
    oa = from_heads(oa) * jax.nn.sigmoid(hgate)

    rq_h = to_heads(rq, RET_HEADS)
    rk_h = to_heads(rk, RET_HEADS) * (RET_DK ** -0.5)
    rv_h = to_heads(rv, RET_HEADS)
    if rope is not None:
        rq_h = apply_axial_rope(rq_h, rope[0], rope[1])
        rk_h = apply_axial_rope(rk_h, rope[0], rope[1])

    def ret_direction(p, s0, reverse):
        log_gamma = jnp.log1p(-jnp.exp2(p.astype(jnp.float32)))
        lf = jnp.broadcast_to(log_gamma[None, :, None, None], (1, RET_HEADS, t, 1))
        qq, kk, vv = rq_h, rk_h, rv_h
        if reverse:
            qq, kk, vv = (jnp.flip(a, axis=2) for a in (qq, kk, vv))
        o, s = chunk_gated_linear_scan(qq, kk, vv, lf, s0)
        return (jnp.flip(o, axis=2) if reverse else o), s

    ob_f, f_rf = ret_direction(ret_pf, s_rf, False)
    ob_b, f_rb = ret_direction(ret_pb, s_rb, True)
    ob = from_heads(head_layernorm(ob_f + ob_b, ret_norm_g)) * jax.nn.silu(rgate)

    branch_a = jnp.einsum("btc,cd->btd", oa, w_pa)
    branch_b = jnp.einsum("btc,cd->btd", ob, w_pb)
    merged = jax.nn.sigmoid(mga) * branch_a + jax.nn.sigmoid(mgb) * branch_b
    out = jnp.einsum("btd,de->bte", merged, w_out)
    return out, f_hf, f_hb, f_rf, f_rb


def moe_ffn(x, router_w, router_b, w_gu, b_gu, w_dn, b_dn):
    shp = x.shape
    xt = x.reshape(-1, D_MODEL)
    t = xt.shape[0]
    logits = (jnp.einsum("td,de->te", xt, router_w) + router_b).astype(jnp.float32)
    top_val, top_idx = lax.top_k(logits, TOP_K)
    top_w = jax.nn.softmax(top_val, axis=-1)
    a = t * TOP_K
    flat_e = top_idx.reshape(-1)
    flat_tok = jnp.arange(a, dtype=jnp.int32) // TOP_K
    flat_w = top_w.reshape(-1)
    order = jnp.argsort(flat_e)
    e_sorted = flat_e[order]
    counts = jnp.bincount(flat_e, length=N_EXPERTS)
    padded = (counts + MOE_BLOCK - 1) // MOE_BLOCK * MOE_BLOCK
    pad_end = jnp.cumsum(padded)
    pad_start = pad_end - padded
    start = jnp.cumsum(counts) - counts
    dest = pad_start[e_sorted] + jnp.arange(a) - start[e_sorted]
    n_blocks = -(-a // MOE_BLOCK) + N_EXPERTS
    rows = n_blocks * MOE_BLOCK
    row_tok = jnp.full((rows,), t, jnp.int32).at[dest].set(flat_tok[order])
    row_w = jnp.zeros((rows,), jnp.float32).at[dest].set(flat_w[order])
    block_e = jnp.minimum(
        jnp.searchsorted(pad_end, jnp.arange(n_blocks) * MOE_BLOCK, side="right"), N_EXPERTS - 1)
    x_pad = jnp.concatenate([xt, jnp.zeros((1, D_MODEL), xt.dtype)], axis=0)

    def run_block(args):
        tok, e = args
        xb = x_pad[tok]
        hgu = jnp.einsum("td,df->tf", xb, w_gu[e]) + b_gu[e]
        gate = jnp.minimum(hgu[:, 0::2], SWIGLU_LIMIT)
        up = jnp.clip(hgu[:, 1::2], -SWIGLU_LIMIT, SWIGLU_LIMIT)
        act = (up + 1.0) * gate * jax.nn.sigmoid(SWIGLU_ALPHA * gate)
        return jnp.einsum("tf,fd->td", act, w_dn[e]) + b_dn[e]

    y_rows = lax.map(run_block, (row_tok.reshape(n_blocks, MOE_BLOCK), block_e))
    y_rows = y_rows.reshape(rows, D_MODEL) * row_w[:, None]
    y = jax.ops.segment_sum(y_rows, row_tok, num_segments=t + 1)[:t]
    return y.reshape(shp)


def trunk_layer(x, cond, s_hf, s_hb, s_rf, s_rb, rope, ada_w, ada_b, norm1_g, norm2_g, w_in,
                lb_f, lb_b, hg_norm_g, ret_pf, ret_pb, ret_norm_g, w_pa, w_pb, w_out,
                router_w, router_b, w_gu, b_gu, w_dn, b_dn):
    mod = jnp.einsum("bd,dm->bm", jax.nn.silu(cond), ada_w) + ada_b
    sh1, sc1, g1, sh2, sc2, g2 = jnp.split(mod[:, None, :], N_MOD, axis=-1)
    xm = rmsnorm(x, norm1_g) * (1.0 + sc1) + sh1
    mix, f_hf, f_hb, f_rf, f_rb = token_mixer(xm, s_hf, s_hb, s_rf, s_rb, rope, w_in, lb_f, lb_b,
                                              hg_norm_g, ret_pf, ret_pb, ret_norm_g, w_pa, w_pb, w_out)
    x = x + g1 * mix
    xm = rmsnorm(x, norm2_g) * (1.0 + sc2) + sh2
    x = x + g2 * moe_ffn(xm, router_w, router_b, w_gu, b_gu, w_dn, b_dn)
    return x, f_hf, f_hb, f_rf, f_rb


def setup_inputs(seed: int = 0) -> dict:
    key = jax.random.key(seed)
    ks = jax.random.split(key, 32)
    f32 = jnp.float32

    def nrm(k, shape, scale):
        return jax.random.normal(k, shape, f32) * scale

    log2_base = -5.0 - jnp.arange(RET_HEADS, dtype=f32)
    return {
        "x_prompt": nrm(ks[0], (BATCH, SEQ, D_MODEL), 1.0),
        "x_sample": nrm(ks[1], (DEC_BATCH, DEC_SEQ, D_MODEL), 1.0),
        "state_hgrn_fwd": nrm(ks[2], (DEC_BATCH, DEPTH, HG_HEADS, HG_DK, HG_DV), 0.5),
        "state_hgrn_bwd": nrm(ks[3], (DEC_BATCH, DEPTH, HG_HEADS, HG_DK, HG_DV), 0.5),
        "state_ret_fwd": nrm(ks[4], (DEC_BATCH, DEPTH, RET_HEADS, RET_DK, RET_DV), 0.5),
        "state_ret_bwd": nrm(ks[5], (DEC_BATCH, DEPTH, RET_HEADS, RET_DK, RET_DV), 0.5),
        "c": nrm(ks[6], (DEC_BATCH, D_MODEL), 1.0),
        "c_ctx": nrm(ks[7], (D_MODEL,), 1.0),
        "ada_w": nrm(ks[8], (DEPTH, D_MODEL, N_MOD * D_MODEL), D_MODEL ** -0.5),
        "ada_b": nrm(ks[9], (DEPTH, N_MOD * D_MODEL), 0.02),
        "norm1_g": 1.0 + nrm(ks[10], (DEPTH, D_MODEL), 0.02),
        "norm2_g": 1.0 + nrm(ks[11], (DEPTH, D_MODEL), 0.02),
        "final_norm_g": 1.0 + nrm(ks[12], (D_MODEL,), 0.02),
        "w_in": nrm(ks[13], (DEPTH, D_MODEL, IN_COLS), D_MODEL ** -0.5),
        "hg_lb_fwd": nrm(ks[14], (DEPTH + 1, HG_KW), 0.1),
        "hg_lb_bwd": nrm(ks[15], (DEPTH + 1, HG_KW), 0.1),
        "hg_norm_g": 1.0 + nrm(ks[16], (DEPTH, HG_DV), 0.02),
        "ret_log2_fwd": log2_base + nrm(ks[17], (DEPTH, RET_HEADS), 0.1),
        "ret_log2_bwd": log2_base + nrm(ks[18], (DEPTH, RET_HEADS), 0.1),
        "ret_norm_g": 1.0 + nrm(ks[19], (DEPTH, RET_DV), 0.02),
        "w_proj_hgrn": nrm(ks[20], (DEPTH, HG_W, D_MODEL), HG_W ** -0.5),
        "w_proj_ret": nrm(ks[21], (DEPTH, RET_W, D_MODEL), RET_W ** -0.5),
        "w_out": nrm(ks[22], (DEPTH, D_MODEL, D_MODEL), D_MODEL ** -0.5),
        "router_w": nrm(ks[23], (DEPTH, D_MODEL, N_EXPERTS), D_MODEL ** -0.5),
        "router_b": nrm(ks[24], (DEPTH, N_EXPERTS), 0.01),
        "moe_w_gu": nrm(ks[25], (DEPTH, N_EXPERTS, D_MODEL, 2 * D_FF), D_MODEL ** -0.5),
        "moe_b_gu": nrm(ks[26], (DEPTH, N_EXPERTS, 2 * D_FF), 0.02),
        "moe_w_dn": nrm(ks[27], (DEPTH, N_EXPERTS, D_FF, D_MODEL), D_FF ** -0.5),
        "moe_b_dn": nrm(ks[28], (DEPTH, N_EXPERTS, D_MODEL), 0.02),
    }


def reference(x_prompt, x_sample, state_hgrn_fwd, state_hgrn_bwd, state_ret_fwd, state_ret_bwd,
              c, c_ctx, ada_w, ada_b, norm1_g, norm2_g, final_norm_g, w_in, hg_lb_fwd, hg_lb_bwd,
              hg_norm_g, ret_log2_fwd, ret_log2_bwd, ret_norm_g, w_proj_hgrn, w_proj_ret, w_out,
              router_w, router_b, moe_w_gu, moe_b_gu, moe_w_dn, moe_b_dn):
    f32 = jnp.float32
    lb_fwd_all = jnp.cumsum(jax.nn.softmax(hg_lb_fwd.astype(f32), axis=0), axis=0)
    lb_bwd_all = jnp.cumsum(jax.nn.softmax(hg_lb_bwd.astype(f32), axis=0), axis=0)
    b_ctx = x_prompt.shape[0]
    zeros_h = jnp.zeros((b_ctx, HG_HEADS, HG_DK, HG_DV), f32)
    zeros_r = jnp.zeros((b_ctx, RET_HEADS, RET_DK, RET_DV), f32)
    rope = latent_rope_tables(x_sample.shape[1])
    cond_ctx = c_ctx[None, :]
    h_ctx, h_lat = x_prompt, x_sample
    new_hf, new_hb, new_rf, new_rb = [], [], [], []
    for l in range(DEPTH):
        lw = (ada_w[l], ada_b[l], norm1_g[l], norm2_g[l], w_in[l], lb_fwd_all[l], lb_bwd_all[l],
              hg_norm_g[l], ret_log2_fwd[l], ret_log2_bwd[l], ret_norm_g[l], w_proj_hgrn[l],
              w_proj_ret[l], w_out[l], router_w[l], router_b[l], moe_w_gu[l], moe_b_gu[l],
              moe_w_dn[l], moe_b_dn[l])
        h_ctx, f_hf, f_hb, f_rf, f_rb = trunk_layer(h_ctx, cond_ctx, zeros_h, zeros_h, zeros_r,
                                                    zeros_r, None, *lw)
        new_hf.append(f_hf)
        new_hb.append(f_hb)
        new_rf.append(f_rf)
        new_rb.append(f_rb)
        h_lat, _, _, _, _ = trunk_layer(h_lat, c, state_hgrn_fwd[:, l], state_hgrn_bwd[:, l],
                                        state_ret_fwd[:, l], state_ret_bwd[:, l], rope, *lw)
    y_prompt = rmsnorm(h_ctx, final_norm_g).astype(x_prompt.dtype)
    y_sample = rmsnorm(h_lat, final_norm_g).astype(x_sample.dtype)
    new_hgrn_fwd = jnp.stack(new_hf, axis=1)
    new_hgrn_bwd = jnp.stack(new_hb, axis=1)
    new_ret_fwd = jnp.stack(new_rf, axis=1)
    new_ret_bwd = jnp.stack(new_rb, axis=1)
    return (y_prompt, y_sample, new_hgrn_fwd, new_hgrn_bwd, new_ret_fwd, new_ret_bwd)
```

```python
import functools

import jax
import jax.numpy as jnp
from jax import lax
from jax.experimental import pallas as pl
from jax.experimental.pallas import tpu as pltpu

F32 = jnp.float32
BF16 = jnp.bfloat16

HG_HEADS = 8
HG_DK = 128
HG_DV = 128
RET_HEADS = 8
RET_DK = 128
RET_DV = 256
CHUNK = 32
GRID_W = 64
ROPE_BASE = 10000.0
TOP_K = 4
SWIGLU_LIMIT = 7.0
SWIGLU_ALPHA = 1.702
N_MOD = 6
EPS = 1e-6

LANES = 128
MXU_DIM = 256
VMEM_LIMIT = 56 * 1024 * 1024
COND_ROWS = 16
SCAN_BLOCK = 256
MOE_GROUP = 256
MOE_ITEM_ROWS = 2048
MOE_FF_TILE = 256


def _sigmoid(x):
    return 1.0 / (1.0 + jnp.exp(-x))


def _params(*sem):
    return pltpu.CompilerParams(dimension_semantics=sem, vmem_limit_bytes=VMEM_LIMIT)


def _tile(n, pref):
    if n <= pref:
        return n
    for t in range(pref - pref % LANES, 0, -LANES):
        if n % t == 0:
            return t
    raise ValueError((n, pref))


def _adaln_kernel(c_ref, w_ref, b_ref, o_ref):
    c = c_ref[...]
    s = (c * _sigmoid(c)).astype(BF16)
    o_ref[...] = jnp.dot(s, w_ref[...].astype(BF16), preferred_element_type=F32) + b_ref[...]


def _adaln(cond, ada_w, ada_b):
    d, n = ada_w.shape
    tn = _tile(n, 1024)
    return pl.pallas_call(
        _adaln_kernel,
        grid=(n // tn,),
        in_specs=[pl.BlockSpec((COND_ROWS, d), lambda j: (0, 0)),
                  pl.BlockSpec((d, tn), lambda j: (0, j)),
                  pl.BlockSpec((1, tn), lambda j: (0, j))],
        out_specs=pl.BlockSpec((COND_ROWS, tn), lambda j: (0, j)),
        out_shape=jax.ShapeDtypeStruct((COND_ROWS, n), F32),
        compiler_params=_params("arbitrary"),
        name="adaln",
    )(cond, ada_w, ada_b.reshape(1, n))


def _inproj_kernel(x_ref, sh_ref, sc_ref, g_ref, w_ref, o_ref, xm_ref):
    @pl.when(pl.program_id(1) == 0)
    def _():
        x = x_ref[...]
        ms = jnp.mean(x * x, axis=-1, keepdims=True)
        xn = x * lax.rsqrt(ms + EPS) * g_ref[...]
        xm_ref[...] = (xn * (1.0 + sc_ref[0]) + sh_ref[0]).astype(BF16)

    o_ref[...] = jnp.dot(xm_ref[...], w_ref[...], preferred_element_type=F32)


def _mod_row_fn(tm, n_ctx_rows, t_lat):
    n_ctx_tiles = n_ctx_rows // tm

    def row(i):
        return jnp.where(i < n_ctx_tiles, 0, 1 + ((i - n_ctx_tiles) * tm) // t_lat)

    return row


def _inproj(x_all, mod3, norm_g, w_bf, n_ctx_rows, t_lat):
    t_all, d = x_all.shape
    n = w_bf.shape[1]
    tm = _tile(min(n_ctx_rows, t_lat), 1024)
    tn = _tile(n, 1024)
    row = _mod_row_fn(tm, n_ctx_rows, t_lat)
    return pl.pallas_call(
        _inproj_kernel,
        grid=(t_all // tm, n // tn),
        in_specs=[pl.BlockSpec((tm, d), lambda i, j: (i, 0)),
                  pl.BlockSpec((1, 1, d), lambda i, j: (row(i) * N_MOD + 0, 0, 0)),
                  pl.BlockSpec((1, 1, d), lambda i, j: (row(i) * N_MOD + 1, 0, 0)),
                  pl.BlockSpec((1, d), lambda i, j: (0, 0)),
                  pl.BlockSpec((d, tn), lambda i, j: (0, j))],
        out_specs=pl.BlockSpec((tm, tn), lambda i, j: (i, j)),
        out_shape=jax.ShapeDtypeStruct((t_all, n), F32),
        scratch_shapes=[pltpu.VMEM((tm, d), BF16)],
        compiler_params=_params("arbitrary", "arbitrary"),
        name="inproj",
    )(x_all, mod3, mod3, norm_g.reshape(1, d), w_bf)


def _split3(x):
    hi = x.astype(BF16)
    r1 = x - hi.astype(F32)
    mid = r1.astype(BF16)
    lo = (r1 - mid.astype(F32)).astype(BF16)
    return hi, mid, lo


def _dot3(m_bf, x):
    hi, mid, lo = _split3(x)
    return (jnp.dot(m_bf, hi, preferred_element_type=F32)
            + jnp.dot(m_bf, mid, preferred_element_type=F32)
            + jnp.dot(m_bf, lo, preferred_element_type=F32))


_NT = (((1,), (1,)), ((), ()))
_TN = (((0,), (0,)), ((), ()))


def _hgrn_kernel(*refs, t, has_state, emit_state):
    hq_ref, hzf_ref, hzb_ref, hi_ref, hgate_ref, lbf_ref, lbb_ref, gn_ref = refs[:8]
    pos = 8
    if has_state:
        s0f_ref, s0b_ref = refs[pos:pos + 2]
        pos += 2
    out_ref = refs[pos]
    pos += 1
    if emit_state:
        sf_ref, sb_ref = refs[pos:pos + 2]
        pos += 2
    (q_in_f, k_in_f, k_end_f, dec_f, q_in_b, k_in_b, k_end_b, dec_b, v_bf, o_f, o_b) = refs[pos:]

    blk = min(SCAN_BLOCK, t)
    n_blk = t // blk
    n_chunk = t // CHUNK

    r = lax.broadcasted_iota(jnp.int32, (blk, blk), 0)
    c = lax.broadcasted_iota(jnp.int32, (blk, blk), 1)
    same = (r // CHUNK) == (c // CHUNK)
    m_pre = jnp.where(same & (c <= r), 1.0, 0.0).astype(BF16)
    m_suf = jnp.where(same & (c >= r), 1.0, 0.0).astype(BF16)
    m_all = jnp.where(same, 1.0, 0.0).astype(BF16)

    def lower_bound(lb_ref):
        a = lb_ref[...]
        e = jnp.exp(a - jnp.max(a, axis=0, keepdims=True))
        return e[0:1] / jnp.sum(e, axis=0, keepdims=True)

    lb_f = lower_bound(lbf_ref)
    lb_b = lower_bound(lbb_ref)

    def prep(i, carry):
        rows = pl.ds(pl.multiple_of(i * blk, blk), blk)
        hq = hq_ref[rows, :]
        q = hq * _sigmoid(hq)
        v_bf[rows, :] = hi_ref[rows, :].astype(BF16)
        for z_ref, lb, m_cum, qi, ki, ke, dc in (
                (hzf_ref, lb_f, m_pre, q_in_f, k_in_f, k_end_f, dec_f),
                (hzb_ref, lb_b, m_suf, q_in_b, k_in_b, k_end_b, dec_b)):
            f = lb + (1.0 - lb) * _sigmoid(z_ref[rows, :])
            k = 1.0 - f
            lf = jnp.log(f)
            g = _dot3(m_cum, lf)
            tot = _dot3(m_all, lf)
            qi[rows, :] = (q * jnp.exp(g)).astype(BF16)
            ki[rows, :] = (k * jnp.exp(-g)).astype(BF16)
            ke[rows, :] = (k * jnp.exp(tot - g)).astype(BF16)
            dc[rows, :] = jnp.exp(tot)
        return carry

    lax.fori_loop(0, n_blk, prep, 0)

    rr = lax.broadcasted_iota(jnp.int32, (CHUNK, CHUNK), 0)
    cc = lax.broadcasted_iota(jnp.int32, (CHUNK, CHUNK), 1)
    low = cc <= rr
    upp = cc >= rr

    def chunk_step(idx, st, qi, ki, ke, dc, mask, o_ref):
        start = pl.multiple_of(idx * CHUNK, CHUNK)
        rows = pl.ds(start, CHUNK)
        q = qi[rows, :]
        v = v_bf[rows, :]
        s = lax.dot_general(q, ki[rows, :], _NT, preferred_element_type=F32)
        s = jnp.where(mask, s, 0.0).astype(BF16)
        o = jnp.dot(s, v, preferred_element_type=F32)
        o = o + lax.dot_general(q, st.astype(BF16), _NT, preferred_element_type=F32)
        o_ref[rows, :] = o
        upd = lax.dot_general(v, ke[rows, :], _TN, preferred_element_type=F32)
        return dc[pl.ds(start, 1), :] * st + upd

    def body(i, carry):
        st_f, st_b = carry
        st_f = chunk_step(i, st_f, q_in_f, k_in_f, k_end_f, dec_f, low, o_f)
        st_b = chunk_step(n_chunk - 1 - i, st_b, q_in_b, k_in_b, k_end_b, dec_b, upp, o_b)
        return st_f, st_b

    if has_state:
        init = (s0f_ref[...].T, s0b_ref[...].T)
    else:
        init = (jnp.zeros((HG_DV, HG_DK), F32), jnp.zeros((HG_DV, HG_DK), F32))
    st_f, st_b = lax.fori_loop(0, n_chunk, body, init)
    if emit_state:
        sf_ref[...] = st_f.T
        sb_ref[...] = st_b.T

    o = o_f[...] + o_b[...]
    o = o * lax.rsqrt(jnp.mean(o * o, axis=-1, keepdims=True) + EPS) * gn_ref[...]
    out_ref[...] = (o * _sigmoid(hgate_ref[...])).astype(BF16)


def _hgrn(proj, lb_fwd, lb_bwd, norm_g, states, *, n_seq, t, row_off, emit_state):
    nh = HG_HEADS
    col = lambda k: (lambda b, h: (row_off + b, k * nh + h))
    blk = lambda k: pl.BlockSpec((t, LANES), col(k))
    in_specs = [blk(0), blk(1), blk(2), blk(3), blk(4),
                pl.BlockSpec((2, LANES), lambda b, h: (0, h)),
                pl.BlockSpec((2, LANES), lambda b, h: (0, h)),
                pl.BlockSpec((1, LANES), lambda b, h: (0, 0))]
    args = [proj] * 5 + [lb_fwd, lb_bwd, norm_g.reshape(1, HG_DV)]
    st_spec = pl.BlockSpec((None, None, None, HG_DK, HG_DV), lambda b, h: (b, 0, h, 0, 0))
    if states is not None:
        in_specs += [st_spec, st_spec]
        args += list(states)
    out_specs = [pl.BlockSpec((t, LANES), lambda b, h: (b, h))]
    out_shape = [jax.ShapeDtypeStruct((n_seq * t, nh * HG_DV), BF16)]
    if emit_state:
        out_specs += [st_spec, st_spec]
        out_shape += [jax.ShapeDtypeStruct((n_seq, 1, nh, HG_DK, HG_DV), F32)] * 2
    scratch = ([pltpu.VMEM((t, LANES), BF16)] * 3 + [pltpu.VMEM((t, LANES), F32)]) * 2
    scratch += [pltpu.VMEM((t, LANES), BF16), pltpu.VMEM((t, LANES), F32), pltpu.VMEM((t, LANES), F32)]
    return pl.pallas_call(
        functools.partial(_hgrn_kernel, t=t, has_state=states is not None, emit_state=emit_state),
        grid=(n_seq, nh),
        in_specs=in_specs,
        out_specs=out_specs,
        out_shape=out_shape,
        scratch_shapes=scratch,
        compiler_params=_params("arbitrary", "arbitrary"),
        name="hgrn_lat" if states is not None else "hgrn_ctx",
    )(*args)


def _ret_kernel(*refs, t, has_state, emit_state, use_rope):
    rq_ref, rk_ref, rv_ref, rgate_ref, pf_ref, pb_ref, gn_ref = refs[:7]
    pos = 7
    if use_rope:
        cos_ref, sin_ref = refs[pos:pos + 2]
        pos += 2
    if has_state:
        s0f_ref, s0b_ref = refs[pos:pos + 2]
        pos += 2
    out_ref = refs[pos]
    pos += 1
    if emit_state:
        sf_ref, sb_ref = refs[pos:pos + 2]
        pos += 2
    q_s, k_s, v_s, o_f, o_b, st_f, st_b = refs[pos:]

    ch = min(MXU_DIM, t)
    n_chunk = t // ch

    lane = lax.broadcasted_iota(jnp.int32, (t, RET_DK), 1)
    first_half = (lane % 64) < 32

    def rope(x):
        if not use_rope:
            return x
        swapped = jnp.where(first_half, pltpu.roll(x, RET_DK - 32, 1), pltpu.roll(x, 32, 1))
        return x * cos_ref[...] + swapped * sin_ref[...]

    q_s[...] = rope(rq_ref[...])
    k_s[...] = rope(rk_ref[...] * (RET_DK ** -0.5))
    v_s[...] = rv_ref[...].astype(BF16)

    ri = lax.broadcasted_iota(jnp.int32, (ch, ch), 0)
    ci = lax.broadcasted_iota(jnp.int32, (ch, ch), 1)
    row128 = lax.broadcasted_iota(jnp.int32, (ch, RET_DK), 0).astype(F32)

    def direction(p_ref, s_scr, o_scr, s0_ref, reverse):
        lg = jnp.log1p(-jnp.exp2(p_ref[0]))
        lg_k = lg[:, :RET_DK]
        dist = (ci - ri) if reverse else (ri - ci)
        keep = dist >= 0
        dmask = jnp.where(keep, jnp.exp(lg[:, :ch] * jnp.where(keep, dist, 0).astype(F32)), 0.0)
        if reverse:
            q_scale = jnp.exp(lg_k * (ch - row128))
            k_scale = jnp.exp(lg_k * row128)
        else:
            q_scale = jnp.exp(lg_k * (row128 + 1.0))
            k_scale = jnp.exp(lg_k * (ch - 1.0 - row128))
        dec = jnp.exp(lg * float(ch))
        s_scr[...] = s0_ref[...] if has_state else jnp.zeros((RET_DK, RET_DV), F32)
        for step in range(n_chunk):
            cidx = n_chunk - 1 - step if reverse else step
            rows = pl.ds(cidx * ch, ch)
            q = q_s[rows, :]
            k = k_s[rows, :]
            v = v_s[rows, :]
            s = lax.dot_general(q.astype(BF16), k.astype(BF16), _NT, preferred_element_type=F32)
            o = jnp.dot((s * dmask).astype(BF16), v, preferred_element_type=F32)
            st = s_scr[...]
            o = o + jnp.dot((q * q_scale).astype(BF16), st.astype(BF16), preferred_element_type=F32)
            o_scr[rows, :] = o
            upd = lax.dot_general((k * k_scale).astype(BF16), v, _TN, preferred_element_type=F32)
            s_scr[...] = dec * st + upd

    direction(pf_ref, st_f, o_f, s0f_ref if has_state else None, False)
    direction(pb_ref, st_b, o_b, s0b_ref if has_state else None, True)
    if emit_state:
        sf_ref[...] = st_f[...]
        sb_ref[...] = st_b[...]

    o = o_f[...] + o_b[...]
    oc = o - jnp.mean(o, axis=-1, keepdims=True)
    o = oc * lax.rsqrt(jnp.mean(oc * oc, axis=-1, keepdims=True) + EPS) * gn_ref[...]
    gate = rgate_ref[...]
    out_ref[...] = (o * (gate * _sigmoid(gate))).astype(BF16)


def _ret(proj, p_fwd, p_bwd, norm_g, rope, states, *, n_seq, t, row_off, emit_state):
    nh = RET_HEADS
    base = (3 * HG_HEADS * HG_DK + 2 * HG_HEADS * HG_DV)
    qk0 = base // RET_DK
    v0 = (base + 2 * nh * RET_DK) // RET_DV
    in_specs = [pl.BlockSpec((t, RET_DK), lambda b, h: (row_off + b, qk0 + h)),
                pl.BlockSpec((t, RET_DK), lambda b, h: (row_off + b, qk0 + nh + h)),
                pl.BlockSpec((t, RET_DV), lambda b, h: (row_off + b, v0 + h)),
                pl.BlockSpec((t, RET_DV), lambda b, h: (row_off + b, v0 + nh + h)),
                pl.BlockSpec((1, 1, RET_DV), lambda b, h: (h, 0, 0)),
                pl.BlockSpec((1, 1, RET_DV), lambda b, h: (h, 0, 0)),
                pl.BlockSpec((1, RET_DV), lambda b, h: (0, 0))]
    args = [proj] * 4 + [p_fwd, p_bwd, norm_g.reshape(1, RET_DV)]
    if rope is not None:
        in_specs += [pl.BlockSpec((t, RET_DK), lambda b, h: (0, 0))] * 2
        args += list(rope)
    st_spec = pl.BlockSpec((None, None, None, RET_DK, RET_DV), lambda b, h: (b, 0, h, 0, 0))
    if states is not None:
        in_specs += [st_spec, st_spec]
        args += list(states)
    out_specs = [pl.BlockSpec((t, RET_DV), lambda b, h: (b, h))]
    out_shape = [jax.ShapeDtypeStruct((n_seq * t, nh * RET_DV), BF16)]
    if emit_state:
        out_specs += [st_spec, st_spec]
        out_shape += [jax.ShapeDtypeStruct((n_seq, 1, nh, RET_DK, RET_DV), F32)] * 2
    scratch = [pltpu.VMEM((t, RET_DK), F32), pltpu.VMEM((t, RET_DK), F32), pltpu.VMEM((t, RET_DV), BF16),
               pltpu.VMEM((t, RET_DV), F32), pltpu.VMEM((t, RET_DV), F32),
               pltpu.VMEM((RET_DK, RET_DV), F32), pltpu.VMEM((RET_DK, RET_DV), F32)]
    return pl.pallas_call(
        functools.partial(_ret_kernel, t=t, has_state=states is not None, emit_state=emit_state,
                          use_rope=rope is not None),
        grid=(n_seq, nh),
        in_specs=in_specs,
        out_specs=out_specs,
        out_shape=out_shape,
        scratch_shapes=scratch,
        compiler_params=_params("arbitrary", "arbitrary"),
        name="ret_lat" if states is not None else "ret_ctx",
    )(*args)


def _rope_tables(t):
    pairs = RET_DK // 4
    pos = jnp.arange(t, dtype=jnp.int32)
    rows = (pos // GRID_W).astype(F32)
    cols = (pos % GRID_W).astype(F32)
    inv = ROPE_BASE ** (-jnp.arange(pairs, dtype=F32) / pairs)
    ar = rows[:, None] * inv
    ac = cols[:, None] * inv
    cos = jnp.concatenate([jnp.cos(ar), jnp.cos(ar), jnp.cos(ac), jnp.cos(ac)], axis=1)
    sin = jnp.concatenate([-jnp.sin(ar), jnp.sin(ar), -jnp.sin(ac), jnp.sin(ac)], axis=1)
    return cos, sin


def _merge_kernel(oa_ref, ob_ref, ga_ref, gb_ref, wa_ref, wb_ref, o_ref):
    a = jnp.dot(oa_ref[...], wa_ref[...], preferred_element_type=F32)
    b = jnp.dot(ob_ref[...], wb_ref[...], preferred_element_type=F32)
    o_ref[...] = (_sigmoid(ga_ref[...]) * a + _sigmoid(gb_ref[...]) * b).astype(BF16)


def _merge(oa, ob, proj, wa_bf, wb_bf):
    t_all, ka = oa.shape
    kb = ob.shape[1]
    d = wa_bf.shape[1]
    tm = _tile(t_all, 512)
    tn = _tile(d, 1024)
    gate0 = (proj.shape[1] - 2 * d) // tn
    return pl.pallas_call(
        _merge_kernel,
        grid=(d // tn, t_all // tm),
        in_specs=[pl.BlockSpec((tm, ka), lambda j, i: (i, 0)),
                  pl.BlockSpec((tm, kb), lambda j, i: (i, 0)),
                  pl.BlockSpec((tm, tn), lambda j, i: (i, gate0 + j)),
                  pl.BlockSpec((tm, tn), lambda j, i: (i, gate0 + d // tn + j)),
                  pl.BlockSpec((ka, tn), lambda j, i: (0, j)),
                  pl.BlockSpec((kb, tn), lambda j, i: (0, j))],
        out_specs=pl.BlockSpec((tm, tn), lambda j, i: (i, j)),
        out_shape=jax.ShapeDtypeStruct((t_all, d), BF16),
        compiler_params=_params("arbitrary", "arbitrary"),
        name="merge",
    )(oa, ob, proj, proj, wa_bf, wb_bf)


def _pack_bf16_pairs(x):
    half = x.shape[1] // 2
    lo = lax.bitcast_convert_type(x[:, :half].astype(BF16).astype(F32), jnp.uint32)
    hi = lax.bitcast_convert_type(x[:, half:].astype(BF16).astype(F32), jnp.uint32)
    return (lo >> 16) | (hi & jnp.uint32(0xFFFF0000))


def _unpack_bf16_pairs(w):
    lo = lax.bitcast_convert_type(w << 16, F32).astype(BF16)
    hi = lax.bitcast_convert_type(w & jnp.uint32(0xFFFF0000), F32).astype(BF16)
    return lo, hi


def _router_kernel(m_ref, x_ref, g1_ref, sh_ref, sc_ref, ng_ref, wo_ref, rw_ref, rb_ref,
                   x1_ref, xp_ref, idx_ref, wgt_ref):
    out = jnp.dot(m_ref[...], wo_ref[...], preferred_element_type=F32)
    x1 = x_ref[...] + g1_ref[0] * out
    x1_ref[...] = x1
    ms = jnp.mean(x1 * x1, axis=-1, keepdims=True)
    xm = x1 * lax.rsqrt(ms + EPS) * ng_ref[...]
    xm = xm * (1.0 + sc_ref[0]) + sh_ref[0]
    xp_ref[...] = _pack_bf16_pairs(xm)

    logits = jnp.dot(xm, rw_ref[...], preferred_element_type=F32,
                     precision=lax.Precision.HIGHEST) + rb_ref[...]
    lane = lax.broadcasted_iota(jnp.int32, logits.shape, 1)
    lane_f = lane.astype(F32)
    idx_out = jnp.zeros(logits.shape, F32)
    val_out = jnp.zeros(logits.shape, F32)
    top = None
    for k in range(TOP_K):
        m = jnp.max(logits, axis=-1, keepdims=True)
        idx = jnp.min(jnp.where(logits == m, lane_f, float(LANES)), axis=-1, keepdims=True)
        if top is None:
            top = m
        idx_out = jnp.where(lane == k, idx, idx_out)
        val_out = jnp.where(lane == k, jnp.exp(m - top), val_out)
        logits = jnp.where(lane_f == idx, -jnp.inf, logits)
    idx_ref[...] = idx_out.astype(jnp.int32)
    wgt_ref[...] = val_out / jnp.sum(val_out, axis=-1, keepdims=True)


def _router(merged, x_all, mod3, norm_g, wo_bf, rw_pad, rb_pad, n_ctx_rows, t_lat):
    t_all, d = x_all.shape
    tm = _tile(min(n_ctx_rows, t_lat), 256)
    row = _mod_row_fn(tm, n_ctx_rows, t_lat)
    mod_spec = lambda k: pl.BlockSpec((1, 1, d), lambda i: (row(i) * N_MOD + k, 0, 0))
    return pl.pallas_call(
        _router_kernel,
        grid=(t_all // tm,),
        in_specs=[pl.BlockSpec((tm, d), lambda i: (i, 0)),
                  pl.BlockSpec((tm, d), lambda i: (i, 0)),
                  mod_spec(2), mod_spec(3), mod_spec(4),
                  pl.BlockSpec((1, d), lambda i: (0, 0)),
                  pl.BlockSpec((d, d), lambda i: (0, 0)),
                  pl.BlockSpec((d, LANES), lambda i: (0, 0)),
                  pl.BlockSpec((1, LANES), lambda i: (0, 0))],
        out_specs=[pl.BlockSpec((tm, d), lambda i: (i, 0)),
                   pl.BlockSpec((tm, d // 2), lambda i: (i, 0)),
                   pl.BlockSpec((tm, LANES), lambda i: (i, 0)),
                   pl.BlockSpec((tm, LANES), lambda i: (i, 0))],
        out_shape=[jax.ShapeDtypeStruct((t_all, d), F32),
                   jax.ShapeDtypeStruct((t_all, d // 2), jnp.uint32),
                   jax.ShapeDtypeStruct((t_all, LANES), jnp.int32),
                   jax.ShapeDtypeStruct((t_all, LANES), F32)],
        compiler_params=_params("arbitrary"),
        name="router",
    )(merged, x_all, mod3, mod3, mod3, norm_g.reshape(1, d), wo_bf, rw_pad, rb_pad)


def _moe_kernel(e_ref, start_ref, nsub_ref, tail_ref,
                xp_hbm, tok_hbm, wgu_ref, wdn_ref, bgu_ref, bdn_ref, perm_ref,
                ys_hbm,
                tok_smem, x_buf, y_acc, wgu_bf, wdn_bf, tok_sem, row_sem, out_sem,
                *, n_ff_tiles):
    i = pl.program_id(0)
    f = pl.program_id(1)
    nsub = nsub_ref[i]
    start = start_ref[i]
    g = MOE_GROUP
    d = y_acc.shape[1]
    half = d // 2
    ff2 = wgu_bf.shape[1]

    def row_copy(r):
        tok = tok_smem[r // LANES, r % LANES]
        return pltpu.make_async_copy(xp_hbm.at[pl.ds(tok, 1), :], x_buf.at[pl.ds(r, 1), :], row_sem)

    @pl.when((f == 0) & (nsub > 0))
    def _():
        tok_copy = pltpu.make_async_copy(tok_hbm.at[pl.ds(start, tok_smem.shape[0]), :], tok_smem, tok_sem)
        tok_copy.start()
        tok_copy.wait()

        def issue(r, c):
            row_copy(r).start()
            return c

        lax.fori_loop(0, nsub * g, issue, 0)

        def init(sb, c):
            rows = pl.ds(pl.multiple_of(sb * g, g), g)
            y_acc[rows, :] = jnp.broadcast_to(bdn_ref[...], (g, d))
            return c

        lax.fori_loop(0, nsub, init, 0)

        def drain(r, c):
            row_copy(r).wait()
            return c

        lax.fori_loop(0, nsub * g, drain, 0)

    @pl.when(nsub > 0)
    def _():
        for j in range(ff2 // MXU_DIM):
            cols = slice(j * MXU_DIM, (j + 1) * MXU_DIM)
            wgu_bf[:, cols] = jnp.dot(wgu_ref[:, cols].astype(BF16), perm_ref[...],
                                      preferred_element_type=F32).astype(BF16)
        wdn_bf[...] = wdn_ref[...].astype(BF16)
        bgu = bgu_ref[...]

        def sub_block(sb, c):
            rows = pl.ds(pl.multiple_of(sb * g, g), g)
            x_lo, x_hi = _unpack_bf16_pairs(x_buf[rows, :])
            h = (jnp.dot(x_lo, wgu_bf[:half, :], preferred_element_type=F32)
                 + jnp.dot(x_hi, wgu_bf[half:, :], preferred_element_type=F32) + bgu)
            acts = []
            for j in range(ff2 // MXU_DIM):
                gate = jnp.minimum(h[:, j * MXU_DIM: j * MXU_DIM + LANES], SWIGLU_LIMIT)
                up = jnp.clip(h[:, j * MXU_DIM + LANES: (j + 1) * MXU_DIM], -SWIGLU_LIMIT, SWIGLU_LIMIT)
                acts.append(((up + 1.0) * gate * _sigmoid(SWIGLU_ALPHA * gate)).astype(BF16))
            act = jnp.concatenate(acts, axis=1)
            y_acc[rows, :] += jnp.dot(act, wdn_bf[...], preferred_element_type=F32)
            return c

        lax.fori_loop(0, nsub, sub_block, 0)

    @pl.when((f == n_ff_tiles - 1) & (nsub > 0))
    def _():
        def out_copy(sb):
            rows = pl.ds(pl.multiple_of(sb * g, g), g)
            dst = pl.ds(pl.multiple_of(start * LANES + sb * g, g), g)
            return pltpu.make_async_copy(y_acc.at[rows, :], ys_hbm.at[dst, :], out_sem)

        def issue(sb, c):
            out_copy(sb).start()
            return c

        def drain(sb, c):
            out_copy(sb).wait()
            return c

        lax.fori_loop(0, nsub, issue, 0)
        lax.fori_loop(0, nsub, drain, 0)

    @pl.when((i == pl.num_programs(0) - 1) & (f == n_ff_tiles - 1))
    def _():
        y_acc[0:g, :] = jnp.zeros((g, d), F32)
        first = tail_ref[0]
        n_tail = ys_hbm.shape[0] // g - first

        def tail_copy(k):
            dst = pl.ds(pl.multiple_of((first + k) * g, g), g)
            return pltpu.make_async_copy(y_acc.at[0:g, :], ys_hbm.at[dst, :], out_sem)

        def issue(k, c):
            tail_copy(k).start()
            return c

        def drain(k, c):
            tail_copy(k).wait()
            return c

        lax.fori_loop(0, n_tail, issue, 0)
        lax.fori_loop(0, n_tail, drain, 0)


def _moe(xp, tok2d, item_e, item_start, item_nsub, tail, w_gu, w_dn, b_gu_perm, b_dn, perm, n_rows):
    n_exp, d, ff2_all = w_gu.shape
    ff = ff2_all // 2
    tf = _tile(ff, MOE_FF_TILE)
    n_f = ff // tf
    n_items = item_e.shape[0]

    def f_eff(i, f, nsub):
        return jnp.where(nsub[i] > 0, f, n_f - 1)

    grid_spec = pltpu.PrefetchScalarGridSpec(
        num_scalar_prefetch=4,
        grid=(n_items, n_f),
        in_specs=[pl.BlockSpec(memory_space=pl.ANY),
                  pl.BlockSpec(memory_space=pl.ANY),
                  pl.BlockSpec((None, d, 2 * tf), lambda i, f, e, s, n, t: (e[i], 0, f_eff(i, f, n))),
                  pl.BlockSpec((None, tf, d), lambda i, f, e, s, n, t: (e[i], f_eff(i, f, n), 0)),
                  pl.BlockSpec((None, 1, 2 * tf), lambda i, f, e, s, n, t: (e[i], 0, f_eff(i, f, n))),
                  pl.BlockSpec((None, 1, d), lambda i, f, e, s, n, t: (e[i], 0, 0)),
                  pl.BlockSpec((MXU_DIM, MXU_DIM), lambda i, f, e, s, n, t: (0, 0))],
        out_specs=pl.BlockSpec(memory_space=pl.ANY),
        scratch_shapes=[pltpu.SMEM((MOE_ITEM_ROWS // LANES, LANES), jnp.int32),
                        pltpu.VMEM((MOE_ITEM_ROWS, d // 2), jnp.uint32),
                        pltpu.VMEM((MOE_ITEM_ROWS, d), F32),
                        pltpu.VMEM((d, 2 * tf), BF16),
                        pltpu.VMEM((tf, d), BF16),
                        pltpu.SemaphoreType.DMA(()),
                        pltpu.SemaphoreType.DMA(()),
                        pltpu.SemaphoreType.DMA(())],
    )
    return pl.pallas_call(
        functools.partial(_moe_kernel, n_ff_tiles=n_f),
        grid_spec=grid_spec,
        out_shape=jax.ShapeDtypeStruct((n_rows, d), F32),
        compiler_params=_params("arbitrary", "arbitrary"),
        name="moe",
    )(item_e, item_start, item_nsub, tail, xp, tok2d, w_gu, w_dn,
      b_gu_perm.reshape(n_exp, 1, ff2_all), b_dn.reshape(n_exp, 1, d), perm)


def _moe_plan(top_idx, n_exp):
    t_all = top_idx.shape[0]
    n_assign = t_all * TOP_K
    g = MOE_GROUP
    rows_max = n_assign + n_exp * g
    n_items = n_exp + rows_max // MOE_ITEM_ROWS
    n_rows = rows_max + MOE_ITEM_ROWS

    flat_e = top_idx.reshape(-1)
    order = jnp.argsort(flat_e)
    e_sorted = flat_e[order]
    counts = jnp.bincount(flat_e, length=n_exp).astype(jnp.int32)
    padded = (counts + g - 1) // g * g
    pad_start = jnp.cumsum(padded) - padded
    first = jnp.cumsum(counts) - counts
    dest_sorted = (pad_start[e_sorted] + jnp.arange(n_assign, dtype=jnp.int32) - first[e_sorted]).astype(jnp.int32)
    dest = jnp.zeros((n_assign,), jnp.int32).at[order].set(dest_sorted)
    row_tok = jnp.zeros((n_rows,), jnp.int32).at[dest_sorted].set((order // TOP_K).astype(jnp.int32))

    chunks = (padded + MOE_ITEM_ROWS - 1) // MOE_ITEM_ROWS
    chunk_end = jnp.cumsum(chunks)
    items = jnp.arange(n_items, dtype=jnp.int32)
    n_used = chunk_end[-1]
    last_e = jnp.max(jnp.where(chunks > 0, jnp.arange(n_exp, dtype=jnp.int32), 0))
    e_of = jnp.minimum(jnp.searchsorted(chunk_end, items, side="right"), n_exp - 1).astype(jnp.int32)
    local = items - (chunk_end[e_of] - chunks[e_of])
    used = items < n_used
    item_e = jnp.where(used, e_of, last_e).astype(jnp.int32)
    left = padded[e_of] - local * MOE_ITEM_ROWS
    item_nsub = jnp.where(used, jnp.clip(left, 0, MOE_ITEM_ROWS) // g, 0).astype(jnp.int32)
    item_start = jnp.where(used, (pad_start[e_of] + local * MOE_ITEM_ROWS) // LANES, 0).astype(jnp.int32)
    tail = (jnp.sum(padded) // g).astype(jnp.int32).reshape(1)
    return dest, row_tok.reshape(n_rows // LANES, LANES), item_e, item_start, item_nsub, tail, n_rows


def _combine_kernel(dest_ref, ys_hbm, wgt_ref, x1_ref, g2_ref, fg_ref, o_ref, gbuf, sem, *, tm):
    n = tm * TOP_K

    def row_copy(a):
        d = dest_ref[a // LANES, a % LANES]
        slot = (a % TOP_K) * tm + a // TOP_K
        return pltpu.make_async_copy(ys_hbm.at[pl.ds(d, 1), :], gbuf.at[pl.ds(slot, 1), :], sem)

    def issue(a, c):
        row_copy(a).start()
        return c

    def drain(a, c):
        row_copy(a).wait()
        return c

    lax.fori_loop(0, n, issue, 0)
    lax.fori_loop(0, n, drain, 0)

    w = wgt_ref[...]
    y = w[:, 0:1] * gbuf[0:tm, :]
    for k in range(1, TOP_K):
        y = y + w[:, k:k + 1] * gbuf[k * tm:(k + 1) * tm, :]
    x = x1_ref[...] + g2_ref[0] * y
    ms = jnp.mean(x * x, axis=-1, keepdims=True)
    o_ref[...] = x * lax.rsqrt(ms + EPS) * fg_ref[...]


def _combine(dest2d, ys, wgt, x1, mod3, final_g, n_ctx_rows, t_lat):
    t_all, d = x1.shape
    tm = _tile(min(n_ctx_rows, t_lat), 256)
    row = _mod_row_fn(tm, n_ctx_rows, t_lat)
    idx_rows = tm * TOP_K // LANES
    return pl.pallas_call(
        functools.partial(_combine_kernel, tm=tm),
        grid=(t_all // tm,),
        in_specs=[pl.BlockSpec((idx_rows, LANES), lambda i: (i, 0), memory_space=pltpu.SMEM),
                  pl.BlockSpec(memory_space=pl.ANY),
                  pl.BlockSpec((tm, LANES), lambda i: (i, 0)),
                  pl.BlockSpec((tm, d), lambda i: (i, 0)),
                  pl.BlockSpec((1, 1, d), lambda i: (row(i) * N_MOD + 5, 0, 0)),
                  pl.BlockSpec((1, d), lambda i: (0, 0))],
        out_specs=pl.BlockSpec((tm, d), lambda i: (i, 0)),
        out_shape=jax.ShapeDtypeStruct((t_all, d), F32),
        scratch_shapes=[pltpu.VMEM((tm * TOP_K, d), F32), pltpu.SemaphoreType.DMA(())],
        compiler_params=_params("arbitrary"),
        name="combine",
    )(dest2d, ys, wgt, x1, mod3, final_g.reshape(1, d))


def kernel(x_prompt, x_sample, state_hgrn_fwd, state_hgrn_bwd, state_ret_fwd, state_ret_bwd, c, c_ctx, ada_w, ada_b, norm1_g, norm2_g, final_norm_g, w_in, hg_lb_fwd, hg_lb_bwd, hg_norm_g, ret_log2_fwd, ret_log2_bwd, ret_norm_g, w_proj_hgrn, w_proj_ret, w_out, router_w, router_b, moe_w_gu, moe_b_gu, moe_w_dn, moe_b_dn):
    assert ada_w.shape[0] == 1, "one trunk layer"
    b_ctx, t_ctx, d = x_prompt.shape
    b_lat, t_lat, _ = x_sample.shape
    n_ctx_rows = b_ctx * t_ctx
    assert n_ctx_rows % t_lat == 0 and b_lat + 1 <= COND_ROWS
    n_exp = router_w.shape[2]

    x_all = jnp.concatenate([x_prompt.reshape(n_ctx_rows, d), x_sample.reshape(b_lat * t_lat, d)], axis=0)
    cond = jnp.zeros((COND_ROWS, d), F32).at[0].set(c_ctx).at[1:1 + b_lat].set(c)
    mod3 = _adaln(cond, ada_w[0], ada_b[0]).reshape(COND_ROWS * N_MOD, 1, d)

    proj = _inproj(x_all, mod3, norm1_g[0], w_in[0].astype(BF16), n_ctx_rows, t_lat)

    hg_args = (proj, hg_lb_fwd, hg_lb_bwd, hg_norm_g[0])
    oa_ctx, new_hf, new_hb = _hgrn(*hg_args, None, n_seq=b_ctx, t=t_ctx, row_off=0, emit_state=True)
    (oa_lat,) = _hgrn(*hg_args, (state_hgrn_fwd, state_hgrn_bwd), n_seq=b_lat, t=t_lat,
                      row_off=n_ctx_rows // t_lat, emit_state=False)

    p_f = jnp.broadcast_to(ret_log2_fwd[0][:, None, None], (RET_HEADS, 1, RET_DV))
    p_b = jnp.broadcast_to(ret_log2_bwd[0][:, None, None], (RET_HEADS, 1, RET_DV))
    ret_args = (proj, p_f, p_b, ret_norm_g[0])
    ob_ctx, new_rf, new_rb = _ret(*ret_args, None, None, n_seq=b_ctx, t=t_ctx, row_off=0, emit_state=True)
    (ob_lat,) = _ret(*ret_args, _rope_tables(t_lat), (state_ret_fwd, state_ret_bwd), n_seq=b_lat, t=t_lat,
                     row_off=n_ctx_rows // t_lat, emit_state=False)

    oa = jnp.concatenate([oa_ctx, oa_lat], axis=0)
    ob = jnp.concatenate([ob_ctx, ob_lat], axis=0)
    merged = _merge(oa, ob, proj, w_proj_hgrn[0].astype(BF16), w_proj_ret[0].astype(BF16))

    rw_pad = jnp.zeros((d, LANES), F32).at[:, :n_exp].set(router_w[0])
    rb_pad = jnp.full((1, LANES), -1e30, F32).at[0, :n_exp].set(router_b[0])
    x1, xp, top_idx, top_w = _router(merged, x_all, mod3, norm2_g[0], w_out[0].astype(BF16),
                                     rw_pad, rb_pad, n_ctx_rows, t_lat)

    dest, tok2d, item_e, item_start, item_nsub, tail, n_rows = _moe_plan(top_idx[:, :TOP_K], n_exp)
    ff = moe_w_dn.shape[2]
    b_gu_perm = moe_b_gu[0].reshape(n_exp, ff // LANES, LANES, 2).transpose(0, 1, 3, 2).reshape(n_exp, 2 * ff)
    src = jnp.arange(MXU_DIM)
    perm = jnp.zeros((MXU_DIM, MXU_DIM), BF16).at[src, (src % 2) * LANES + src // 2].set(1)
    ys = _moe(xp, tok2d, item_e, item_start, item_nsub, tail, moe_w_gu[0], moe_w_dn[0], b_gu_perm, moe_b_dn[0],
              perm, n_rows)

    y_all = _combine(dest.reshape(-1, LANES), ys, top_w, x1, mod3, final_norm_g, n_ctx_rows, t_lat)
    y_prompt = y_all[:n_ctx_rows].reshape(b_ctx, t_ctx, d)
    y_sample = y_all[n_ctx_rows:].reshape(b_lat, t_lat, d)
    return (y_prompt, y_sample, new_hf, new_hb, new_rf, new_rb)
```

```python
import functools

import jax
import jax.numpy as jnp
from jax import lax
from jax.experimental import pallas as pl
from jax.experimental.pallas import tpu as pltpu

F32 = jnp.float32
BF16 = jnp.bfloat16

HG_HEADS = 8
HG_DK = 128
HG_DV = 128
RET_HEADS = 8
RET_DK = 128
RET_DV = 256
CHUNK = 32
GRID_W = 64
ROPE_BASE = 10000.0
TOP_K = 4
SWIGLU_LIMIT = 7.0
SWIGLU_ALPHA = 1.702
N_MOD = 6
EPS = 1e-6

LANES = 128
MXU_DIM = 256
VMEM_LIMIT = 56 * 1024 * 1024
COND_ROWS = 16
SCAN_BLOCK = 256
SCAN_UNROLL = 4
HGRN_HEADS_PER_STEP = 4
MOE_GROUP = 256
MOE_ITEM_ROWS = 2048
MOE_FF_TILE = 256
ROW_UNROLL = 8


def _sigmoid(x):
    return 0.5 * jnp.tanh(0.5 * x) + 0.5


def _params(*sem):
    return pltpu.CompilerParams(dimension_semantics=sem, vmem_limit_bytes=VMEM_LIMIT)


def _tile(n, pref):
    if n <= pref:
        return n
    for t in range(pref - pref % LANES, 0, -LANES):
        if n % t == 0:
            return t
    raise ValueError((n, pref))


def _adaln_kernel(c_ref, w_ref, b_ref, o_ref):
    c = c_ref[...]
    s = (c * _sigmoid(c)).astype(BF16)
    o_ref[...] = jnp.dot(s, w_ref[...].astype(BF16), preferred_element_type=F32) + b_ref[...]


def _adaln(cond, ada_w, ada_b):
    d, n = ada_w.shape
    tn = _tile(n, 1024)
    return pl.pallas_call(
        _adaln_kernel,
        grid=(n // tn,),
        in_specs=[pl.BlockSpec((COND_ROWS, d), lambda j: (0, 0)),
                  pl.BlockSpec((d, tn), lambda j: (0, j)),
                  pl.BlockSpec((1, tn), lambda j: (0, j))],
        out_specs=pl.BlockSpec((COND_ROWS, tn), lambda j: (0, j)),
        out_shape=jax.ShapeDtypeStruct((COND_ROWS, n), F32),
        compiler_params=_params("arbitrary"),
        name="adaln",
    )(cond, ada_w, ada_b.reshape(1, n))


def _inproj_kernel(x_ref, sh_ref, sc_ref, g_ref, w_ref, o_ref, xm_ref):
    @pl.when(pl.program_id(1) == 0)
    def _():
        x = x_ref[...]
        ms = jnp.mean(x * x, axis=-1, keepdims=True)
        xn = x * lax.rsqrt(ms + EPS) * g_ref[...]
        xm_ref[...] = (xn * (1.0 + sc_ref[0]) + sh_ref[0]).astype(BF16)

    o_ref[...] = jnp.dot(xm_ref[...], w_ref[...], preferred_element_type=F32)


def _mod_row_fn(tm, n_ctx_rows, t_lat):
    n_ctx_tiles = n_ctx_rows // tm

    def row(i):
        return jnp.where(i < n_ctx_tiles, 0, 1 + ((i - n_ctx_tiles) * tm) // t_lat)

    return row


def _inproj(x_all, mod3, norm_g, w_bf, n_ctx_rows, t_lat):
    t_all, d = x_all.shape
    n = w_bf.shape[1]
    tm = _tile(min(n_ctx_rows, t_lat), 1024)
    tn = _tile(n, 1024)
    row = _mod_row_fn(tm, n_ctx_rows, t_lat)
    return pl.pallas_call(
        _inproj_kernel,
        grid=(t_all // tm, n // tn),
        in_specs=[pl.BlockSpec((tm, d), lambda i, j: (i, 0)),
                  pl.BlockSpec((1, 1, d), lambda i, j: (row(i) * N_MOD + 0, 0, 0)),
                  pl.BlockSpec((1, 1, d), lambda i, j: (row(i) * N_MOD + 1, 0, 0)),
                  pl.BlockSpec((1, d), lambda i, j: (0, 0)),
                  pl.BlockSpec((d, tn), lambda i, j: (0, j))],
        out_specs=pl.BlockSpec((tm, tn), lambda i, j: (i, j)),
        out_shape=jax.ShapeDtypeStruct((t_all, n), F32),
        scratch_shapes=[pltpu.VMEM((tm, d), BF16)],
        compiler_params=_params("arbitrary", "arbitrary"),
        name="inproj",
    )(x_all, mod3, mod3, norm_g.reshape(1, d), w_bf)


def _dot_split(m_bf, x):
    hi = x.astype(BF16)
    lo = (x - hi.astype(F32)).astype(BF16)
    return jnp.dot(m_bf, hi, preferred_element_type=F32) + jnp.dot(m_bf, lo, preferred_element_type=F32)


_NT = (((1,), (1,)), ((), ()))
_TN = (((0,), (0,)), ((), ()))


def _hgrn_kernel(*refs, t, hpg, has_state, emit_state):
    hq_ref, hzf_ref, hzb_ref, hi_ref, hgate_ref, lbf_ref, lbb_ref, gn_ref = refs[:8]
    pos = 8
    if has_state:
        s0f_ref, s0b_ref = refs[pos:pos + 2]
        pos += 2
    out_ref = refs[pos]
    pos += 1
    if emit_state:
        sf_ref, sb_ref = refs[pos:pos + 2]
        pos += 2
    (q_in_f, k_end_f, dec_f, q_in_b, k_end_b, dec_b, v_bf, o_f, o_b, st_ref) = refs[pos:]
    heads = [slice(j * LANES, (j + 1) * LANES) for j in range(hpg)]

    blk = min(SCAN_BLOCK, t)
    n_blk = t // blk
    n_chunk = t // CHUNK

    r = lax.broadcasted_iota(jnp.int32, (blk, blk), 0)
    c = lax.broadcasted_iota(jnp.int32, (blk, blk), 1)
    same = (r // CHUNK) == (c // CHUNK)
    low = same & (c <= r)
    upp = same & (c >= r)
    m_pre = jnp.where(low, 1.0, 0.0).astype(BF16)
    m_suf = jnp.where(upp, 1.0, 0.0).astype(BF16)

    def lower_bound(lb_ref):
        a = lb_ref[...]
        e = jnp.exp(a - jnp.max(a, axis=0, keepdims=True))
        return e[0:1] / jnp.sum(e, axis=0, keepdims=True)

    lb_f = lower_bound(lbf_ref)
    lb_b = lower_bound(lbb_ref)

    width = hpg * LANES

    def chunk_total(g, reverse):
        g3 = g.reshape(blk // CHUNK, CHUNK, width)
        edge = g3[:, 0:1, :] if reverse else g3[:, CHUNK - 1:CHUNK, :]
        return jnp.broadcast_to(edge, g3.shape).reshape(blk, width)

    directions = ((hzf_ref, lb_f, m_pre, low, False, q_in_f, k_end_f, dec_f, o_f),
                  (hzb_ref, lb_b, m_suf, upp, True, q_in_b, k_end_b, dec_b, o_b))

    def prep(i, carry):
        rows = pl.ds(pl.multiple_of(i * blk, blk), blk)
        hq = hq_ref[rows, :]
        q = hq * _sigmoid(hq)
        v = hi_ref[rows, :].astype(BF16)
        v_bf[rows, :] = v
        for z_ref, lb, m_cum, mask, reverse, qi, ke, dc, o_ref in directions:
            f = lb + (1.0 - lb) * _sigmoid(z_ref[rows, :])
            k = 1.0 - f
            g = _dot_split(m_cum, jnp.log(f))
            tot = chunk_total(g, reverse)
            q_in = (q * jnp.exp(g)).astype(BF16)
            k_in = (k * jnp.exp(-g)).astype(BF16)
            qi[rows, :] = q_in
            ke[rows, :] = (k * jnp.exp(tot - g)).astype(BF16)
            dc[rows, :] = jnp.exp(tot)
            for hd in heads:
                s = lax.dot_general(q_in[:, hd], k_in[:, hd], _NT, preferred_element_type=F32)
                o_ref[rows, hd] = jnp.dot(jnp.where(mask, s, 0.0).astype(BF16), v[:, hd],
                                          preferred_element_type=F32)
        return carry

    lax.fori_loop(0, n_blk, prep, 0)

    for j in range(hpg):
        for slot, s0_ref in ((2 * j, s0f_ref if has_state else None), (2 * j + 1, s0b_ref if has_state else None)):
            st_ref[slot] = s0_ref[j].T if has_state else jnp.zeros((HG_DV, HG_DK), F32)

    def chunk_step(idx, slot, hd, qi, ke, dc, o_ref):
        start = pl.multiple_of(idx * CHUNK, CHUNK)
        rows = pl.ds(start, CHUNK)
        st = st_ref[slot]
        o_ref[rows, hd] += lax.dot_general(qi[rows, hd], st.astype(BF16), _NT, preferred_element_type=F32)
        upd = lax.dot_general(v_bf[rows, hd], ke[rows, hd], _TN, preferred_element_type=F32)
        st_ref[slot] = dc[pl.ds(start, 1), hd] * st + upd

    def body(i, carry):
        for j, hd in enumerate(heads):
            chunk_step(i, 2 * j, hd, q_in_f, k_end_f, dec_f, o_f)
            chunk_step(n_chunk - 1 - i, 2 * j + 1, hd, q_in_b, k_end_b, dec_b, o_b)
        return carry

    lax.fori_loop(0, n_chunk, body, 0, unroll=min(n_chunk, SCAN_UNROLL))
    if emit_state:
        for j in range(hpg):
            sf_ref[j] = st_ref[2 * j].T
            sb_ref[j] = st_ref[2 * j + 1].T

    for hd in heads:
        o = o_f[:, hd] + o_b[:, hd]
        o = o * lax.rsqrt(jnp.mean(o * o, axis=-1, keepdims=True) + EPS) * gn_ref[...]
        out_ref[:, hd] = (o * _sigmoid(hgate_ref[:, hd])).astype(BF16)


def _hgrn(proj, lb_fwd, lb_bwd, norm_g, states, *, n_seq, t, row_off, emit_state):
    nh = HG_HEADS
    hpg = min(nh, HGRN_HEADS_PER_STEP)
    assert nh % hpg == 0
    ng = nh // hpg
    w = hpg * LANES
    col = lambda k: (lambda b, h: (row_off + b, k * ng + h))
    blk = lambda k: pl.BlockSpec((t, w), col(k))
    in_specs = [blk(0), blk(1), blk(2), blk(3), blk(4),
                pl.BlockSpec((2, w), lambda b, h: (0, h)),
                pl.BlockSpec((2, w), lambda b, h: (0, h)),
                pl.BlockSpec((1, LANES), lambda b, h: (0, 0))]
    args = [proj] * 5 + [lb_fwd, lb_bwd, norm_g.reshape(1, HG_DV)]
    st_spec = pl.BlockSpec((None, None, hpg, HG_DK, HG_DV), lambda b, h: (b, 0, h, 0, 0))
    if states is not None:
        in_specs += [st_spec, st_spec]
        args += list(states)
    out_specs = [pl.BlockSpec((t, w), lambda b, h: (b, h))]
    out_shape = [jax.ShapeDtypeStruct((n_seq * t, nh * HG_DV), BF16)]
    if emit_state:
        out_specs += [st_spec, st_spec]
        out_shape += [jax.ShapeDtypeStruct((n_seq, 1, nh, HG_DK, HG_DV), F32)] * 2
    scratch = ([pltpu.VMEM((t, w), BF16)] * 2 + [pltpu.VMEM((t, w), F32)]) * 2
    scratch += [pltpu.VMEM((t, w), BF16), pltpu.VMEM((t, w), F32), pltpu.VMEM((t, w), F32),
                pltpu.VMEM((2 * hpg, HG_DV, HG_DK), F32)]
    return pl.pallas_call(
        functools.partial(_hgrn_kernel, t=t, hpg=hpg, has_state=states is not None, emit_state=emit_state),
        grid=(n_seq, ng),
        in_specs=in_specs,
        out_specs=out_specs,
        out_shape=out_shape,
        scratch_shapes=scratch,
        compiler_params=_params("arbitrary", "arbitrary"),
        name="hgrn_lat" if states is not None else "hgrn_ctx",
    )(*args)


def _ret_kernel(*refs, t, has_state, emit_state, use_rope):
    rq_ref, rk_ref, rv_ref, rgate_ref, pf_ref, pb_ref, gn_ref = refs[:7]
    pos = 7
    if use_rope:
        cos_ref, sin_ref = refs[pos:pos + 2]
        pos += 2
    if has_state:
        s0f_ref, s0b_ref = refs[pos:pos + 2]
        pos += 2
    out_ref = refs[pos]
    pos += 1
    if emit_state:
        sf_ref, sb_ref = refs[pos:pos + 2]
        pos += 2
    q_s, k_s, v_s, o_f, o_b, st_f, st_b = refs[pos:]

    ch = min(MXU_DIM, t)
    n_chunk = t // ch

    lane = lax.broadcasted_iota(jnp.int32, (t, RET_DK), 1)
    first_half = (lane % 64) < 32

    def rope(x):
        if not use_rope:
            return x
        swapped = jnp.where(first_half, pltpu.roll(x, RET_DK - 32, 1), pltpu.roll(x, 32, 1))
        return x * cos_ref[...] + swapped * sin_ref[...]

    q_s[...] = rope(rq_ref[...])
    k_s[...] = rope(rk_ref[...] * (RET_DK ** -0.5))
    v_s[...] = rv_ref[...].astype(BF16)

    ri = lax.broadcasted_iota(jnp.int32, (ch, ch), 0)
    ci = lax.broadcasted_iota(jnp.int32, (ch, ch), 1)
    row128 = lax.broadcasted_iota(jnp.int32, (ch, RET_DK), 0).astype(F32)

    def direction(p_ref, s_scr, o_scr, s0_ref, reverse):
        lg = jnp.log1p(-jnp.exp2(p_ref[0]))
        lg_k = lg[:, :RET_DK]
        dist = (ci - ri) if reverse else (ri - ci)
        keep = dist >= 0
        dmask = jnp.where(keep, jnp.exp(lg[:, :ch] * jnp.where(keep, dist, 0).astype(F32)), 0.0)
        if reverse:
            q_scale = jnp.exp(lg_k * (ch - row128))
            k_scale = jnp.exp(lg_k * row128)
        else:
            q_scale = jnp.exp(lg_k * (row128 + 1.0))
            k_scale = jnp.exp(lg_k * (ch - 1.0 - row128))
        dec = jnp.exp(lg * float(ch))
        s_scr[...] = s0_ref[...] if has_state else jnp.zeros((RET_DK, RET_DV), F32)
        for step in range(n_chunk):
            cidx = n_chunk - 1 - step if reverse else step
            rows = pl.ds(cidx * ch, ch)
            q = q_s[rows, :]
            k = k_s[rows, :]
            v = v_s[rows, :]
            s = lax.dot_general(q.astype(BF16), k.astype(BF16), _NT, preferred_element_type=F32)
            o = jnp.dot((s * dmask).astype(BF16), v, preferred_element_type=F32)
            st = s_scr[...]
            o = o + jnp.dot((q * q_scale).astype(BF16), st.astype(BF16), preferred_element_type=F32)
            o_scr[rows, :] = o
            upd = lax.dot_general((k * k_scale).astype(BF16), v, _TN, preferred_element_type=F32)
            s_scr[...] = dec * st + upd

    direction(pf_ref, st_f, o_f, s0f_ref if has_state else None, False)
    direction(pb_ref, st_b, o_b, s0b_ref if has_state else None, True)
    if emit_state:
        sf_ref[...] = st_f[...]
        sb_ref[...] = st_b[...]

    o = o_f[...] + o_b[...]
    oc = o - jnp.mean(o, axis=-1, keepdims=True)
    o = oc * lax.rsqrt(jnp.mean(oc * oc, axis=-1, keepdims=True) + EPS) * gn_ref[...]
    gate = rgate_ref[...]
    out_ref[...] = (o * (gate * _sigmoid(gate))).astype(BF16)


def _ret(proj, p_fwd, p_bwd, norm_g, rope, states, *, n_seq, t, row_off, emit_state):
    nh = RET_HEADS
    base = (3 * HG_HEADS * HG_DK + 2 * HG_HEADS * HG_DV)
    qk0 = base // RET_DK
    v0 = (base + 2 * nh * RET_DK) // RET_DV
    in_specs = [pl.BlockSpec((t, RET_DK), lambda b, h: (row_off + b, qk0 + h)),
                pl.BlockSpec((t, RET_DK), lambda b, h: (row_off + b, qk0 + nh + h)),
                pl.BlockSpec((t, RET_DV), lambda b, h: (row_off + b, v0 + h)),
                pl.BlockSpec((t, RET_DV), lambda b, h: (row_off + b, v0 + nh + h)),
                pl.BlockSpec((1, 1, RET_DV), lambda b, h: (h, 0, 0)),
                pl.BlockSpec((1, 1, RET_DV), lambda b, h: (h, 0, 0)),
                pl.BlockSpec((1, RET_DV), lambda b, h: (0, 0))]
    args = [proj] * 4 + [p_fwd, p_bwd, norm_g.reshape(1, RET_DV)]
    if rope is not None:
        in_specs += [pl.BlockSpec((t, RET_DK), lambda b, h: (0, 0))] * 2
        args += list(rope)
    st_spec = pl.BlockSpec((None, None, None, RET_DK, RET_DV), lambda b, h: (b, 0, h, 0, 0))
    if states is not None:
        in_specs += [st_spec, st_spec]
        args += list(states)
    out_specs = [pl.BlockSpec((t, RET_DV), lambda b, h: (b, h))]
    out_shape = [jax.ShapeDtypeStruct((n_seq * t, nh * RET_DV), BF16)]
    if emit_state:
        out_specs += [st_spec, st_spec]
        out_shape += [jax.ShapeDtypeStruct((n_seq, 1, nh, RET_DK, RET_DV), F32)] * 2
    scratch = [pltpu.VMEM((t, RET_DK), F32), pltpu.VMEM((t, RET_DK), F32), pltpu.VMEM((t, RET_DV), BF16),
               pltpu.VMEM((t, RET_DV), F32), pltpu.VMEM((t, RET_DV), F32),
               pltpu.VMEM((RET_DK, RET_DV), F32), pltpu.VMEM((RET_DK, RET_DV), F32)]
    return pl.pallas_call(
        functools.partial(_ret_kernel, t=t, has_state=states is not None, emit_state=emit_state,
                          use_rope=rope is not None),
        grid=(n_seq, nh),
        in_specs=in_specs,
        out_specs=out_specs,
        out_shape=out_shape,
        scratch_shapes=scratch,
        compiler_params=_params("arbitrary", "arbitrary"),
        name="ret_lat" if states is not None else "ret_ctx",
    )(*args)


def _rope_tables(t):
    pairs = RET_DK // 4
    pos = jnp.arange(t, dtype=jnp.int32)
    rows = (pos // GRID_W).astype(F32)
    cols = (pos % GRID_W).astype(F32)
    inv = ROPE_BASE ** (-jnp.arange(pairs, dtype=F32) / pairs)
    ar = rows[:, None] * inv
    ac = cols[:, None] * inv
    cos = jnp.concatenate([jnp.cos(ar), jnp.cos(ar), jnp.cos(ac), jnp.cos(ac)], axis=1)
    sin = jnp.concatenate([-jnp.sin(ar), jnp.sin(ar), -jnp.sin(ac), jnp.sin(ac)], axis=1)
    return cos, sin


def _merge_kernel(oa_ref, ob_ref, ga_ref, gb_ref, wa_ref, wb_ref, o_ref):
    a = jnp.dot(oa_ref[...], wa_ref[...], preferred_element_type=F32)
    b = jnp.dot(ob_ref[...], wb_ref[...], preferred_element_type=F32)
    o_ref[...] = (_sigmoid(ga_ref[...]) * a + _sigmoid(gb_ref[...]) * b).astype(BF16)


def _merge(oa, ob, proj, wa_bf, wb_bf):
    t_all, ka = oa.shape
    kb = ob.shape[1]
    d = wa_bf.shape[1]
    tm = _tile(t_all, 512)
    tn = _tile(d, 1024)
    gate0 = (proj.shape[1] - 2 * d) // tn
    return pl.pallas_call(
        _merge_kernel,
        grid=(d // tn, t_all // tm),
        in_specs=[pl.BlockSpec((tm, ka), lambda j, i: (i, 0)),
                  pl.BlockSpec((tm, kb), lambda j, i: (i, 0)),
                  pl.BlockSpec((tm, tn), lambda j, i: (i, gate0 + j)),
                  pl.BlockSpec((tm, tn), lambda j, i: (i, gate0 + d // tn + j)),
                  pl.BlockSpec((ka, tn), lambda j, i: (0, j)),
                  pl.BlockSpec((kb, tn), lambda j, i: (0, j))],
        out_specs=pl.BlockSpec((tm, tn), lambda j, i: (i, j)),
        out_shape=jax.ShapeDtypeStruct((t_all, d), BF16),
        compiler_params=_params("arbitrary", "arbitrary"),
        name="merge",
    )(oa, ob, proj, proj, wa_bf, wb_bf)


def _pack_bf16_pairs(x):
    half = x.shape[1] // 2
    lo = lax.bitcast_convert_type(x[:, :half].astype(BF16).astype(F32), jnp.uint32)
    hi = lax.bitcast_convert_type(x[:, half:].astype(BF16).astype(F32), jnp.uint32)
    return (lo >> 16) | (hi & jnp.uint32(0xFFFF0000))


def _unpack_bf16_pairs(w):
    lo = lax.bitcast_convert_type(w << 16, F32).astype(BF16)
    hi = lax.bitcast_convert_type(w & jnp.uint32(0xFFFF0000), F32).astype(BF16)
    return lo, hi


def _router_kernel(m_ref, x_ref, g1_ref, sh_ref, sc_ref, ng_ref, wo_ref, rw_ref, rb_ref,
                   x1_ref, xp_ref, idx_ref, wgt_ref):
    out = jnp.dot(m_ref[...], wo_ref[...], preferred_element_type=F32)
    x1 = x_ref[...] + g1_ref[0] * out
    x1_ref[...] = x1
    ms = jnp.mean(x1 * x1, axis=-1, keepdims=True)
    xm = x1 * lax.rsqrt(ms + EPS) * ng_ref[...]
    xm = xm * (1.0 + sc_ref[0]) + sh_ref[0]
    xp_ref[...] = _pack_bf16_pairs(xm)

    logits = jnp.dot(xm, rw_ref[...], preferred_element_type=F32,
                     precision=lax.Precision.HIGHEST) + rb_ref[...]
    lane = lax.broadcasted_iota(jnp.int32, logits.shape, 1)
    lane_f = lane.astype(F32)
    idx_out = jnp.zeros(logits.shape, F32)
    val_out = jnp.zeros(logits.shape, F32)
    top = None
    for k in range(TOP_K):
        m = jnp.max(logits, axis=-1, keepdims=True)
        idx = jnp.min(jnp.where(logits == m, lane_f, float(LANES)), axis=-1, keepdims=True)
        if top is None:
            top = m
        idx_out = jnp.where(lane == k, idx, idx_out)
        val_out = jnp.where(lane == k, jnp.exp(m - top), val_out)
        logits = jnp.where(lane_f == idx, -jnp.inf, logits)
    idx_ref[...] = idx_out.astype(jnp.int32)
    wgt_ref[...] = val_out / jnp.sum(val_out, axis=-1, keepdims=True)


def _router(merged, x_all, mod3, norm_g, wo_bf, rw_pad, rb_pad, n_ctx_rows, t_lat):
    t_all, d = x_all.shape
    tm = _tile(min(n_ctx_rows, t_lat), 256)
    row = _mod_row_fn(tm, n_ctx_rows, t_lat)
    mod_spec = lambda k: pl.BlockSpec((1, 1, d), lambda i: (row(i) * N_MOD + k, 0, 0))
    return pl.pallas_call(
        _router_kernel,
        grid=(t_all // tm,),
        in_specs=[pl.BlockSpec((tm, d), lambda i: (i, 0)),
                  pl.BlockSpec((tm, d), lambda i: (i, 0)),
                  mod_spec(2), mod_spec(3), mod_spec(4),
                  pl.BlockSpec((1, d), lambda i: (0, 0)),
                  pl.BlockSpec((d, d), lambda i: (0, 0)),
                  pl.BlockSpec((d, LANES), lambda i: (0, 0)),
                  pl.BlockSpec((1, LANES), lambda i: (0, 0))],
        out_specs=[pl.BlockSpec((tm, d), lambda i: (i, 0)),
                   pl.BlockSpec((tm, d // 2), lambda i: (i, 0)),
                   pl.BlockSpec((tm, LANES), lambda i: (i, 0)),
                   pl.BlockSpec((tm, LANES), lambda i: (i, 0))],
        out_shape=[jax.ShapeDtypeStruct((t_all, d), F32),
                   jax.ShapeDtypeStruct((t_all, d // 2), jnp.uint32),
                   jax.ShapeDtypeStruct((t_all, LANES), jnp.int32),
                   jax.ShapeDtypeStruct((t_all, LANES), F32)],
        compiler_params=_params("arbitrary"),
        name="router",
    )(merged, x_all, mod3, mod3, mod3, norm_g.reshape(1, d), wo_bf, rw_pad, rb_pad)


def _moe_kernel(e_ref, start_ref, nsub_ref, tail_ref,
                xp_hbm, tok_hbm, wgu_ref, wdn_ref, bgu_ref, bdn_ref, perm_ref,
                ys_hbm,
                tok_smem, x_buf, y_acc, wgu_bf, wdn_bf, tok_sem, row_sem, out_sem,
                *, n_ff_tiles):
    i = pl.program_id(0)
    f = pl.program_id(1)
    nsub = nsub_ref[i]
    start = start_ref[i]
    g = MOE_GROUP
    d = y_acc.shape[1]
    half = d // 2
    ff2 = wgu_bf.shape[1]

    def row_copy(base, u):
        tok = tok_smem[lax.shift_right_logical(base, 7), (base & (LANES - 1)) + u]
        return pltpu.make_async_copy(xp_hbm.at[pl.ds(tok, 1), :], x_buf.at[pl.ds(base + u, 1), :], row_sem)

    @pl.when((f == 0) & (nsub > 0))
    def _():
        tok_copy = pltpu.make_async_copy(tok_hbm.at[pl.ds(start, tok_smem.shape[0]), :], tok_smem, tok_sem)
        tok_copy.start()
        tok_copy.wait()

        def issue(r, c):
            for u in range(ROW_UNROLL):
                row_copy(r * ROW_UNROLL, u).start()
            return c

        lax.fori_loop(0, nsub * (g // ROW_UNROLL), issue, 0)

        def init(sb, c):
            rows = pl.ds(pl.multiple_of(sb * g, g), g)
            y_acc[rows, :] = jnp.broadcast_to(bdn_ref[...], (g, d))
            return c

        lax.fori_loop(0, nsub, init, 0)

        def drain(r, c):
            for u in range(ROW_UNROLL):
                row_copy(r * ROW_UNROLL, u).wait()
            return c

        lax.fori_loop(0, nsub * (g // ROW_UNROLL), drain, 0)

    @pl.when(nsub > 0)
    def _():
        for j in range(ff2 // MXU_DIM):
            cols = slice(j * MXU_DIM, (j + 1) * MXU_DIM)
            wgu_bf[:, cols] = jnp.dot(wgu_ref[:, cols].astype(BF16), perm_ref[...],
                                      preferred_element_type=F32).astype(BF16)
        wdn_bf[...] = wdn_ref[...].astype(BF16)
        bgu = bgu_ref[...]

        def sub_block(sb, c):
            rows = pl.ds(pl.multiple_of(sb * g, g), g)
            x_lo, x_hi = _unpack_bf16_pairs(x_buf[rows, :])
            h = (jnp.dot(x_lo, wgu_bf[:half, :], preferred_element_type=F32)
                 + jnp.dot(x_hi, wgu_bf[half:, :], preferred_element_type=F32) + bgu)
            acts = []
            for j in range(ff2 // MXU_DIM):
                gate = jnp.minimum(h[:, j * MXU_DIM: j * MXU_DIM + LANES], SWIGLU_LIMIT)
                up = jnp.clip(h[:, j * MXU_DIM + LANES: (j + 1) * MXU_DIM], -SWIGLU_LIMIT, SWIGLU_LIMIT)
                acts.append(((up + 1.0) * gate * _sigmoid(SWIGLU_ALPHA * gate)).astype(BF16))
            act = jnp.concatenate(acts, axis=1)
            y_acc[rows, :] += jnp.dot(act, wdn_bf[...], preferred_element_type=F32)
            return c

        lax.fori_loop(0, nsub, sub_block, 0)

    @pl.when((f == n_ff_tiles - 1) & (nsub > 0))
    def _():
        def out_copy(sb):
            rows = pl.ds(pl.multiple_of(sb * g, g), g)
            dst = pl.ds(pl.multiple_of(start * LANES + sb * g, g), g)
            return pltpu.make_async_copy(y_acc.at[rows, :], ys_hbm.at[dst, :], out_sem)

        def issue(sb, c):
            out_copy(sb).start()
            return c

        def drain(sb, c):
            out_copy(sb).wait()
            return c

        lax.fori_loop(0, nsub, issue, 0)
        lax.fori_loop(0, nsub, drain, 0)

    @pl.when((i == pl.num_programs(0) - 1) & (f == n_ff_tiles - 1))
    def _():
        y_acc[0:g, :] = jnp.zeros((g, d), F32)
        first = tail_ref[0]
        n_tail = ys_hbm.shape[0] // g - first

        def tail_copy(k):
            dst = pl.ds(pl.multiple_of((first + k) * g, g), g)
            return pltpu.make_async_copy(y_acc.at[0:g, :], ys_hbm.at[dst, :], out_sem)

        def issue(k, c):
            tail_copy(k).start()
            return c

        def drain(k, c):
            tail_copy(k).wait()
            return c

        lax.fori_loop(0, n_tail, issue, 0)
        lax.fori_loop(0, n_tail, drain, 0)


def _moe(xp, tok2d, item_e, item_start, item_nsub, tail, w_gu, w_dn, b_gu_perm, b_dn, perm, n_rows):
    n_exp, d, ff2_all = w_gu.shape
    ff = ff2_all // 2
    tf = _tile(ff, MOE_FF_TILE)
    n_f = ff // tf
    n_items = item_e.shape[0]

    def f_eff(i, f, nsub):
        return jnp.where(nsub[i] > 0, f, n_f - 1)

    grid_spec = pltpu.PrefetchScalarGridSpec(
        num_scalar_prefetch=4,
        grid=(n_items, n_f),
        in_specs=[pl.BlockSpec(memory_space=pl.ANY),
                  pl.BlockSpec(memory_space=pl.ANY),
                  pl.BlockSpec((None, d, 2 * tf), lambda i, f, e, s, n, t: (e[i], 0, f_eff(i, f, n))),
                  pl.BlockSpec((None, tf, d), lambda i, f, e, s, n, t: (e[i], f_eff(i, f, n), 0)),
                  pl.BlockSpec((None, 1, 2 * tf), lambda i, f, e, s, n, t: (e[i], 0, f_eff(i, f, n))),
                  pl.BlockSpec((None, 1, d), lambda i, f, e, s, n, t: (e[i], 0, 0)),
                  pl.BlockSpec((MXU_DIM, MXU_DIM), lambda i, f, e, s, n, t: (0, 0))],
        out_specs=pl.BlockSpec(memory_space=pl.ANY),
        scratch_shapes=[pltpu.SMEM((MOE_ITEM_ROWS // LANES, LANES), jnp.int32),
                        pltpu.VMEM((MOE_ITEM_ROWS, d // 2), jnp.uint32),
                        pltpu.VMEM((MOE_ITEM_ROWS, d), F32),
                        pltpu.VMEM((d, 2 * tf), BF16),
                        pltpu.VMEM((tf, d), BF16),
                        pltpu.SemaphoreType.DMA(()),
                        pltpu.SemaphoreType.DMA(()),
                        pltpu.SemaphoreType.DMA(())],
    )
    return pl.pallas_call(
        functools.partial(_moe_kernel, n_ff_tiles=n_f),
        grid_spec=grid_spec,
        out_shape=jax.ShapeDtypeStruct((n_rows, d), F32),
        compiler_params=_params("arbitrary", "arbitrary"),
        name="moe",
    )(item_e, item_start, item_nsub, tail, xp, tok2d, w_gu, w_dn,
      b_gu_perm.reshape(n_exp, 1, ff2_all), b_dn.reshape(n_exp, 1, d), perm)


def _moe_plan(top_idx, n_exp):
    t_all = top_idx.shape[0]
    n_assign = t_all * TOP_K
    g = MOE_GROUP
    rows_max = n_assign + n_exp * g
    n_items = n_exp + rows_max // MOE_ITEM_ROWS
    n_rows = rows_max + MOE_ITEM_ROWS

    flat_e = top_idx.reshape(-1).astype(jnp.int32)
    assign = jnp.arange(n_assign, dtype=jnp.int32)
    e_sorted, order = lax.sort_key_val(flat_e, assign)
    experts = jnp.arange(n_exp, dtype=jnp.int32)
    first = jnp.sum(e_sorted[:, None] < experts[None, :], axis=0, dtype=jnp.int32)
    counts = jnp.concatenate([first[1:], jnp.full((1,), n_assign, jnp.int32)]) - first
    padded = (counts + g - 1) // g * g
    pad_end = jnp.cumsum(padded)
    pad_start = pad_end - padded
    dest_sorted = pad_start[e_sorted] + assign - first[e_sorted]
    _, dest = lax.sort_key_val(order, dest_sorted)
    rows = jnp.arange(n_rows, dtype=jnp.int32)
    e_row = jnp.minimum(jnp.sum(rows[:, None] >= pad_end[None, :], axis=1, dtype=jnp.int32), n_exp - 1)
    in_group = rows - pad_start[e_row]
    src = jnp.clip(first[e_row] + in_group, 0, n_assign - 1)
    row_tok = jnp.where((in_group < counts[e_row]) & (rows < pad_end[-1]), order[src] // TOP_K, 0)

    chunks = (padded + MOE_ITEM_ROWS - 1) // MOE_ITEM_ROWS
    chunk_end = jnp.cumsum(chunks)
    items = jnp.arange(n_items, dtype=jnp.int32)
    n_used = chunk_end[-1]
    last_e = jnp.max(jnp.where(chunks > 0, jnp.arange(n_exp, dtype=jnp.int32), 0))
    e_of = jnp.minimum(jnp.searchsorted(chunk_end, items, side="right"), n_exp - 1).astype(jnp.int32)
    local = items - (chunk_end[e_of] - chunks[e_of])
    used = items < n_used
    item_e = jnp.where(used, e_of, last_e).astype(jnp.int32)
    left = padded[e_of] - local * MOE_ITEM_ROWS
    item_nsub = jnp.where(used, jnp.clip(left, 0, MOE_ITEM_ROWS) // g, 0).astype(jnp.int32)
    item_start = jnp.where(used, (pad_start[e_of] + local * MOE_ITEM_ROWS) // LANES, 0).astype(jnp.int32)
    tail = (jnp.sum(padded) // g).astype(jnp.int32).reshape(1)
    return dest, row_tok.reshape(n_rows // LANES, LANES), item_e, item_start, item_nsub, tail, n_rows


def _combine_kernel(dest_ref, ys_hbm, wgt_ref, x1_ref, g2_ref, fg_ref, o_ref, gbuf, sem, *, tm):
    tok_per_trip = ROW_UNROLL // TOP_K

    def row_copy(trip, u):
        base = trip * ROW_UNROLL
        d = dest_ref[lax.shift_right_logical(base, 7), (base & (LANES - 1)) + u]
        slot = (u % TOP_K) * tm + trip * tok_per_trip + u // TOP_K
        return pltpu.make_async_copy(ys_hbm.at[pl.ds(d, 1), :], gbuf.at[pl.ds(slot, 1), :], sem)

    def issue(trip, c):
        for u in range(ROW_UNROLL):
            row_copy(trip, u).start()
        return c

    def drain(trip, c):
        for u in range(ROW_UNROLL):
            row_copy(trip, u).wait()
        return c

    lax.fori_loop(0, tm * TOP_K // ROW_UNROLL, issue, 0)
    lax.fori_loop(0, tm * TOP_K // ROW_UNROLL, drain, 0)

    w = wgt_ref[...]
    y = w[:, 0:1] * gbuf[0:tm, :]
    for k in range(1, TOP_K):
        y = y + w[:, k:k + 1] * gbuf[k * tm:(k + 1) * tm, :]
    x = x1_ref[...] + g2_ref[0] * y
    ms = jnp.mean(x * x, axis=-1, keepdims=True)
    o_ref[...] = x * lax.rsqrt(ms + EPS) * fg_ref[...]


def _combine(dest2d, ys, wgt, x1, mod3, final_g, n_ctx_rows, t_lat):
    t_all, d = x1.shape
    tm = _tile(min(n_ctx_rows, t_lat), 256)
    row = _mod_row_fn(tm, n_ctx_rows, t_lat)
    idx_rows = tm * TOP_K // LANES
    return pl.pallas_call(
        functools.partial(_combine_kernel, tm=tm),
        grid=(t_all // tm,),
        in_specs=[pl.BlockSpec((idx_rows, LANES), lambda i: (i, 0), memory_space=pltpu.SMEM),
                  pl.BlockSpec(memory_space=pl.ANY),
                  pl.BlockSpec((tm, LANES), lambda i: (i, 0)),
                  pl.BlockSpec((tm, d), lambda i: (i, 0)),
                  pl.BlockSpec((1, 1, d), lambda i: (row(i) * N_MOD + 5, 0, 0)),
                  pl.BlockSpec((1, d), lambda i: (0, 0))],
        out_specs=pl.BlockSpec((tm, d), lambda i: (i, 0)),
        out_shape=jax.ShapeDtypeStruct((t_all, d), F32),
        scratch_shapes=[pltpu.VMEM((tm * TOP_K, d), F32), pltpu.SemaphoreType.DMA(())],
        compiler_params=_params("arbitrary"),
        name="combine",
    )(dest2d, ys, wgt, x1, mod3, final_g.reshape(1, d))


def kernel(x_prompt, x_sample, state_hgrn_fwd, state_hgrn_bwd, state_ret_fwd, state_ret_bwd, c, c_ctx, ada_w, ada_b, norm1_g, norm2_g, final_norm_g, w_in, hg_lb_fwd, hg_lb_bwd, hg_norm_g, ret_log2_fwd, ret_log2_bwd, ret_norm_g, w_proj_hgrn, w_proj_ret, w_out, router_w, router_b, moe_w_gu, moe_b_gu, moe_w_dn, moe_b_dn):
    assert ada_w.shape[0] == 1, "one trunk layer"
    b_ctx, t_ctx, d = x_prompt.shape
    b_lat, t_lat, _ = x_sample.shape
    n_ctx_rows = b_ctx * t_ctx
    assert n_ctx_rows % t_lat == 0 and b_lat + 1 <= COND_ROWS
    n_exp = router_w.shape[2]

    x_all = jnp.concatenate([x_prompt.reshape(n_ctx_rows, d), x_sample.reshape(b_lat * t_lat, d)], axis=0)
    cond = jnp.zeros((COND_ROWS, d), F32).at[0].set(c_ctx).at[1:1 + b_lat].set(c)
    mod3 = _adaln(cond, ada_w[0], ada_b[0]).reshape(COND_ROWS * N_MOD, 1, d)

    proj = _inproj(x_all, mod3, norm1_g[0], w_in[0].astype(BF16), n_ctx_rows, t_lat)

    hg_args = (proj, hg_lb_fwd, hg_lb_bwd, hg_norm_g[0])
    oa_ctx, new_hf, new_hb = _hgrn(*hg_args, None, n_seq=b_ctx, t=t_ctx, row_off=0, emit_state=True)
    (oa_lat,) = _hgrn(*hg_args, (state_hgrn_fwd, state_hgrn_bwd), n_seq=b_lat, t=t_lat,
                      row_off=n_ctx_rows // t_lat, emit_state=False)

    p_f = jnp.broadcast_to(ret_log2_fwd[0][:, None, None], (RET_HEADS, 1, RET_DV))
    p_b = jnp.broadcast_to(ret_log2_bwd[0][:, None, None], (RET_HEADS, 1, RET_DV))
    ret_args = (proj, p_f, p_b, ret_norm_g[0])
    ob_ctx, new_rf, new_rb = _ret(*ret_args, None, None, n_seq=b_ctx, t=t_ctx, row_off=0, emit_state=True)
    (ob_lat,) = _ret(*ret_args, _rope_tables(t_lat), (state_ret_fwd, state_ret_bwd), n_seq=b_lat, t=t_lat,
                     row_off=n_ctx_rows // t_lat, emit_state=False)

    oa = jnp.concatenate([oa_ctx, oa_lat], axis=0)
    ob = jnp.concatenate([ob_ctx, ob_lat], axis=0)
    merged = _merge(oa, ob, proj, w_proj_hgrn[0].astype(BF16), w_proj_ret[0].astype(BF16))

    rw_pad = jnp.zeros((d, LANES), F32).at[:, :n_exp].set(router_w[0])
    rb_pad = jnp.full((1, LANES), -1e30, F32).at[0, :n_exp].set(router_b[0])
    x1, xp, top_idx, top_w = _router(merged, x_all, mod3, norm2_g[0], w_out[0].astype(BF16),
                                     rw_pad, rb_pad, n_ctx_rows, t_lat)

    dest, tok2d, item_e, item_start, item_nsub, tail, n_rows = _moe_plan(top_idx[:, :TOP_K], n_exp)
    ff = moe_w_dn.shape[2]
    b_gu_perm = moe_b_gu[0].reshape(n_exp, ff // LANES, LANES, 2).transpose(0, 1, 3, 2).reshape(n_exp, 2 * ff)
    src = jnp.arange(MXU_DIM)
    perm = jnp.zeros((MXU_DIM, MXU_DIM), BF16).at[src, (src % 2) * LANES + src // 2].set(1)
    ys = _moe(xp, tok2d, item_e, item_start, item_nsub, tail, moe_w_gu[0], moe_w_dn[0], b_gu_perm, moe_b_dn[0],
              perm, n_rows)

    y_all = _combine(dest.reshape(-1, LANES), ys, top_w, x1, mod3, final_norm_g, n_ctx_rows, t_lat)
    y_prompt = y_all[:n_ctx_rows].reshape(b_ctx, t_ctx, d)
    y_sample = y_all[n_ctx_rows:].reshape(b_lat, t_lat, d)
    return (y_prompt, y_sample, new_hf, new_hb, new_rf, new_rb)
```

```python
import functools

import jax
import jax.numpy as jnp
from jax import lax
from jax.experimental import pallas as pl
from jax.experimental.pallas import tpu as pltpu

F32 = jnp.float32
BF16 = jnp.bfloat16

HG_HEADS = 8
HG_DK = 128
HG_DV = 128
RET_HEADS = 8
RET_DK = 128
RET_DV = 256
CHUNK = 32
GRID_W = 64
ROPE_BASE = 10000.0
TOP_K = 4
SWIGLU_LIMIT = 7.0
SWIGLU_ALPHA = 1.702
N_MOD = 6
EPS = 1e-6

LANES = 128
MXU_DIM = 256
VMEM_LIMIT = 56 * 1024 * 1024
COND_ROWS = 16
SCAN_BLOCK = 256
SCAN_UNROLL = 4
HGRN_HEADS_PER_STEP = 4
RET_HEADS_PER_STEP = 4
RET_ROWS_PER_STEP = 2048
MOE_GROUP = 256
MOE_ITEM_ROWS = 2048
MOE_FF_TILE = 256
ROW_UNROLL = 8


def _sigmoid(x):
    return 0.5 * jnp.tanh(0.5 * x) + 0.5


def _params(*sem):
    return pltpu.CompilerParams(dimension_semantics=sem, vmem_limit_bytes=VMEM_LIMIT)


def _tile(n, pref):
    if n <= pref:
        return n
    for t in range(pref - pref % LANES, 0, -LANES):
        if n % t == 0:
            return t
    raise ValueError((n, pref))


def _adaln_kernel(c_ref, w_ref, b_ref, o_ref):
    c = c_ref[...]
    s = (c * _sigmoid(c)).astype(BF16)
    o_ref[...] = jnp.dot(s, w_ref[...].astype(BF16), preferred_element_type=F32) + b_ref[...]


def _adaln(cond, ada_w, ada_b):
    d, n = ada_w.shape
    tn = _tile(n, 1024)
    return pl.pallas_call(
        _adaln_kernel,
        grid=(n // tn,),
        in_specs=[pl.BlockSpec((COND_ROWS, d), lambda j: (0, 0)),
                  pl.BlockSpec((d, tn), lambda j: (0, j)),
                  pl.BlockSpec((1, tn), lambda j: (0, j))],
        out_specs=pl.BlockSpec((COND_ROWS, tn), lambda j: (0, j)),
        out_shape=jax.ShapeDtypeStruct((COND_ROWS, n), F32),
        compiler_params=_params("arbitrary"),
        name="adaln",
    )(cond, ada_w, ada_b.reshape(1, n))


def _prenorm_kernel(xc_ref, xl_ref, sh_ref, sc_ref, g_ref, o_ref, *, n_ctx_tiles):
    x = jnp.where(pl.program_id(0) < n_ctx_tiles, xc_ref[...], xl_ref[...])
    ms = jnp.mean(x * x, axis=-1, keepdims=True)
    xn = x * lax.rsqrt(ms + EPS) * g_ref[...]
    o_ref[...] = (xn * (1.0 + sc_ref[0]) + sh_ref[0]).astype(BF16)


def _inproj_kernel(xm_ref, w_ref, o_ref):
    o_ref[...] = jnp.dot(xm_ref[...], w_ref[...].astype(BF16), preferred_element_type=F32)


def _mod_row_fn(tm, n_ctx_rows, t_lat):
    n_ctx_tiles = n_ctx_rows // tm

    def row(i):
        return jnp.where(i < n_ctx_tiles, 0, 1 + ((i - n_ctx_tiles) * tm) // t_lat)

    return row


def _group_specs(tm, width, n_ctx_tiles):
    ctx = lambda i: (jnp.minimum(i, n_ctx_tiles - 1), 0)
    lat = lambda i: (jnp.maximum(i - n_ctx_tiles, 0), 0)
    return [pl.BlockSpec((tm, width), ctx), pl.BlockSpec((tm, width), lat)]


def _inproj(x_ctx, x_lat, mod3, norm_g, w_in, t_lat):
    n_ctx_rows, d = x_ctx.shape
    t_all = n_ctx_rows + x_lat.shape[0]
    n = w_in.shape[1]
    tp = _tile(min(n_ctx_rows, t_lat), 512)
    row = _mod_row_fn(tp, n_ctx_rows, t_lat)
    xm = pl.pallas_call(
        functools.partial(_prenorm_kernel, n_ctx_tiles=n_ctx_rows // tp),
        grid=(t_all // tp,),
        in_specs=_group_specs(tp, d, n_ctx_rows // tp) + [
            pl.BlockSpec((1, 1, d), lambda i: (row(i) * N_MOD + 0, 0, 0)),
            pl.BlockSpec((1, 1, d), lambda i: (row(i) * N_MOD + 1, 0, 0)),
            pl.BlockSpec((1, d), lambda i: (0, 0))],
        out_specs=pl.BlockSpec((tp, d), lambda i: (i, 0)),
        out_shape=jax.ShapeDtypeStruct((t_all, d), BF16),
        compiler_params=_params("arbitrary"),
        name="prenorm",
    )(x_ctx, x_lat, mod3, mod3, norm_g.reshape(1, d))
    tm = _tile(t_all, 1024)
    tn = _tile(n, 1024)
    return pl.pallas_call(
        _inproj_kernel,
        grid=(t_all // tm, n // tn),
        in_specs=[pl.BlockSpec((tm, d), lambda i, j: (i, 0)),
                  pl.BlockSpec((d, tn), lambda i, j: (0, j))],
        out_specs=pl.BlockSpec((tm, tn), lambda i, j: (i, j)),
        out_shape=jax.ShapeDtypeStruct((t_all, n), F32),
        compiler_params=_params("arbitrary", "arbitrary"),
        name="inproj",
    )(xm, w_in)


def _dot_split(m_bf, x):
    hi = x.astype(BF16)
    lo = (x - hi.astype(F32)).astype(BF16)
    return jnp.dot(m_bf, hi, preferred_element_type=F32) + jnp.dot(m_bf, lo, preferred_element_type=F32)


_NT = (((1,), (1,)), ((), ()))
_TN = (((0,), (0,)), ((), ()))


def _hgrn_kernel(*refs, t, hpg, has_state, emit_state):
    hq_ref, hzf_ref, hzb_ref, hi_ref, hgate_ref, lbf_ref, lbb_ref, gn_ref = refs[:8]
    pos = 8
    if has_state:
        s0f_ref, s0b_ref = refs[pos:pos + 2]
        pos += 2
    out_ref = refs[pos]
    pos += 1
    if emit_state:
        sf_ref, sb_ref = refs[pos:pos + 2]
        pos += 2
    (q_in_f, k_end_f, dec_f, q_in_b, k_end_b, dec_b, v_bf, o_f, o_b, st_ref) = refs[pos:]
    heads = [slice(j * LANES, (j + 1) * LANES) for j in range(hpg)]

    blk = min(SCAN_BLOCK, t)
    n_blk = t // blk
    n_chunk = t // CHUNK

    r = lax.broadcasted_iota(jnp.int32, (blk, blk), 0)
    c = lax.broadcasted_iota(jnp.int32, (blk, blk), 1)
    same = (r // CHUNK) == (c // CHUNK)
    low = same & (c <= r)
    upp = same & (c >= r)
    m_pre = jnp.where(low, 1.0, 0.0).astype(BF16)
    m_suf = jnp.where(upp, 1.0, 0.0).astype(BF16)

    def lower_bound(lb_ref):
        a = lb_ref[...]
        e = jnp.exp(a - jnp.max(a, axis=0, keepdims=True))
        return e[0:1] / jnp.sum(e, axis=0, keepdims=True)

    lb_f = lower_bound(lbf_ref)
    lb_b = lower_bound(lbb_ref)

    width = hpg * LANES

    def chunk_total(g, reverse):
        g3 = g.reshape(blk // CHUNK, CHUNK, width)
        edge = g3[:, 0:1, :] if reverse else g3[:, CHUNK - 1:CHUNK, :]
        return jnp.broadcast_to(edge, g3.shape).reshape(blk, width)

    directions = ((hzf_ref, lb_f, m_pre, low, False, q_in_f, k_end_f, dec_f, o_f),
                  (hzb_ref, lb_b, m_suf, upp, True, q_in_b, k_end_b, dec_b, o_b))

    def prep(i, carry):
        rows = pl.ds(pl.multiple_of(i * blk, blk), blk)
        hq = hq_ref[rows, :]
        q = hq * _sigmoid(hq)
        v = hi_ref[rows, :].astype(BF16)
        v_bf[rows, :] = v
        for z_ref, lb, m_cum, mask, reverse, qi, ke, dc, o_ref in directions:
            f = lb + (1.0 - lb) * _sigmoid(z_ref[rows, :])
            k = 1.0 - f
            g = _dot_split(m_cum, jnp.log(f))
            tot = chunk_total(g, reverse)
            q_in = (q * jnp.exp(g)).astype(BF16)
            k_in = (k * jnp.exp(-g)).astype(BF16)
            qi[rows, :] = q_in
            ke[rows, :] = (k * jnp.exp(tot - g)).astype(BF16)
            dc[rows, :] = jnp.exp(tot)
            for hd in heads:
                s = lax.dot_general(q_in[:, hd], k_in[:, hd], _NT, preferred_element_type=F32)
                o_ref[rows, hd] = jnp.dot(jnp.where(mask, s, 0.0).astype(BF16), v[:, hd],
                                          preferred_element_type=F32)
        return carry

    lax.fori_loop(0, n_blk, prep, 0)

    for j in range(hpg):
        for slot, s0_ref in ((2 * j, s0f_ref if has_state else None), (2 * j + 1, s0b_ref if has_state else None)):
            st_ref[slot] = s0_ref[j].T if has_state else jnp.zeros((HG_DV, HG_DK), F32)

    def chunk_step(idx, slot, hd, qi, ke, dc, o_ref):
        start = pl.multiple_of(idx * CHUNK, CHUNK)
        rows = pl.ds(start, CHUNK)
        st = st_ref[slot]
        o_ref[rows, hd] += lax.dot_general(qi[rows, hd], st.astype(BF16), _NT, preferred_element_type=F32)
        upd = lax.dot_general(v_bf[rows, hd], ke[rows, hd], _TN, preferred_element_type=F32)
        st_ref[slot] = dc[pl.ds(start, 1), hd] * st + upd

    def body(i, carry):
        for j, hd in enumerate(heads):
            chunk_step(i, 2 * j, hd, q_in_f, k_end_f, dec_f, o_f)
            chunk_step(n_chunk - 1 - i, 2 * j + 1, hd, q_in_b, k_end_b, dec_b, o_b)
        return carry

    lax.fori_loop(0, n_chunk, body, 0, unroll=min(n_chunk, SCAN_UNROLL))
    if emit_state:
        for j in range(hpg):
            sf_ref[j] = st_ref[2 * j].T
            sb_ref[j] = st_ref[2 * j + 1].T

    for hd in heads:
        o = o_f[:, hd] + o_b[:, hd]
        o = o * lax.rsqrt(jnp.mean(o * o, axis=-1, keepdims=True) + EPS) * gn_ref[...]
        out_ref[:, hd] = (o * _sigmoid(hgate_ref[:, hd])).astype(BF16)


def _hgrn(proj, lb_fwd, lb_bwd, norm_g, states, *, n_seq, t, row_off, emit_state):
    nh = HG_HEADS
    hpg = min(nh, HGRN_HEADS_PER_STEP)
    assert nh % hpg == 0
    ng = nh // hpg
    w = hpg * LANES
    col = lambda k: (lambda b, h: (row_off + b, k * ng + h))
    blk = lambda k: pl.BlockSpec((t, w), col(k))
    in_specs = [blk(0), blk(1), blk(2), blk(3), blk(4),
                pl.BlockSpec((2, w), lambda b, h: (0, h)),
                pl.BlockSpec((2, w), lambda b, h: (0, h)),
                pl.BlockSpec((1, LANES), lambda b, h: (0, 0))]
    args = [proj] * 5 + [lb_fwd, lb_bwd, norm_g.reshape(1, HG_DV)]
    st_spec = pl.BlockSpec((None, None, hpg, HG_DK, HG_DV), lambda b, h: (b, 0, h, 0, 0))
    if states is not None:
        in_specs += [st_spec, st_spec]
        args += list(states)
    out_specs = [pl.BlockSpec((t, w), lambda b, h: (b, h))]
    out_shape = [jax.ShapeDtypeStruct((n_seq * t, nh * HG_DV), BF16)]
    if emit_state:
        out_specs += [st_spec, st_spec]
        out_shape += [jax.ShapeDtypeStruct((n_seq, 1, nh, HG_DK, HG_DV), F32)] * 2
    scratch = ([pltpu.VMEM((t, w), BF16)] * 2 + [pltpu.VMEM((t, w), F32)]) * 2
    scratch += [pltpu.VMEM((t, w), BF16), pltpu.VMEM((t, w), F32), pltpu.VMEM((t, w), F32),
                pltpu.VMEM((2 * hpg, HG_DV, HG_DK), F32)]
    return pl.pallas_call(
        functools.partial(_hgrn_kernel, t=t, hpg=hpg, has_state=states is not None, emit_state=emit_state),
        grid=(n_seq, ng),
        in_specs=in_specs,
        out_specs=out_specs,
        out_shape=out_shape,
        scratch_shapes=scratch,
        compiler_params=_params("arbitrary", "arbitrary"),
        name="hgrn_lat" if states is not None else "hgrn_ctx",
    )(*args)


def _ret_kernel(*refs, t, hpg, has_state, emit_state, use_rope):
    rq_ref, rk_ref, rv_ref, rgate_ref, pf_ref, pb_ref, gn_ref = refs[:7]
    pos = 7
    if use_rope:
        cos_ref, sin_ref = refs[pos:pos + 2]
        pos += 2
    if has_state:
        s0f_ref, s0b_ref = refs[pos:pos + 2]
        pos += 2
    out_ref = refs[pos]
    pos += 1
    if emit_state:
        sf_ref, sb_ref = refs[pos:pos + 2]
        pos += 2
    q_s, k_s, v_s, o_f, o_b, st_ref = refs[pos:]

    ch = min(MXU_DIM, t)
    n_chunk = t // ch

    lane = lax.broadcasted_iota(jnp.int32, (t, RET_DK), 1)
    first_half = (lane % 64) < 32

    def rope(x):
        if not use_rope:
            return x
        swapped = jnp.where(first_half, pltpu.roll(x, RET_DK - 32, 1), pltpu.roll(x, 32, 1))
        return x * cos_ref[...] + swapped * sin_ref[...]

    ri = lax.broadcasted_iota(jnp.int32, (ch, ch), 0)
    ci = lax.broadcasted_iota(jnp.int32, (ch, ch), 1)
    row128 = lax.broadcasted_iota(jnp.int32, (ch, RET_DK), 0).astype(F32)

    def direction(j, p_ref, slot, o_scr, s0_ref, reverse):
        qk = slice(j * RET_DK, (j + 1) * RET_DK)
        vv = slice(j * RET_DV, (j + 1) * RET_DV)
        lg = jnp.log1p(-jnp.exp2(p_ref[j]))
        lg_k = lg[:, :RET_DK]
        dist = (ci - ri) if reverse else (ri - ci)
        keep = dist >= 0
        dmask = jnp.where(keep, jnp.exp(lg[:, :ch] * jnp.where(keep, dist, 0).astype(F32)), 0.0)
        if reverse:
            q_scale = jnp.exp(lg_k * (ch - row128))
            k_scale = jnp.exp(lg_k * row128)
        else:
            q_scale = jnp.exp(lg_k * (row128 + 1.0))
            k_scale = jnp.exp(lg_k * (ch - 1.0 - row128))
        dec = jnp.exp(lg * float(ch))
        st_ref[slot] = s0_ref[j] if has_state else jnp.zeros((RET_DK, RET_DV), F32)
        for step in range(n_chunk):
            cidx = n_chunk - 1 - step if reverse else step
            rows = pl.ds(cidx * ch, ch)
            q = q_s[rows, qk]
            k = k_s[rows, qk]
            v = v_s[rows, vv]
            s = lax.dot_general(q.astype(BF16), k.astype(BF16), _NT, preferred_element_type=F32)
            o = jnp.dot((s * dmask).astype(BF16), v, preferred_element_type=F32)
            st = st_ref[slot]
            o = o + jnp.dot((q * q_scale).astype(BF16), st.astype(BF16), preferred_element_type=F32)
            o_scr[rows, vv] = o
            upd = lax.dot_general((k * k_scale).astype(BF16), v, _TN, preferred_element_type=F32)
            st_ref[slot] = dec * st + upd

    for j in range(hpg):
        qk = slice(j * RET_DK, (j + 1) * RET_DK)
        q_s[:, qk] = rope(rq_ref[:, qk])
        k_s[:, qk] = rope(rk_ref[:, qk] * (RET_DK ** -0.5))
    v_s[...] = rv_ref[...].astype(BF16)
    for j in range(hpg):
        direction(j, pf_ref, 2 * j, o_f, s0f_ref if has_state else None, False)
        direction(j, pb_ref, 2 * j + 1, o_b, s0b_ref if has_state else None, True)
    if emit_state:
        for j in range(hpg):
            sf_ref[j] = st_ref[2 * j]
            sb_ref[j] = st_ref[2 * j + 1]

    for j in range(hpg):
        vv = slice(j * RET_DV, (j + 1) * RET_DV)
        o = o_f[:, vv] + o_b[:, vv]
        oc = o - jnp.mean(o, axis=-1, keepdims=True)
        o = oc * lax.rsqrt(jnp.mean(oc * oc, axis=-1, keepdims=True) + EPS) * gn_ref[...]
        gate = rgate_ref[:, vv]
        out_ref[:, vv] = (o * (gate * _sigmoid(gate))).astype(BF16)


def _ret(proj, p_fwd, p_bwd, norm_g, rope, states, *, n_seq, t, row_off, emit_state):
    nh = RET_HEADS
    base = (3 * HG_HEADS * HG_DK + 2 * HG_HEADS * HG_DV)
    hpg = max(1, min(nh, RET_HEADS_PER_STEP, RET_ROWS_PER_STEP // t))
    while nh % hpg or base % (hpg * RET_DK) or (base + 2 * nh * RET_DK) % (hpg * RET_DV):
        hpg -= 1
    ng = nh // hpg
    wk, wv = hpg * RET_DK, hpg * RET_DV
    qk0 = base // wk
    v0 = (base + 2 * nh * RET_DK) // wv
    in_specs = [pl.BlockSpec((t, wk), lambda b, h: (row_off + b, qk0 + h)),
                pl.BlockSpec((t, wk), lambda b, h: (row_off + b, qk0 + ng + h)),
                pl.BlockSpec((t, wv), lambda b, h: (row_off + b, v0 + h)),
                pl.BlockSpec((t, wv), lambda b, h: (row_off + b, v0 + ng + h)),
                pl.BlockSpec((hpg, 1, RET_DV), lambda b, h: (h, 0, 0)),
                pl.BlockSpec((hpg, 1, RET_DV), lambda b, h: (h, 0, 0)),
                pl.BlockSpec((1, RET_DV), lambda b, h: (0, 0))]
    args = [proj] * 4 + [p_fwd, p_bwd, norm_g.reshape(1, RET_DV)]
    if rope is not None:
        in_specs += [pl.BlockSpec((t, RET_DK), lambda b, h: (0, 0))] * 2
        args += list(rope)
    st_spec = pl.BlockSpec((None, None, hpg, RET_DK, RET_DV), lambda b, h: (b, 0, h, 0, 0))
    if states is not None:
        in_specs += [st_spec, st_spec]
        args += list(states)
    out_specs = [pl.BlockSpec((t, wv), lambda b, h: (b, h))]
    out_shape = [jax.ShapeDtypeStruct((n_seq * t, nh * RET_DV), BF16)]
    if emit_state:
        out_specs += [st_spec, st_spec]
        out_shape += [jax.ShapeDtypeStruct((n_seq, 1, nh, RET_DK, RET_DV), F32)] * 2
    scratch = [pltpu.VMEM((t, wk), F32), pltpu.VMEM((t, wk), F32), pltpu.VMEM((t, wv), BF16),
               pltpu.VMEM((t, wv), F32), pltpu.VMEM((t, wv), F32),
               pltpu.VMEM((2 * hpg, RET_DK, RET_DV), F32)]
    return pl.pallas_call(
        functools.partial(_ret_kernel, t=t, hpg=hpg, has_state=states is not None, emit_state=emit_state,
                          use_rope=rope is not None),
        grid=(n_seq, ng),
        in_specs=in_specs,
        out_specs=out_specs,
        out_shape=out_shape,
        scratch_shapes=scratch,
        compiler_params=_params("arbitrary", "arbitrary"),
        name="ret_lat" if states is not None else "ret_ctx",
    )(*args)


def _rope_tables(t):
    pairs = RET_DK // 4
    pos = jnp.arange(t, dtype=jnp.int32)
    rows = (pos // GRID_W).astype(F32)
    cols = (pos % GRID_W).astype(F32)
    inv = ROPE_BASE ** (-jnp.arange(pairs, dtype=F32) / pairs)
    ar = rows[:, None] * inv
    ac = cols[:, None] * inv
    cos = jnp.concatenate([jnp.cos(ar), jnp.cos(ar), jnp.cos(ac), jnp.cos(ac)], axis=1)
    sin = jnp.concatenate([-jnp.sin(ar), jnp.sin(ar), -jnp.sin(ac), jnp.sin(ac)], axis=1)
    return cos, sin


def _merge_kernel(oac_ref, oal_ref, obc_ref, obl_ref, ga_ref, gb_ref, wa_ref, wb_ref, o_ref, *, n_ctx_tiles):
    is_ctx = pl.program_id(1) < n_ctx_tiles
    oa = jnp.where(is_ctx, oac_ref[...], oal_ref[...])
    ob = jnp.where(is_ctx, obc_ref[...], obl_ref[...])
    a = jnp.dot(oa, wa_ref[...], preferred_element_type=F32)
    b = jnp.dot(ob, wb_ref[...], preferred_element_type=F32)
    o_ref[...] = (_sigmoid(ga_ref[...]) * a + _sigmoid(gb_ref[...]) * b).astype(BF16)


def _merge(oa_ctx, oa_lat, ob_ctx, ob_lat, proj, wa_bf, wb_bf):
    n_ctx_rows, ka = oa_ctx.shape
    t_all = n_ctx_rows + oa_lat.shape[0]
    kb = ob_ctx.shape[1]
    d = wa_bf.shape[1]
    tm = _tile(min(n_ctx_rows, oa_lat.shape[0]), 512)
    tn = _tile(d, 1024)
    nct = n_ctx_rows // tm
    gate0 = (proj.shape[1] - 2 * d) // tn
    ctx = lambda j, i: (jnp.minimum(i, nct - 1), 0)
    lat = lambda j, i: (jnp.maximum(i - nct, 0), 0)
    return pl.pallas_call(
        functools.partial(_merge_kernel, n_ctx_tiles=nct),
        grid=(d // tn, t_all // tm),
        in_specs=[pl.BlockSpec((tm, ka), ctx), pl.BlockSpec((tm, ka), lat),
                  pl.BlockSpec((tm, kb), ctx), pl.BlockSpec((tm, kb), lat),
                  pl.BlockSpec((tm, tn), lambda j, i: (i, gate0 + j)),
                  pl.BlockSpec((tm, tn), lambda j, i: (i, gate0 + d // tn + j)),
                  pl.BlockSpec((ka, tn), lambda j, i: (0, j)),
                  pl.BlockSpec((kb, tn), lambda j, i: (0, j))],
        out_specs=pl.BlockSpec((tm, tn), lambda j, i: (i, j)),
        out_shape=jax.ShapeDtypeStruct((t_all, d), BF16),
        compiler_params=_params("arbitrary", "arbitrary"),
        name="merge",
    )(oa_ctx, oa_lat, ob_ctx, ob_lat, proj, proj, wa_bf, wb_bf)


def _pack_bf16_pairs(x):
    half = x.shape[1] // 2
    lo = lax.bitcast_convert_type(x[:, :half].astype(BF16).astype(F32), jnp.uint32)
    hi = lax.bitcast_convert_type(x[:, half:].astype(BF16).astype(F32), jnp.uint32)
    return (lo >> 16) | (hi & jnp.uint32(0xFFFF0000))


def _unpack_bf16_pairs(w):
    lo = lax.bitcast_convert_type(w << 16, F32).astype(BF16)
    hi = lax.bitcast_convert_type(w & jnp.uint32(0xFFFF0000), F32).astype(BF16)
    return lo, hi


def _router_kernel(m_ref, xc_ref, xl_ref, g1_ref, sh_ref, sc_ref, ng_ref, wo_ref, rw2_ref, rwh_ref, rb_ref,
                   x1_ref, xp_ref, idx_ref, wgt_ref, *, n_ctx_tiles):
    out = jnp.dot(m_ref[...], wo_ref[...], preferred_element_type=F32)
    x = jnp.where(pl.program_id(0) < n_ctx_tiles, xc_ref[...], xl_ref[...])
    x1 = x + g1_ref[0] * out
    x1_ref[...] = x1
    ms = jnp.mean(x1 * x1, axis=-1, keepdims=True)
    xm = x1 * lax.rsqrt(ms + EPS) * ng_ref[...]
    xm = xm * (1.0 + sc_ref[0]) + sh_ref[0]
    xp_ref[...] = _pack_bf16_pairs(xm)

    x_hi = xm.astype(BF16)
    x_lo = (xm - x_hi.astype(F32)).astype(BF16)
    hh_hl = jnp.dot(x_hi, rw2_ref[...], preferred_element_type=F32)
    lh = jnp.dot(x_lo, rwh_ref[...], preferred_element_type=F32)
    logits = hh_hl[:, :LANES] + (hh_hl[:, LANES:] + lh) + rb_ref[...]
    lane = lax.broadcasted_iota(jnp.int32, logits.shape, 1)
    lane_f = lane.astype(F32)
    idx_out = jnp.zeros(logits.shape, F32)
    val_out = jnp.zeros(logits.shape, F32)
    top = None
    for k in range(TOP_K):
        m = jnp.max(logits, axis=-1, keepdims=True)
        idx = jnp.min(jnp.where(logits == m, lane_f, float(LANES)), axis=-1, keepdims=True)
        if top is None:
            top = m
        idx_out = jnp.where(lane == k, idx, idx_out)
        val_out = jnp.where(lane == k, jnp.exp(m - top), val_out)
        logits = jnp.where(lane_f == idx, -jnp.inf, logits)
    idx_ref[...] = idx_out.astype(jnp.int32)
    wgt_ref[...] = val_out / jnp.sum(val_out, axis=-1, keepdims=True)


def _router(merged, x_ctx, x_lat, mod3, norm_g, wo_bf, rw_pad, rb_pad, t_lat):
    n_ctx_rows, d = x_ctx.shape
    t_all = n_ctx_rows + x_lat.shape[0]
    tm = _tile(min(n_ctx_rows, t_lat), 256)
    row = _mod_row_fn(tm, n_ctx_rows, t_lat)
    mod_spec = lambda k: pl.BlockSpec((1, 1, d), lambda i: (row(i) * N_MOD + k, 0, 0))
    rw_hi = rw_pad.astype(BF16)
    rw_lo = (rw_pad - rw_hi.astype(F32)).astype(BF16)
    rw2 = jnp.concatenate([rw_hi, rw_lo], axis=1)
    return pl.pallas_call(
        functools.partial(_router_kernel, n_ctx_tiles=n_ctx_rows // tm),
        grid=(t_all // tm,),
        in_specs=[pl.BlockSpec((tm, d), lambda i: (i, 0))] + _group_specs(tm, d, n_ctx_rows // tm) + [
                  mod_spec(2), mod_spec(3), mod_spec(4),
                  pl.BlockSpec((1, d), lambda i: (0, 0)),
                  pl.BlockSpec((d, d), lambda i: (0, 0)),
                  pl.BlockSpec((d, 2 * LANES), lambda i: (0, 0)),
                  pl.BlockSpec((d, LANES), lambda i: (0, 0)),
                  pl.BlockSpec((1, LANES), lambda i: (0, 0))],
        out_specs=[pl.BlockSpec((tm, d), lambda i: (i, 0)),
                   pl.BlockSpec((tm, d // 2), lambda i: (i, 0)),
                   pl.BlockSpec((tm, LANES), lambda i: (i, 0)),
                   pl.BlockSpec((tm, LANES), lambda i: (i, 0))],
        out_shape=[jax.ShapeDtypeStruct((t_all, d), F32),
                   jax.ShapeDtypeStruct((t_all, d // 2), jnp.uint32),
                   jax.ShapeDtypeStruct((t_all, LANES), jnp.int32),
                   jax.ShapeDtypeStruct((t_all, LANES), F32)],
        compiler_params=_params("arbitrary"),
        name="router",
    )(merged, x_ctx, x_lat, mod3, mod3, mod3, norm_g.reshape(1, d), wo_bf, rw2, rw_hi, rb_pad)


def _moe_kernel(e_ref, start_ref, nsub_ref, src_ref, tail_ref,
                xp_hbm, order_hbm, wgu_ref, wdn_ref, bgu_ref, bdn_ref, perm_ref,
                ys_hbm,
                tok_smem, x_buf, y_acc, wgu_bf, wdn_bf, tok_sem, row_sem, out_sem,
                *, n_ff_tiles):
    i = pl.program_id(0)
    f = pl.program_id(1)
    nsub = nsub_ref[i]
    start = start_ref[i]
    src = src_ref[i]
    src_row = lax.shift_right_logical(src, 7)
    src_off = src & (LANES - 1)
    g = MOE_GROUP
    d = y_acc.shape[1]
    half = d // 2
    ff2 = wgu_bf.shape[1]

    def row_copy(base, u):
        idx = src_off + base + u
        tok = lax.shift_right_logical(tok_smem[lax.shift_right_logical(idx, 7), idx & (LANES - 1)], 2)
        return pltpu.make_async_copy(xp_hbm.at[pl.ds(tok, 1), :], x_buf.at[pl.ds(base + u, 1), :], row_sem)

    @pl.when((f == 0) & (nsub > 0))
    def _():
        tok_copy = pltpu.make_async_copy(order_hbm.at[pl.ds(src_row, tok_smem.shape[0]), :], tok_smem, tok_sem)
        tok_copy.start()
        tok_copy.wait()

        def issue(r, c):
            for u in range(ROW_UNROLL):
                row_copy(r * ROW_UNROLL, u).start()
            return c

        lax.fori_loop(0, nsub * (g // ROW_UNROLL), issue, 0)

        def init(sb, c):
            rows = pl.ds(pl.multiple_of(sb * g, g), g)
            y_acc[rows, :] = jnp.broadcast_to(bdn_ref[...], (g, d))
            return c

        lax.fori_loop(0, nsub, init, 0)

        def drain(r, c):
            for u in range(ROW_UNROLL):
                row_copy(r * ROW_UNROLL, u).wait()
            return c

        lax.fori_loop(0, nsub * (g // ROW_UNROLL), drain, 0)

    @pl.when(nsub > 0)
    def _():
        for j in range(ff2 // MXU_DIM):
            cols = slice(j * MXU_DIM, (j + 1) * MXU_DIM)
            wgu_bf[:, cols] = jnp.dot(wgu_ref[:, cols].astype(BF16), perm_ref[...],
                                      preferred_element_type=F32).astype(BF16)
        wdn_bf[...] = wdn_ref[...].astype(BF16)
        bgu = bgu_ref[...]

        def expert_rows(first_row, n_rows):
            rows = pl.ds(pl.multiple_of(first_row, g), n_rows)
            x_lo, x_hi = _unpack_bf16_pairs(x_buf[rows, :])
            h = (jnp.dot(x_lo, wgu_bf[:half, :], preferred_element_type=F32)
                 + jnp.dot(x_hi, wgu_bf[half:, :], preferred_element_type=F32) + bgu)
            acts = []
            for j in range(ff2 // MXU_DIM):
                gate = jnp.minimum(h[:, j * MXU_DIM: j * MXU_DIM + LANES], SWIGLU_LIMIT)
                up = jnp.clip(h[:, j * MXU_DIM + LANES: (j + 1) * MXU_DIM], -SWIGLU_LIMIT, SWIGLU_LIMIT)
                acts.append(((up + 1.0) * gate * _sigmoid(SWIGLU_ALPHA * gate)).astype(BF16))
            act = jnp.concatenate(acts, axis=1)
            y_acc[rows, :] += jnp.dot(act, wdn_bf[...], preferred_element_type=F32)

        def pair(p, c):
            expert_rows(p * (2 * g), 2 * g)
            return c

        lax.fori_loop(0, lax.shift_right_logical(nsub, 1), pair, 0)

        @pl.when((nsub & 1) == 1)
        def _():
            expert_rows((nsub - 1) * g, g)

    @pl.when((f == n_ff_tiles - 1) & (nsub > 0))
    def _():
        def out_copy(sb):
            rows = pl.ds(pl.multiple_of(sb * g, g), g)
            dst = pl.ds(pl.multiple_of(start * LANES + sb * g, g), g)
            return pltpu.make_async_copy(y_acc.at[rows, :], ys_hbm.at[dst, :], out_sem)

        def issue(sb, c):
            out_copy(sb).start()
            return c

        def drain(sb, c):
            out_copy(sb).wait()
            return c

        lax.fori_loop(0, nsub, issue, 0)
        lax.fori_loop(0, nsub, drain, 0)

    @pl.when((i == pl.num_programs(0) - 1) & (f == n_ff_tiles - 1))
    def _():
        y_acc[0:g, :] = jnp.zeros((g, d), F32)
        first = tail_ref[0]
        n_tail = ys_hbm.shape[0] // g - first

        def tail_copy(k):
            dst = pl.ds(pl.multiple_of((first + k) * g, g), g)
            return pltpu.make_async_copy(y_acc.at[0:g, :], ys_hbm.at[dst, :], out_sem)

        def issue(k, c):
            tail_copy(k).start()
            return c

        def drain(k, c):
            tail_copy(k).wait()
            return c

        lax.fori_loop(0, n_tail, issue, 0)
        lax.fori_loop(0, n_tail, drain, 0)


def _moe(xp, order2d, plan, w_gu, w_dn, b_gu_perm, b_dn, perm, n_rows):
    n_exp, d, ff2_all = w_gu.shape
    ff = ff2_all // 2
    tf = _tile(ff, MOE_FF_TILE)
    n_f = ff // tf
    n_items = plan[0].shape[0]

    def f_eff(i, f, nsub):
        return jnp.where(nsub[i] > 0, f, n_f - 1)

    grid_spec = pltpu.PrefetchScalarGridSpec(
        num_scalar_prefetch=len(plan),
        grid=(n_items, n_f),
        in_specs=[pl.BlockSpec(memory_space=pl.ANY),
                  pl.BlockSpec(memory_space=pl.ANY),
                  pl.BlockSpec((None, d, 2 * tf), lambda i, f, e, s, n, *_: (e[i], 0, f_eff(i, f, n))),
                  pl.BlockSpec((None, tf, d), lambda i, f, e, s, n, *_: (e[i], f_eff(i, f, n), 0)),
                  pl.BlockSpec((None, 1, 2 * tf), lambda i, f, e, s, n, *_: (e[i], 0, f_eff(i, f, n))),
                  pl.BlockSpec((None, 1, d), lambda i, f, e, s, n, *_: (e[i], 0, 0)),
                  pl.BlockSpec((MXU_DIM, MXU_DIM), lambda i, f, e, s, n, *_: (0, 0))],
        out_specs=pl.BlockSpec(memory_space=pl.ANY),
        scratch_shapes=[pltpu.SMEM((MOE_ITEM_ROWS // LANES + 1, LANES), jnp.int32),
                        pltpu.VMEM((MOE_ITEM_ROWS, d // 2), jnp.uint32),
                        pltpu.VMEM((MOE_ITEM_ROWS, d), F32),
                        pltpu.VMEM((d, 2 * tf), BF16),
                        pltpu.VMEM((tf, d), BF16),
                        pltpu.SemaphoreType.DMA(()),
                        pltpu.SemaphoreType.DMA(()),
                        pltpu.SemaphoreType.DMA(())],
    )
    return pl.pallas_call(
        functools.partial(_moe_kernel, n_ff_tiles=n_f),
        grid_spec=grid_spec,
        out_shape=jax.ShapeDtypeStruct((n_rows, d), F32),
        compiler_params=_params("arbitrary", "arbitrary"),
        name="moe",
    )(*plan, xp, order2d, w_gu, w_dn,
      b_gu_perm.reshape(n_exp, 1, ff2_all), b_dn.reshape(n_exp, 1, d), perm)


def _moe_plan(top_idx, n_exp):
    t_all = top_idx.shape[0]
    n_assign = t_all * TOP_K
    g = MOE_GROUP
    rows_max = n_assign + n_exp * g
    n_items = n_exp + rows_max // MOE_ITEM_ROWS
    n_rows = rows_max + MOE_ITEM_ROWS

    flat_e = top_idx.reshape(-1).astype(jnp.int32)
    assign = jnp.arange(n_assign, dtype=jnp.int32)
    e_sorted, order = lax.sort_key_val(flat_e, assign)
    experts = jnp.arange(n_exp, dtype=jnp.int32)
    first = jnp.sum(e_sorted[:, None] < experts[None, :], axis=0, dtype=jnp.int32)
    counts = jnp.concatenate([first[1:], jnp.full((1,), n_assign, jnp.int32)]) - first
    padded = (counts + g - 1) // g * g
    pad_end = jnp.cumsum(padded)
    pad_start = pad_end - padded
    dest_sorted = pad_start[e_sorted] + assign - first[e_sorted]
    _, dest = lax.sort_key_val(order, dest_sorted)
    order2d = jnp.concatenate([order, jnp.zeros((MOE_ITEM_ROWS + 2 * LANES,), jnp.int32)]).reshape(-1, LANES)

    chunks = (padded + MOE_ITEM_ROWS - 1) // MOE_ITEM_ROWS
    chunk_end = jnp.cumsum(chunks)
    items = jnp.arange(n_items, dtype=jnp.int32)
    n_used = chunk_end[-1]
    last_e = jnp.max(jnp.where(chunks > 0, jnp.arange(n_exp, dtype=jnp.int32), 0))
    e_of = jnp.minimum(jnp.searchsorted(chunk_end, items, side="right"), n_exp - 1).astype(jnp.int32)
    local = items - (chunk_end[e_of] - chunks[e_of])
    used = items < n_used
    item_e = jnp.where(used, e_of, last_e).astype(jnp.int32)
    left = padded[e_of] - local * MOE_ITEM_ROWS
    item_nsub = jnp.where(used, jnp.clip(left, 0, MOE_ITEM_ROWS) // g, 0).astype(jnp.int32)
    item_start = jnp.where(used, (pad_start[e_of] + local * MOE_ITEM_ROWS) // LANES, 0).astype(jnp.int32)
    item_src = jnp.where(used, first[e_of] + local * MOE_ITEM_ROWS, 0).astype(jnp.int32)
    tail = (jnp.sum(padded) // g).astype(jnp.int32).reshape(1)
    return dest, order2d, (item_e, item_start, item_nsub, item_src, tail), n_rows


def _combine_kernel(dest_ref, ys_hbm, wgt_ref, x1_ref, g2_ref, fg_ref, o_ref, gbuf, sem, *, tm):
    tok_per_trip = ROW_UNROLL // TOP_K

    def row_copy(trip, u):
        base = trip * ROW_UNROLL
        d = dest_ref[lax.shift_right_logical(base, 7), (base & (LANES - 1)) + u]
        slot = (u % TOP_K) * tm + trip * tok_per_trip + u // TOP_K
        return pltpu.make_async_copy(ys_hbm.at[pl.ds(d, 1), :], gbuf.at[pl.ds(slot, 1), :], sem)

    def issue(trip, c):
        for u in range(ROW_UNROLL):
            row_copy(trip, u).start()
        return c

    def drain(trip, c):
        for u in range(ROW_UNROLL):
            row_copy(trip, u).wait()
        return c

    lax.fori_loop(0, tm * TOP_K // ROW_UNROLL, issue, 0)
    lax.fori_loop(0, tm * TOP_K // ROW_UNROLL, drain, 0)

    w = wgt_ref[...]
    y = w[:, 0:1] * gbuf[0:tm, :]
    for k in range(1, TOP_K):
        y = y + w[:, k:k + 1] * gbuf[k * tm:(k + 1) * tm, :]
    x = x1_ref[...] + g2_ref[0] * y
    ms = jnp.mean(x * x, axis=-1, keepdims=True)
    o_ref[...] = x * lax.rsqrt(ms + EPS) * fg_ref[...]


def _combine(dest2d, ys, wgt, x1, mod3, final_g, n_ctx_rows, t_lat, *, first_row, n_out_rows, name):
    d = x1.shape[1]
    tm = _tile(min(n_ctx_rows, t_lat), 256)
    row = _mod_row_fn(tm, n_ctx_rows, t_lat)
    idx_rows = tm * TOP_K // LANES
    t0 = first_row // tm
    return pl.pallas_call(
        functools.partial(_combine_kernel, tm=tm),
        grid=(n_out_rows // tm,),
        in_specs=[pl.BlockSpec((idx_rows, LANES), lambda i: (t0 + i, 0), memory_space=pltpu.SMEM),
                  pl.BlockSpec(memory_space=pl.ANY),
                  pl.BlockSpec((tm, LANES), lambda i: (t0 + i, 0)),
                  pl.BlockSpec((tm, d), lambda i: (t0 + i, 0)),
                  pl.BlockSpec((1, 1, d), lambda i: (row(t0 + i) * N_MOD + 5, 0, 0)),
                  pl.BlockSpec((1, d), lambda i: (0, 0))],
        out_specs=pl.BlockSpec((tm, d), lambda i: (i, 0)),
        out_shape=jax.ShapeDtypeStruct((n_out_rows, d), F32),
        scratch_shapes=[pltpu.VMEM((tm * TOP_K, d), F32), pltpu.SemaphoreType.DMA(())],
        compiler_params=_params("arbitrary"),
        name=name,
    )(dest2d, ys, wgt, x1, mod3, final_g.reshape(1, d))


def kernel(x_prompt, x_sample, state_hgrn_fwd, state_hgrn_bwd, state_ret_fwd, state_ret_bwd, c, c_ctx, ada_w, ada_b, norm1_g, norm2_g, final_norm_g, w_in, hg_lb_fwd, hg_lb_bwd, hg_norm_g, ret_log2_fwd, ret_log2_bwd, ret_norm_g, w_proj_hgrn, w_proj_ret, w_out, router_w, router_b, moe_w_gu, moe_b_gu, moe_w_dn, moe_b_dn):
    assert ada_w.shape[0] == 1, "one trunk layer"
    b_ctx, t_ctx, d = x_prompt.shape
    b_lat, t_lat, _ = x_sample.shape
    n_ctx_rows = b_ctx * t_ctx
    assert n_ctx_rows % t_lat == 0 and b_lat + 1 <= COND_ROWS
    n_exp = router_w.shape[2]

    x_ctx = x_prompt.reshape(n_ctx_rows, d)
    x_lat = x_sample.reshape(b_lat * t_lat, d)
    cond = jnp.zeros((COND_ROWS, d), F32).at[0].set(c_ctx).at[1:1 + b_lat].set(c)
    mod3 = _adaln(cond, ada_w[0], ada_b[0]).reshape(COND_ROWS * N_MOD, 1, d)

    proj = _inproj(x_ctx, x_lat, mod3, norm1_g[0], w_in[0], t_lat)

    hg_args = (proj, hg_lb_fwd, hg_lb_bwd, hg_norm_g[0])
    oa_ctx, new_hf, new_hb = _hgrn(*hg_args, None, n_seq=b_ctx, t=t_ctx, row_off=0, emit_state=True)
    (oa_lat,) = _hgrn(*hg_args, (state_hgrn_fwd, state_hgrn_bwd), n_seq=b_lat, t=t_lat,
                      row_off=n_ctx_rows // t_lat, emit_state=False)

    p_f = jnp.broadcast_to(ret_log2_fwd[0][:, None, None], (RET_HEADS, 1, RET_DV))
    p_b = jnp.broadcast_to(ret_log2_bwd[0][:, None, None], (RET_HEADS, 1, RET_DV))
    ret_args = (proj, p_f, p_b, ret_norm_g[0])
    ob_ctx, new_rf, new_rb = _ret(*ret_args, None, None, n_seq=b_ctx, t=t_ctx, row_off=0, emit_state=True)
    (ob_lat,) = _ret(*ret_args, _rope_tables(t_lat), (state_ret_fwd, state_ret_bwd), n_seq=b_lat, t=t_lat,
                     row_off=n_ctx_rows // t_lat, emit_state=False)

    merged = _merge(oa_ctx, oa_lat, ob_ctx, ob_lat, proj, w_proj_hgrn[0].astype(BF16), w_proj_ret[0].astype(BF16))

    rw_pad = jnp.zeros((d, LANES), F32).at[:, :n_exp].set(router_w[0])
    rb_pad = jnp.full((1, LANES), -1e30, F32).at[0, :n_exp].set(router_b[0])
    x1, xp, top_idx, top_w = _router(merged, x_ctx, x_lat, mod3, norm2_g[0], w_out[0].astype(BF16),
                                     rw_pad, rb_pad, t_lat)

    dest, order2d, plan, n_rows = _moe_plan(top_idx[:, :TOP_K], n_exp)
    ff = moe_w_dn.shape[2]
    b_gu_perm = moe_b_gu[0].reshape(n_exp, ff // LANES, LANES, 2).transpose(0, 1, 3, 2).reshape(n_exp, 2 * ff)
    src = jnp.arange(MXU_DIM)
    perm = jnp.zeros((MXU_DIM, MXU_DIM), BF16).at[src, (src % 2) * LANES + src // 2].set(1)
    ys = _moe(xp, order2d, plan, moe_w_gu[0], moe_w_dn[0], b_gu_perm, moe_b_dn[0], perm, n_rows)

    comb_args = (dest.reshape(-1, LANES), ys, top_w, x1, mod3, final_norm_g, n_ctx_rows, t_lat)
    y_prompt = _combine(*comb_args, first_row=0, n_out_rows=n_ctx_rows, name="combine_ctx")
    y_sample = _combine(*comb_args, first_row=n_ctx_rows, n_out_rows=b_lat * t_lat, name="combine_lat")
    return (y_prompt.reshape(b_ctx, t_ctx, d), y_sample.reshape(b_lat, t_lat, d), new_hf, new_hb, new_rf, new_rb)
```

```python
import functools

import jax
import jax.numpy as jnp
from jax import lax
from jax.experimental import pallas as pl
from jax.experimental.pallas import tpu as pltpu

F32 = jnp.float32
BF16 = jnp.bfloat16

HG_HEADS = 8
HG_DK = 128
HG_DV = 128
RET_HEADS = 8
RET_DK = 128
RET_DV = 256
CHUNK = 32
GRID_W = 64
ROPE_BASE = 10000.0
TOP_K = 4
SWIGLU_LIMIT = 7.0
SWIGLU_ALPHA = 1.702
N_MOD = 6
EPS = 1e-6

LANES = 128
MXU_DIM = 256
VMEM_LIMIT = 56 * 1024 * 1024
COND_ROWS = 16
SCAN_BLOCK = 256
SCAN_UNROLL = 8
HGRN_HEADS_PER_STEP = 4
RET_HEADS_PER_STEP = 4
RET_ROWS_PER_STEP = 2048
MOE_GROUP = 256
MOE_ITEM_ROWS = 2048
MOE_FF_TILE = 256
ROW_UNROLL = 8


def _sigmoid(x):
    return 0.5 * jnp.tanh(0.5 * x) + 0.5


def _params(*sem):
    return pltpu.CompilerParams(dimension_semantics=sem, vmem_limit_bytes=VMEM_LIMIT)


def _tile(n, pref):
    if n <= pref:
        return n
    for t in range(pref - pref % LANES, 0, -LANES):
        if n % t == 0:
            return t
    raise ValueError((n, pref))


def _adaln_kernel(c_ref, w_ref, b_ref, o_ref):
    c = c_ref[...]
    s = (c * _sigmoid(c)).astype(BF16)
    o_ref[...] = jnp.dot(s, w_ref[...].astype(BF16), preferred_element_type=F32) + b_ref[...]


def _adaln(cond, ada_w, ada_b):
    d, n = ada_w.shape
    tn = _tile(n, 1024)
    return pl.pallas_call(
        _adaln_kernel,
        grid=(n // tn,),
        in_specs=[pl.BlockSpec((COND_ROWS, d), lambda j: (0, 0)),
                  pl.BlockSpec((d, tn), lambda j: (0, j)),
                  pl.BlockSpec((1, tn), lambda j: (0, j))],
        out_specs=pl.BlockSpec((COND_ROWS, tn), lambda j: (0, j)),
        out_shape=jax.ShapeDtypeStruct((COND_ROWS, n), F32),
        compiler_params=_params("arbitrary"),
        name="adaln",
    )(cond, ada_w, ada_b.reshape(1, n))


def _prenorm_kernel(xc_ref, xl_ref, sh_ref, sc_ref, g_ref, o_ref, *, n_ctx_tiles):
    x = jnp.where(pl.program_id(0) < n_ctx_tiles, xc_ref[...], xl_ref[...])
    ms = jnp.mean(x * x, axis=-1, keepdims=True)
    xn = x * lax.rsqrt(ms + EPS) * g_ref[...]
    o_ref[...] = (xn * (1.0 + sc_ref[0]) + sh_ref[0]).astype(BF16)


def _inproj_kernel(xm_ref, w_ref, o_ref):
    o_ref[...] = jnp.dot(xm_ref[...], w_ref[...].astype(BF16), preferred_element_type=F32)


def _mod_row_fn(tm, n_ctx_rows, t_lat):
    n_ctx_tiles = n_ctx_rows // tm

    def row(i):
        return jnp.where(i < n_ctx_tiles, 0, 1 + ((i - n_ctx_tiles) * tm) // t_lat)

    return row


def _group_specs(tm, width, n_ctx_tiles):
    ctx = lambda i: (jnp.minimum(i, n_ctx_tiles - 1), 0)
    lat = lambda i: (jnp.maximum(i - n_ctx_tiles, 0), 0)
    return [pl.BlockSpec((tm, width), ctx), pl.BlockSpec((tm, width), lat)]


def _inproj(x_ctx, x_lat, mod3, norm_g, w_in, t_lat):
    n_ctx_rows, d = x_ctx.shape
    t_all = n_ctx_rows + x_lat.shape[0]
    n = w_in.shape[1]
    tp = _tile(min(n_ctx_rows, t_lat), 512)
    row = _mod_row_fn(tp, n_ctx_rows, t_lat)
    xm = pl.pallas_call(
        functools.partial(_prenorm_kernel, n_ctx_tiles=n_ctx_rows // tp),
        grid=(t_all // tp,),
        in_specs=_group_specs(tp, d, n_ctx_rows // tp) + [
            pl.BlockSpec((1, 1, d), lambda i: (row(i) * N_MOD + 0, 0, 0)),
            pl.BlockSpec((1, 1, d), lambda i: (row(i) * N_MOD + 1, 0, 0)),
            pl.BlockSpec((1, d), lambda i: (0, 0))],
        out_specs=pl.BlockSpec((tp, d), lambda i: (i, 0)),
        out_shape=jax.ShapeDtypeStruct((t_all, d), BF16),
        compiler_params=_params("arbitrary"),
        name="prenorm",
    )(x_ctx, x_lat, mod3, mod3, norm_g.reshape(1, d))
    tm = _tile(t_all, 1024)
    tn = _tile(n, 1024)
    return pl.pallas_call(
        _inproj_kernel,
        grid=(t_all // tm, n // tn),
        in_specs=[pl.BlockSpec((tm, d), lambda i, j: (i, 0)),
                  pl.BlockSpec((d, tn), lambda i, j: (0, j))],
        out_specs=pl.BlockSpec((tm, tn), lambda i, j: (i, j)),
        out_shape=jax.ShapeDtypeStruct((t_all, n), F32),
        compiler_params=_params("arbitrary", "arbitrary"),
        name="inproj",
    )(xm, w_in)


def _dot_split(m_bf, x):
    hi = x.astype(BF16)
    lo = (x - hi.astype(F32)).astype(BF16)
    return jnp.dot(m_bf, hi, preferred_element_type=F32) + jnp.dot(m_bf, lo, preferred_element_type=F32)


_NT = (((1,), (1,)), ((), ()))
_TN = (((0,), (0,)), ((), ()))


def _hgrn_kernel(*refs, t, hpg, has_state, emit_state):
    hq_ref, hzf_ref, hzb_ref, hi_ref, hgate_ref, lbf_ref, lbb_ref, gn_ref = refs[:8]
    pos = 8
    if has_state:
        s0f_ref, s0b_ref = refs[pos:pos + 2]
        pos += 2
    out_ref = refs[pos]
    pos += 1
    if emit_state:
        sf_ref, sb_ref = refs[pos:pos + 2]
        pos += 2
    (q_in_f, k_end_f, dec_f, q_in_b, k_end_b, dec_b, v_bf, o_f, o_b, st_ref) = refs[pos:]
    heads = [slice(j * LANES, (j + 1) * LANES) for j in range(hpg)]

    blk = min(SCAN_BLOCK, t)
    n_blk = t // blk
    n_chunk = t // CHUNK

    r = lax.broadcasted_iota(jnp.int32, (blk, blk), 0)
    c = lax.broadcasted_iota(jnp.int32, (blk, blk), 1)
    same = (r // CHUNK) == (c // CHUNK)
    low = same & (c <= r)
    upp = same & (c >= r)
    m_pre = jnp.where(low, 1.0, 0.0).astype(BF16)
    m_suf = jnp.where(upp, 1.0, 0.0).astype(BF16)

    def lower_bound(lb_ref):
        a = lb_ref[...]
        e = jnp.exp(a - jnp.max(a, axis=0, keepdims=True))
        return e[0:1] / jnp.sum(e, axis=0, keepdims=True)

    lb_f = lower_bound(lbf_ref)
    lb_b = lower_bound(lbb_ref)

    width = hpg * LANES

    def chunk_total(g, reverse):
        g3 = g.reshape(blk // CHUNK, CHUNK, width)
        edge = g3[:, 0:1, :] if reverse else g3[:, CHUNK - 1:CHUNK, :]
        return jnp.broadcast_to(edge, g3.shape).reshape(blk, width)

    directions = ((hzf_ref, lb_f, m_pre, low, False, q_in_f, k_end_f, dec_f, o_f),
                  (hzb_ref, lb_b, m_suf, upp, True, q_in_b, k_end_b, dec_b, o_b))

    def prep(i, carry):
        rows = pl.ds(pl.multiple_of(i * blk, blk), blk)
        hq = hq_ref[rows, :]
        q = hq * _sigmoid(hq)
        v = hi_ref[rows, :].astype(BF16)
        v_bf[rows, :] = v
        for z_ref, lb, m_cum, mask, reverse, qi, ke, dc, o_ref in directions:
            f = lb + (1.0 - lb) * _sigmoid(z_ref[rows, :])
            k = 1.0 - f
            g = _dot_split(m_cum, jnp.log(f))
            tot = chunk_total(g, reverse)
            q_in = (q * jnp.exp(g)).astype(BF16)
            k_in = (k * jnp.exp(-g)).astype(BF16)
            qi[rows, :] = q_in
            ke[rows, :] = (k * jnp.exp(tot - g)).astype(BF16)
            dc[rows, :] = jnp.exp(tot)
            for hd in heads:
                s = lax.dot_general(q_in[:, hd], k_in[:, hd], _NT, preferred_element_type=F32)
                o_ref[rows, hd] = jnp.dot(jnp.where(mask, s, 0.0).astype(BF16), v[:, hd],
                                          preferred_element_type=F32)
        return carry

    lax.fori_loop(0, n_blk, prep, 0)

    for j in range(hpg):
        for slot, s0_ref in ((2 * j, s0f_ref if has_state else None), (2 * j + 1, s0b_ref if has_state else None)):
            st_ref[slot] = s0_ref[j].T if has_state else jnp.zeros((HG_DV, HG_DK), F32)

    def chunk_step(idx, slot, hd, qi, ke, dc, o_ref):
        start = pl.multiple_of(idx * CHUNK, CHUNK)
        rows = pl.ds(start, CHUNK)
        st = st_ref[slot]
        o_ref[rows, hd] += lax.dot_general(qi[rows, hd], st.astype(BF16), _NT, preferred_element_type=F32)
        upd = lax.dot_general(v_bf[rows, hd], ke[rows, hd], _TN, preferred_element_type=F32)
        st_ref[slot] = dc[pl.ds(start, 1), hd] * st + upd

    def body(i, carry):
        for j, hd in enumerate(heads):
            chunk_step(i, 2 * j, hd, q_in_f, k_end_f, dec_f, o_f)
            chunk_step(n_chunk - 1 - i, 2 * j + 1, hd, q_in_b, k_end_b, dec_b, o_b)
        return carry

    lax.fori_loop(0, n_chunk, body, 0, unroll=min(n_chunk, SCAN_UNROLL))
    if emit_state:
        for j in range(hpg):
            sf_ref[j] = st_ref[2 * j].T
            sb_ref[j] = st_ref[2 * j + 1].T

    for hd in heads:
        o = o_f[:, hd] + o_b[:, hd]
        o = o * lax.rsqrt(jnp.mean(o * o, axis=-1, keepdims=True) + EPS) * gn_ref[...]
        out_ref[:, hd] = (o * _sigmoid(hgate_ref[:, hd])).astype(BF16)


def _hgrn(proj, lb_fwd, lb_bwd, norm_g, states, *, n_seq, t, row_off, emit_state):
    nh = HG_HEADS
    hpg = min(nh, HGRN_HEADS_PER_STEP)
    assert nh % hpg == 0
    ng = nh // hpg
    w = hpg * LANES
    col = lambda k: (lambda b, h: (row_off + b, k * ng + h))
    blk = lambda k: pl.BlockSpec((t, w), col(k))
    in_specs = [blk(0), blk(1), blk(2), blk(3), blk(4),
                pl.BlockSpec((2, w), lambda b, h: (0, h)),
                pl.BlockSpec((2, w), lambda b, h: (0, h)),
                pl.BlockSpec((1, LANES), lambda b, h: (0, 0))]
    args = [proj] * 5 + [lb_fwd, lb_bwd, norm_g.reshape(1, HG_DV)]
    st_spec = pl.BlockSpec((None, None, hpg, HG_DK, HG_DV), lambda b, h: (b, 0, h, 0, 0))
    if states is not None:
        in_specs += [st_spec, st_spec]
        args += list(states)
    out_specs = [pl.BlockSpec((t, w), lambda b, h: (b, h))]
    out_shape = [jax.ShapeDtypeStruct((n_seq * t, nh * HG_DV), BF16)]
    if emit_state:
        out_specs += [st_spec, st_spec]
        out_shape += [jax.ShapeDtypeStruct((n_seq, 1, nh, HG_DK, HG_DV), F32)] * 2
    scratch = ([pltpu.VMEM((t, w), BF16)] * 2 + [pltpu.VMEM((t, w), F32)]) * 2
    scratch += [pltpu.VMEM((t, w), BF16), pltpu.VMEM((t, w), F32), pltpu.VMEM((t, w), F32),
                pltpu.VMEM((2 * hpg, HG_DV, HG_DK), F32)]
    return pl.pallas_call(
        functools.partial(_hgrn_kernel, t=t, hpg=hpg, has_state=states is not None, emit_state=emit_state),
        grid=(n_seq, ng),
        in_specs=in_specs,
        out_specs=out_specs,
        out_shape=out_shape,
        scratch_shapes=scratch,
        compiler_params=_params("arbitrary", "arbitrary"),
        name="hgrn_lat" if states is not None else "hgrn_ctx",
    )(*args)


def _ret_kernel(*refs, t, hpg, has_state, emit_state, use_rope):
    rq_ref, rk_ref, rv_ref, rgate_ref, pf_ref, pb_ref, gn_ref = refs[:7]
    pos = 7
    if use_rope:
        cos_ref, sin_ref = refs[pos:pos + 2]
        pos += 2
    if has_state:
        s0f_ref, s0b_ref = refs[pos:pos + 2]
        pos += 2
    out_ref = refs[pos]
    pos += 1
    if emit_state:
        sf_ref, sb_ref = refs[pos:pos + 2]
        pos += 2
    q_s, k_s, v_s, o_f, o_b, st_ref = refs[pos:]

    ch = min(MXU_DIM, t)
    n_chunk = t // ch

    lane = lax.broadcasted_iota(jnp.int32, (t, RET_DK), 1)
    first_half = (lane % 64) < 32

    def rope(x):
        if not use_rope:
            return x
        swapped = jnp.where(first_half, pltpu.roll(x, RET_DK - 32, 1), pltpu.roll(x, 32, 1))
        return x * cos_ref[...] + swapped * sin_ref[...]

    ri = lax.broadcasted_iota(jnp.int32, (ch, ch), 0)
    ci = lax.broadcasted_iota(jnp.int32, (ch, ch), 1)
    row128 = lax.broadcasted_iota(jnp.int32, (ch, RET_DK), 0).astype(F32)

    def direction(j, p_ref, slot, o_scr, s0_ref, reverse):
        qk = slice(j * RET_DK, (j + 1) * RET_DK)
        vv = slice(j * RET_DV, (j + 1) * RET_DV)
        lg = jnp.log1p(-jnp.exp2(p_ref[j]))
        lg_k = lg[:, :RET_DK]
        dist = (ci - ri) if reverse else (ri - ci)
        keep = dist >= 0
        dmask = jnp.where(keep, jnp.exp(lg[:, :ch] * jnp.where(keep, dist, 0).astype(F32)), 0.0)
        if reverse:
            q_scale = jnp.exp(lg_k * (ch - row128))
            k_scale = jnp.exp(lg_k * row128)
        else:
            q_scale = jnp.exp(lg_k * (row128 + 1.0))
            k_scale = jnp.exp(lg_k * (ch - 1.0 - row128))
        dec = jnp.exp(lg * float(ch))
        st_ref[slot] = s0_ref[j] if has_state else jnp.zeros((RET_DK, RET_DV), F32)
        for step in range(n_chunk):
            cidx = n_chunk - 1 - step if reverse else step
            rows = pl.ds(cidx * ch, ch)
            q = q_s[rows, qk]
            k = k_s[rows, qk]
            v = v_s[rows, vv]
            s = lax.dot_general(q.astype(BF16), k.astype(BF16), _NT, preferred_element_type=F32)
            o = jnp.dot((s * dmask).astype(BF16), v, preferred_element_type=F32)
            st = st_ref[slot]
            o = o + jnp.dot((q * q_scale).astype(BF16), st.astype(BF16), preferred_element_type=F32)
            o_scr[rows, vv] = o
            upd = lax.dot_general((k * k_scale).astype(BF16), v, _TN, preferred_element_type=F32)
            st_ref[slot] = dec * st + upd

    for j in range(hpg):
        qk = slice(j * RET_DK, (j + 1) * RET_DK)
        q_s[:, qk] = rope(rq_ref[:, qk])
        k_s[:, qk] = rope(rk_ref[:, qk] * (RET_DK ** -0.5))
    v_s[...] = rv_ref[...].astype(BF16)
    for j in range(hpg):
        direction(j, pf_ref, 2 * j, o_f, s0f_ref if has_state else None, False)
        direction(j, pb_ref, 2 * j + 1, o_b, s0b_ref if has_state else None, True)
    if emit_state:
        for j in range(hpg):
            sf_ref[j] = st_ref[2 * j]
            sb_ref[j] = st_ref[2 * j + 1]

    for j in range(hpg):
        vv = slice(j * RET_DV, (j + 1) * RET_DV)
        o = o_f[:, vv] + o_b[:, vv]
        oc = o - jnp.mean(o, axis=-1, keepdims=True)
        o = oc * lax.rsqrt(jnp.mean(oc * oc, axis=-1, keepdims=True) + EPS) * gn_ref[...]
        gate = rgate_ref[:, vv]
        out_ref[:, vv] = (o * (gate * _sigmoid(gate))).astype(BF16)


def _ret(proj, p_fwd, p_bwd, norm_g, rope, states, *, n_seq, t, row_off, emit_state):
    nh = RET_HEADS
    base = (3 * HG_HEADS * HG_DK + 2 * HG_HEADS * HG_DV)
    hpg = max(1, min(nh, RET_HEADS_PER_STEP, RET_ROWS_PER_STEP // t))
    while nh % hpg or base % (hpg * RET_DK) or (base + 2 * nh * RET_DK) % (hpg * RET_DV):
        hpg -= 1
    ng = nh // hpg
    wk, wv = hpg * RET_DK, hpg * RET_DV
    qk0 = base // wk
    v0 = (base + 2 * nh * RET_DK) // wv
    in_specs = [pl.BlockSpec((t, wk), lambda b, h: (row_off + b, qk0 + h)),
                pl.BlockSpec((t, wk), lambda b, h: (row_off + b, qk0 + ng + h)),
                pl.BlockSpec((t, wv), lambda b, h: (row_off + b, v0 + h)),
                pl.BlockSpec((t, wv), lambda b, h: (row_off + b, v0 + ng + h)),
                pl.BlockSpec((hpg, 1, RET_DV), lambda b, h: (h, 0, 0)),
                pl.BlockSpec((hpg, 1, RET_DV), lambda b, h: (h, 0, 0)),
                pl.BlockSpec((1, RET_DV), lambda b, h: (0, 0))]
    args = [proj] * 4 + [p_fwd, p_bwd, norm_g.reshape(1, RET_DV)]
    if rope is not None:
        in_specs += [pl.BlockSpec((t, RET_DK), lambda b, h: (0, 0))] * 2
        args += list(rope)
    st_spec = pl.BlockSpec((None, None, hpg, RET_DK, RET_DV), lambda b, h: (b, 0, h, 0, 0))
    if states is not None:
        in_specs += [st_spec, st_spec]
        args += list(states)
    out_specs = [pl.BlockSpec((t, wv), lambda b, h: (b, h))]
    out_shape = [jax.ShapeDtypeStruct((n_seq * t, nh * RET_DV), BF16)]
    if emit_state:
        out_specs += [st_spec, st_spec]
        out_shape += [jax.ShapeDtypeStruct((n_seq, 1, nh, RET_DK, RET_DV), F32)] * 2
    scratch = [pltpu.VMEM((t, wk), F32), pltpu.VMEM((t, wk), F32), pltpu.VMEM((t, wv), BF16),
               pltpu.VMEM((t, wv), F32), pltpu.VMEM((t, wv), F32),
               pltpu.VMEM((2 * hpg, RET_DK, RET_DV), F32)]
    return pl.pallas_call(
        functools.partial(_ret_kernel, t=t, hpg=hpg, has_state=states is not None, emit_state=emit_state,
                          use_rope=rope is not None),
        grid=(n_seq, ng),
        in_specs=in_specs,
        out_specs=out_specs,
        out_shape=out_shape,
        scratch_shapes=scratch,
        compiler_params=_params("arbitrary", "arbitrary"),
        name="ret_lat" if states is not None else "ret_ctx",
    )(*args)


def _rope_tables(t):
    pairs = RET_DK // 4
    pos = jnp.arange(t, dtype=jnp.int32)
    rows = (pos // GRID_W).astype(F32)
    cols = (pos % GRID_W).astype(F32)
    inv = ROPE_BASE ** (-jnp.arange(pairs, dtype=F32) / pairs)
    ar = rows[:, None] * inv
    ac = cols[:, None] * inv
    cos = jnp.concatenate([jnp.cos(ar), jnp.cos(ar), jnp.cos(ac), jnp.cos(ac)], axis=1)
    sin = jnp.concatenate([-jnp.sin(ar), jnp.sin(ar), -jnp.sin(ac), jnp.sin(ac)], axis=1)
    return cos, sin


def _merge_kernel(oac_ref, oal_ref, obc_ref, obl_ref, ga_ref, gb_ref, wa_ref, wb_ref, o_ref, *, n_ctx_tiles):
    is_ctx = pl.program_id(1) < n_ctx_tiles
    oa = jnp.where(is_ctx, oac_ref[...], oal_ref[...])
    ob = jnp.where(is_ctx, obc_ref[...], obl_ref[...])
    a = jnp.dot(oa, wa_ref[...], preferred_element_type=F32)
    b = jnp.dot(ob, wb_ref[...], preferred_element_type=F32)
    o_ref[...] = (_sigmoid(ga_ref[...]) * a + _sigmoid(gb_ref[...]) * b).astype(BF16)


def _merge(oa_ctx, oa_lat, ob_ctx, ob_lat, proj, wa_bf, wb_bf):
    n_ctx_rows, ka = oa_ctx.shape
    t_all = n_ctx_rows + oa_lat.shape[0]
    kb = ob_ctx.shape[1]
    d = wa_bf.shape[1]
    tm = _tile(min(n_ctx_rows, oa_lat.shape[0]), 512)
    tn = _tile(d, 1024)
    nct = n_ctx_rows // tm
    gate0 = (proj.shape[1] - 2 * d) // tn
    ctx = lambda j, i: (jnp.minimum(i, nct - 1), 0)
    lat = lambda j, i: (jnp.maximum(i - nct, 0), 0)
    return pl.pallas_call(
        functools.partial(_merge_kernel, n_ctx_tiles=nct),
        grid=(d // tn, t_all // tm),
        in_specs=[pl.BlockSpec((tm, ka), ctx), pl.BlockSpec((tm, ka), lat),
                  pl.BlockSpec((tm, kb), ctx), pl.BlockSpec((tm, kb), lat),
                  pl.BlockSpec((tm, tn), lambda j, i: (i, gate0 + j)),
                  pl.BlockSpec((tm, tn), lambda j, i: (i, gate0 + d // tn + j)),
                  pl.BlockSpec((ka, tn), lambda j, i: (0, j)),
                  pl.BlockSpec((kb, tn), lambda j, i: (0, j))],
        out_specs=pl.BlockSpec((tm, tn), lambda j, i: (i, j)),
        out_shape=jax.ShapeDtypeStruct((t_all, d), BF16),
        compiler_params=_params("arbitrary", "arbitrary"),
        name="merge",
    )(oa_ctx, oa_lat, ob_ctx, ob_lat, proj, proj, wa_bf, wb_bf)


def _pack_bf16_pairs(x):
    half = x.shape[1] // 2
    lo = lax.bitcast_convert_type(x[:, :half].astype(BF16).astype(F32), jnp.uint32)
    hi = lax.bitcast_convert_type(x[:, half:].astype(BF16).astype(F32), jnp.uint32)
    return (lo >> 16) | (hi & jnp.uint32(0xFFFF0000))


def _unpack_bf16_pairs(w):
    lo = lax.bitcast_convert_type(w << 16, F32).astype(BF16)
    hi = lax.bitcast_convert_type(w & jnp.uint32(0xFFFF0000), F32).astype(BF16)
    return lo, hi


def _router_kernel(m_ref, xc_ref, xl_ref, g1_ref, sh_ref, sc_ref, ng_ref, wo_ref, rw2_ref, rwh_ref, rb_ref,
                   x1_ref, xp_ref, idx_ref, wgt_ref, *, n_ctx_tiles):
    out = jnp.dot(m_ref[...], wo_ref[...], preferred_element_type=F32)
    x = jnp.where(pl.program_id(0) < n_ctx_tiles, xc_ref[...], xl_ref[...])
    x1 = x + g1_ref[0] * out
    x1_ref[...] = x1
    ms = jnp.mean(x1 * x1, axis=-1, keepdims=True)
    xm = x1 * lax.rsqrt(ms + EPS) * ng_ref[...]
    xm = xm * (1.0 + sc_ref[0]) + sh_ref[0]
    packed = _pack_bf16_pairs(xm)
    for s in range(xp_ref.shape[1]):
        xp_ref[:, s, :] = packed[:, s * LANES:(s + 1) * LANES]

    x_hi = xm.astype(BF16)
    x_lo = (xm - x_hi.astype(F32)).astype(BF16)
    hh_hl = jnp.dot(x_hi, rw2_ref[...], preferred_element_type=F32)
    lh = jnp.dot(x_lo, rwh_ref[...], preferred_element_type=F32)
    logits = hh_hl[:, :LANES] + (hh_hl[:, LANES:] + lh) + rb_ref[...]
    lane = lax.broadcasted_iota(jnp.int32, logits.shape, 1)
    lane_f = lane.astype(F32)
    idx_out = jnp.zeros(logits.shape, F32)
    val_out = jnp.zeros(logits.shape, F32)
    top = None
    for k in range(TOP_K):
        m = jnp.max(logits, axis=-1, keepdims=True)
        idx = jnp.min(jnp.where(logits == m, lane_f, float(LANES)), axis=-1, keepdims=True)
        if top is None:
            top = m
        idx_out = jnp.where(lane == k, idx, idx_out)
        val_out = jnp.where(lane == k, jnp.exp(m - top), val_out)
        logits = jnp.where(lane_f == idx, -jnp.inf, logits)
    idx_ref[...] = idx_out.astype(jnp.int32)
    wgt_ref[...] = val_out / jnp.sum(val_out, axis=-1, keepdims=True)


def _router(merged, x_ctx, x_lat, mod3, norm_g, wo_bf, rw_pad, rb_pad, t_lat):
    n_ctx_rows, d = x_ctx.shape
    t_all = n_ctx_rows + x_lat.shape[0]
    tm = _tile(min(n_ctx_rows, t_lat), 256)
    row = _mod_row_fn(tm, n_ctx_rows, t_lat)
    mod_spec = lambda k: pl.BlockSpec((1, 1, d), lambda i: (row(i) * N_MOD + k, 0, 0))
    rw_hi = rw_pad.astype(BF16)
    rw_lo = (rw_pad - rw_hi.astype(F32)).astype(BF16)
    rw2 = jnp.concatenate([rw_hi, rw_lo], axis=1)
    return pl.pallas_call(
        functools.partial(_router_kernel, n_ctx_tiles=n_ctx_rows // tm),
        grid=(t_all // tm,),
        in_specs=[pl.BlockSpec((tm, d), lambda i: (i, 0))] + _group_specs(tm, d, n_ctx_rows // tm) + [
                  mod_spec(2), mod_spec(3), mod_spec(4),
                  pl.BlockSpec((1, d), lambda i: (0, 0)),
                  pl.BlockSpec((d, d), lambda i: (0, 0)),
                  pl.BlockSpec((d, 2 * LANES), lambda i: (0, 0)),
                  pl.BlockSpec((d, LANES), lambda i: (0, 0)),
                  pl.BlockSpec((1, LANES), lambda i: (0, 0))],
        out_specs=[pl.BlockSpec((tm, d), lambda i: (i, 0)),
                   pl.BlockSpec((tm, d // 2 // LANES, LANES), lambda i: (i, 0, 0)),
                   pl.BlockSpec((tm, LANES), lambda i: (i, 0)),
                   pl.BlockSpec((tm, LANES), lambda i: (i, 0))],
        out_shape=[jax.ShapeDtypeStruct((t_all, d), F32),
                   jax.ShapeDtypeStruct((t_all, d // 2 // LANES, LANES), jnp.uint32),
                   jax.ShapeDtypeStruct((t_all, LANES), jnp.int32),
                   jax.ShapeDtypeStruct((t_all, LANES), F32)],
        compiler_params=_params("arbitrary"),
        name="router",
    )(merged, x_ctx, x_lat, mod3, mod3, mod3, norm_g.reshape(1, d), wo_bf, rw2, rw_hi, rb_pad)


def _moe_kernel(e_ref, start_ref, nsub_ref, src_ref, tail_ref,
                xp_hbm, order_hbm, wgu_ref, wdn_ref, bgu_ref, bdn_ref, perm_ref,
                ys_hbm,
                tok_smem, x_raw, x_lo, x_hi, y_acc, wgu_bf, wdn_bf, tok_sem, row_sem, out_sem,
                *, n_ff_tiles):
    i = pl.program_id(0)
    f = pl.program_id(1)
    nsub = nsub_ref[i]
    start = start_ref[i]
    n_slab = xp_hbm.shape[1]
    g = MOE_GROUP
    d = y_acc.shape[1]
    half = d // 2
    ff2 = wgu_bf.shape[1]

    def order_copy(it):
        row = lax.shift_right_logical(src_ref[it], 7)
        return pltpu.make_async_copy(order_hbm.at[pl.ds(row, tok_smem.shape[0]), :], tok_smem, tok_sem)

    def row_copy(off, base, u):
        idx = off + base + u
        tok = lax.shift_right_logical(tok_smem[lax.shift_right_logical(idx, 7), idx & (LANES - 1)], 2)
        return pltpu.make_async_copy(xp_hbm.at[tok], x_raw.at[pl.ds((base + u) * n_slab, n_slab), :], row_sem)

    def gather_start(it):
        off = src_ref[it] & (LANES - 1)

        def issue(r, c):
            for u in range(ROW_UNROLL):
                row_copy(off, r * ROW_UNROLL, u).start()
            return c

        lax.fori_loop(0, nsub_ref[it] * (g // ROW_UNROLL), issue, 0)

    def gather_wait(it):
        off = src_ref[it] & (LANES - 1)

        def drain(r, c):
            for u in range(ROW_UNROLL):
                row_copy(off, r * ROW_UNROLL, u).wait()
            return c

        lax.fori_loop(0, nsub_ref[it] * (g // ROW_UNROLL), drain, 0)

    nxt = jnp.minimum(i + 1, pl.num_programs(0) - 1)
    has_next = (i + 1 < pl.num_programs(0)) & (nsub_ref[nxt] > 0)

    @pl.when((i == 0) & (f == 0) & (nsub > 0))
    def _():
        order_copy(0).start()
        order_copy(0).wait()
        gather_start(0)

    @pl.when((f == 0) & (nsub > 0))
    def _():
        def init(sb, c):
            rows = pl.ds(pl.multiple_of(sb * g, g), g)
            y_acc[rows, :] = jnp.broadcast_to(bdn_ref[...], (g, d))
            return c

        lax.fori_loop(0, nsub, init, 0)
        gather_wait(i)

        def unpack(sb, c):
            rows = pl.ds(pl.multiple_of(sb * g, g), g)
            for s in range(n_slab):
                lo, hi = _unpack_bf16_pairs(x_raw[pl.ds(sb * (g * n_slab) + s, g, stride=n_slab), :])
                x_lo[rows, s * LANES:(s + 1) * LANES] = lo
                x_hi[rows, s * LANES:(s + 1) * LANES] = hi
            return c

        lax.fori_loop(0, nsub, unpack, 0)

        @pl.when(has_next)
        def _():
            order_copy(nxt).start()

    @pl.when((f == 1) & (nsub > 0) & has_next)
    def _():
        order_copy(nxt).wait()
        gather_start(nxt)

    @pl.when(nsub > 0)
    def _():
        for j in range(ff2 // MXU_DIM):
            cols = slice(j * MXU_DIM, (j + 1) * MXU_DIM)
            wgu_bf[:, cols] = jnp.dot(wgu_ref[:, cols].astype(BF16), perm_ref[...],
                                      preferred_element_type=F32).astype(BF16)
        wdn_bf[...] = wdn_ref[...].astype(BF16)
        bgu = bgu_ref[...]

        def expert_rows(first_row, n_rows):
            rows = pl.ds(pl.multiple_of(first_row, g), n_rows)
            h = (jnp.dot(x_lo[rows, :], wgu_bf[:half, :], preferred_element_type=F32)
                 + jnp.dot(x_hi[rows, :], wgu_bf[half:, :], preferred_element_type=F32) + bgu)
            acts = []
            for j in range(ff2 // MXU_DIM):
                gate = jnp.minimum(h[:, j * MXU_DIM: j * MXU_DIM + LANES], SWIGLU_LIMIT)
                up = jnp.clip(h[:, j * MXU_DIM + LANES: (j + 1) * MXU_DIM], -SWIGLU_LIMIT, SWIGLU_LIMIT)
                acts.append(((up + 1.0) * gate * _sigmoid(SWIGLU_ALPHA * gate)).astype(BF16))
            act = jnp.concatenate(acts, axis=1)
            y_acc[rows, :] += jnp.dot(act, wdn_bf[...], preferred_element_type=F32)

        def pair(p, c):
            expert_rows(p * (2 * g), 2 * g)
            return c

        lax.fori_loop(0, lax.shift_right_logical(nsub, 1), pair, 0)

        @pl.when((nsub & 1) == 1)
        def _():
            expert_rows((nsub - 1) * g, g)

    @pl.when((f == n_ff_tiles - 1) & (nsub > 0))
    def _():
        def out_copy(sb):
            rows = pl.ds(pl.multiple_of(sb * g, g), g)
            dst = pl.ds(pl.multiple_of(start * LANES + sb * g, g), g)
            return pltpu.make_async_copy(y_acc.at[rows, :], ys_hbm.at[dst, :], out_sem)

        def issue(sb, c):
            out_copy(sb).start()
            return c

        def drain(sb, c):
            out_copy(sb).wait()
            return c

        lax.fori_loop(0, nsub, issue, 0)
        lax.fori_loop(0, nsub, drain, 0)

    @pl.when((i == pl.num_programs(0) - 1) & (f == n_ff_tiles - 1))
    def _():
        y_acc[0:g, :] = jnp.zeros((g, d), F32)
        first = tail_ref[0]
        n_tail = ys_hbm.shape[0] // g - first

        def tail_copy(k):
            dst = pl.ds(pl.multiple_of((first + k) * g, g), g)
            return pltpu.make_async_copy(y_acc.at[0:g, :], ys_hbm.at[dst, :], out_sem)

        def issue(k, c):
            tail_copy(k).start()
            return c

        def drain(k, c):
            tail_copy(k).wait()
            return c

        lax.fori_loop(0, n_tail, issue, 0)
        lax.fori_loop(0, n_tail, drain, 0)


def _moe(xp, order2d, plan, w_gu, w_dn, b_gu_perm, b_dn, perm, n_rows):
    n_exp, d, ff2_all = w_gu.shape
    ff = ff2_all // 2
    tf = _tile(ff, MOE_FF_TILE)
    n_f = ff // tf
    assert n_f >= 2, "the next item's row gather is started in the second feature step"
    n_items = plan[0].shape[0]

    def f_eff(i, f, nsub):
        return jnp.where(nsub[i] > 0, f, n_f - 1)

    grid_spec = pltpu.PrefetchScalarGridSpec(
        num_scalar_prefetch=len(plan),
        grid=(n_items, n_f),
        in_specs=[pl.BlockSpec(memory_space=pl.ANY),
                  pl.BlockSpec(memory_space=pl.ANY),
                  pl.BlockSpec((None, d, 2 * tf), lambda i, f, e, s, n, *_: (e[i], 0, f_eff(i, f, n))),
                  pl.BlockSpec((None, tf, d), lambda i, f, e, s, n, *_: (e[i], f_eff(i, f, n), 0)),
                  pl.BlockSpec((None, 1, 2 * tf), lambda i, f, e, s, n, *_: (e[i], 0, f_eff(i, f, n))),
                  pl.BlockSpec((None, 1, d), lambda i, f, e, s, n, *_: (e[i], 0, 0)),
                  pl.BlockSpec((MXU_DIM, MXU_DIM), lambda i, f, e, s, n, *_: (0, 0))],
        out_specs=pl.BlockSpec(memory_space=pl.ANY),
        scratch_shapes=[pltpu.SMEM((MOE_ITEM_ROWS // LANES + 1, LANES), jnp.int32),
                        pltpu.VMEM((MOE_ITEM_ROWS * xp.shape[1], LANES), jnp.uint32),
                        pltpu.VMEM((MOE_ITEM_ROWS, d // 2), BF16),
                        pltpu.VMEM((MOE_ITEM_ROWS, d // 2), BF16),
                        pltpu.VMEM((MOE_ITEM_ROWS, d), F32),
                        pltpu.VMEM((d, 2 * tf), BF16),
                        pltpu.VMEM((tf, d), BF16),
                        pltpu.SemaphoreType.DMA(()),
                        pltpu.SemaphoreType.DMA(()),
                        pltpu.SemaphoreType.DMA(())],
    )
    return pl.pallas_call(
        functools.partial(_moe_kernel, n_ff_tiles=n_f),
        grid_spec=grid_spec,
        out_shape=jax.ShapeDtypeStruct((n_rows, d), F32),
        compiler_params=_params("arbitrary", "arbitrary"),
        name="moe",
    )(*plan, xp, order2d, w_gu, w_dn,
      b_gu_perm.reshape(n_exp, 1, ff2_all), b_dn.reshape(n_exp, 1, d), perm)


def _moe_plan(top_idx, n_exp):
    t_all = top_idx.shape[0]
    n_assign = t_all * TOP_K
    g = MOE_GROUP
    rows_max = n_assign + n_exp * g
    n_items = n_exp + rows_max // MOE_ITEM_ROWS
    n_rows = rows_max + MOE_ITEM_ROWS

    flat_e = top_idx.reshape(-1).astype(jnp.int32)
    assign = jnp.arange(n_assign, dtype=jnp.int32)
    e_sorted, order = lax.sort_key_val(flat_e, assign)
    experts = jnp.arange(n_exp, dtype=jnp.int32)
    first = jnp.sum(e_sorted[:, None] < experts[None, :], axis=0, dtype=jnp.int32)
    counts = jnp.concatenate([first[1:], jnp.full((1,), n_assign, jnp.int32)]) - first
    padded = (counts + g - 1) // g * g
    pad_end = jnp.cumsum(padded)
    pad_start = pad_end - padded
    dest_sorted = pad_start[e_sorted] + assign - first[e_sorted]
    _, dest = lax.sort_key_val(order, dest_sorted)
    order2d = jnp.concatenate([order, jnp.zeros((MOE_ITEM_ROWS + 2 * LANES,), jnp.int32)]).reshape(-1, LANES)

    chunks = (padded + MOE_ITEM_ROWS - 1) // MOE_ITEM_ROWS
    chunk_end = jnp.cumsum(chunks)
    items = jnp.arange(n_items, dtype=jnp.int32)
    n_used = chunk_end[-1]
    last_e = jnp.max(jnp.where(chunks > 0, jnp.arange(n_exp, dtype=jnp.int32), 0))
    e_of = jnp.minimum(jnp.searchsorted(chunk_end, items, side="right"), n_exp - 1).astype(jnp.int32)
    local = items - (chunk_end[e_of] - chunks[e_of])
    used = items < n_used
    item_e = jnp.where(used, e_of, last_e).astype(jnp.int32)
    left = padded[e_of] - local * MOE_ITEM_ROWS
    item_nsub = jnp.where(used, jnp.clip(left, 0, MOE_ITEM_ROWS) // g, 0).astype(jnp.int32)
    item_start = jnp.where(used, (pad_start[e_of] + local * MOE_ITEM_ROWS) // LANES, 0).astype(jnp.int32)
    item_src = jnp.where(used, first[e_of] + local * MOE_ITEM_ROWS, 0).astype(jnp.int32)
    tail = (jnp.sum(padded) // g).astype(jnp.int32).reshape(1)
    return dest, order2d, (item_e, item_start, item_nsub, item_src, tail), n_rows


def _combine_kernel(dest_ref, ys_hbm, wgt_ref, x1_ref, g2_ref, fg_ref, o_ref, gbuf, sem, *, tm):
    tok_per_trip = ROW_UNROLL // TOP_K

    def row_copy(trip, u):
        base = trip * ROW_UNROLL
        d = dest_ref[lax.shift_right_logical(base, 7), (base & (LANES - 1)) + u]
        slot = (u % TOP_K) * tm + trip * tok_per_trip + u // TOP_K
        return pltpu.make_async_copy(ys_hbm.at[pl.ds(d, 1), :], gbuf.at[pl.ds(slot, 1), :], sem)

    def issue(trip, c):
        for u in range(ROW_UNROLL):
            row_copy(trip, u).start()
        return c

    def drain(trip, c):
        for u in range(ROW_UNROLL):
            row_copy(trip, u).wait()
        return c

    lax.fori_loop(0, tm * TOP_K // ROW_UNROLL, issue, 0)
    lax.fori_loop(0, tm * TOP_K // ROW_UNROLL, drain, 0)

    w = wgt_ref[...]
    y = w[:, 0:1] * gbuf[0:tm, :]
    for k in range(1, TOP_K):
        y = y + w[:, k:k + 1] * gbuf[k * tm:(k + 1) * tm, :]
    x = x1_ref[...] + g2_ref[0] * y
    ms = jnp.mean(x * x, axis=-1, keepdims=True)
    o_ref[...] = x * lax.rsqrt(ms + EPS) * fg_ref[...]


def _combine(dest2d, ys, wgt, x1, mod3, final_g, n_ctx_rows, t_lat, *, first_row, n_out_rows, name):
    d = x1.shape[1]
    tm = _tile(min(n_ctx_rows, t_lat), 256)
    row = _mod_row_fn(tm, n_ctx_rows, t_lat)
    idx_rows = tm * TOP_K // LANES
    t0 = first_row // tm
    return pl.pallas_call(
        functools.partial(_combine_kernel, tm=tm),
        grid=(n_out_rows // tm,),
        in_specs=[pl.BlockSpec((idx_rows, LANES), lambda i: (t0 + i, 0), memory_space=pltpu.SMEM),
                  pl.BlockSpec(memory_space=pl.ANY),
                  pl.BlockSpec((tm, LANES), lambda i: (t0 + i, 0)),
                  pl.BlockSpec((tm, d), lambda i: (t0 + i, 0)),
                  pl.BlockSpec((1, 1, d), lambda i: (row(t0 + i) * N_MOD + 5, 0, 0)),
                  pl.BlockSpec((1, d), lambda i: (0, 0))],
        out_specs=pl.BlockSpec((tm, d), lambda i: (i, 0)),
        out_shape=jax.ShapeDtypeStruct((n_out_rows, d), F32),
        scratch_shapes=[pltpu.VMEM((tm * TOP_K, d), F32), pltpu.SemaphoreType.DMA(())],
        compiler_params=_params("arbitrary"),
        name=name,
    )(dest2d, ys, wgt, x1, mod3, final_g.reshape(1, d))


def kernel(x_prompt, x_sample, state_hgrn_fwd, state_hgrn_bwd, state_ret_fwd, state_ret_bwd, c, c_ctx, ada_w, ada_b, norm1_g, norm2_g, final_norm_g, w_in, hg_lb_fwd, hg_lb_bwd, hg_norm_g, ret_log2_fwd, ret_log2_bwd, ret_norm_g, w_proj_hgrn, w_proj_ret, w_out, router_w, router_b, moe_w_gu, moe_b_gu, moe_w_dn, moe_b_dn):
    assert ada_w.shape[0] == 1, "one trunk layer"
    b_ctx, t_ctx, d = x_prompt.shape
    b_lat, t_lat, _ = x_sample.shape
    n_ctx_rows = b_ctx * t_ctx
    assert n_ctx_rows % t_lat == 0 and b_lat + 1 <= COND_ROWS
    n_exp = router_w.shape[2]

    x_ctx = x_prompt.reshape(n_ctx_rows, d)
    x_lat = x_sample.reshape(b_lat * t_lat, d)
    cond = jnp.zeros((COND_ROWS, d), F32).at[0].set(c_ctx).at[1:1 + b_lat].set(c)
    mod3 = _adaln(cond, ada_w[0], ada_b[0]).reshape(COND_ROWS * N_MOD, 1, d)

    proj = _inproj(x_ctx, x_lat, mod3, norm1_g[0], w_in[0], t_lat)

    hg_args = (proj, hg_lb_fwd, hg_lb_bwd, hg_norm_g[0])
    oa_ctx, new_hf, new_hb = _hgrn(*hg_args, None, n_seq=b_ctx, t=t_ctx, row_off=0, emit_state=True)
    (oa_lat,) = _hgrn(*hg_args, (state_hgrn_fwd, state_hgrn_bwd), n_seq=b_lat, t=t_lat,
                      row_off=n_ctx_rows // t_lat, emit_state=False)

    p_f = jnp.broadcast_to(ret_log2_fwd[0][:, None, None], (RET_HEADS, 1, RET_DV))
    p_b = jnp.broadcast_to(ret_log2_bwd[0][:, None, None], (RET_HEADS, 1, RET_DV))
    ret_args = (proj, p_f, p_b, ret_norm_g[0])
    ob_ctx, new_rf, new_rb = _ret(*ret_args, None, None, n_seq=b_ctx, t=t_ctx, row_off=0, emit_state=True)
    (ob_lat,) = _ret(*ret_args, _rope_tables(t_lat), (state_ret_fwd, state_ret_bwd), n_seq=b_lat, t=t_lat,
                     row_off=n_ctx_rows // t_lat, emit_state=False)

    merged = _merge(oa_ctx, oa_lat, ob_ctx, ob_lat, proj, w_proj_hgrn[0].astype(BF16), w_proj_ret[0].astype(BF16))

    rw_pad = jnp.zeros((d, LANES), F32).at[:, :n_exp].set(router_w[0])
    rb_pad = jnp.full((1, LANES), -1e30, F32).at[0, :n_exp].set(router_b[0])
    x1, xp, top_idx, top_w = _router(merged, x_ctx, x_lat, mod3, norm2_g[0], w_out[0].astype(BF16),
                                     rw_pad, rb_pad, t_lat)

    dest, order2d, plan, n_rows = _moe_plan(top_idx[:, :TOP_K], n_exp)
    ff = moe_w_dn.shape[2]
    b_gu_perm = moe_b_gu[0].reshape(n_exp, ff // LANES, LANES, 2).transpose(0, 1, 3, 2).reshape(n_exp, 2 * ff)
    src = jnp.arange(MXU_DIM)
    perm = jnp.zeros((MXU_DIM, MXU_DIM), BF16).at[src, (src % 2) * LANES + src // 2].set(1)
    ys = _moe(xp, order2d, plan, moe_w_gu[0], moe_w_dn[0], b_gu_perm, moe_b_dn[0], perm, n_rows)

    comb_args = (dest.reshape(-1, LANES), ys, top_w, x1, mod3, final_norm_g, n_ctx_rows, t_lat)
    y_prompt = _combine(*comb_args, first_row=0, n_out_rows=n_ctx_rows, name="combine_ctx")
    y_sample = _combine(*comb_args, first_row=n_ctx_rows, n_out_rows=b_lat * t_lat, name="combine_lat")
    return (y_prompt.reshape(b_ctx, t_ctx, d), y_sample.reshape(b_lat, t_lat, d), new_hf, new_hb, new_rf, new_rb)
```

```python
import functools

import jax
import jax.numpy as jnp
from jax import lax
from jax.experimental import pallas as pl
from jax.experimental.pallas import tpu as pltpu

F32 = jnp.float32
BF16 = jnp.bfloat16

HG_HEADS = 8
HG_DK = 128
HG_DV = 128
RET_HEADS = 8
RET_DK = 128
RET_DV = 256
CHUNK = 32
GRID_W = 64
ROPE_BASE = 10000.0
TOP_K = 4
SWIGLU_LIMIT = 7.0
SWIGLU_ALPHA = 1.702
N_MOD = 6
EPS = 1e-6

LANES = 128
MXU_DIM = 256
VMEM_LIMIT = 56 * 1024 * 1024
COND_ROWS = 16
SCAN_BLOCK = 256
SCAN_UNROLL = 8
HGRN_HEADS_PER_STEP = 4
RET_HEADS_PER_STEP = 4
RET_ROWS_PER_STEP = 2048
SMEM_TILE = 1024
MOE_GROUP = 128
MOE_ITEM_ROWS = 2048
MOE_GROUPS_PER_TRIP = 8
MOE_FF_TILE = 256
ROW_UNROLL = 8


def _sigmoid(x):
    return 0.5 * jnp.tanh(0.5 * x) + 0.5


def _params(*sem):
    return pltpu.CompilerParams(dimension_semantics=sem, vmem_limit_bytes=VMEM_LIMIT)


def _tile(n, pref):
    if n <= pref:
        return n
    for t in range(pref - pref % LANES, 0, -LANES):
        if n % t == 0:
            return t
    raise ValueError((n, pref))


def _adaln_kernel(c_ref, w_ref, b_ref, o_ref):
    c = c_ref[...]
    s = (c * _sigmoid(c)).astype(BF16)
    o_ref[...] = jnp.dot(s, w_ref[...].astype(BF16), preferred_element_type=F32) + b_ref[...]


def _adaln(cond, ada_w, ada_b):
    d, n = ada_w.shape
    tn = _tile(n, 1024)
    return pl.pallas_call(
        _adaln_kernel,
        grid=(n // tn,),
        in_specs=[pl.BlockSpec((COND_ROWS, d), lambda j: (0, 0)),
                  pl.BlockSpec((d, tn), lambda j: (0, j)),
                  pl.BlockSpec((1, tn), lambda j: (0, j))],
        out_specs=pl.BlockSpec((COND_ROWS, tn), lambda j: (0, j)),
        out_shape=jax.ShapeDtypeStruct((COND_ROWS, n), F32),
        compiler_params=_params("arbitrary"),
        name="adaln",
    )(cond, ada_w, ada_b.reshape(1, n))


def _prenorm_kernel(xc_ref, xl_ref, sh_ref, sc_ref, g_ref, o_ref, *, n_ctx_tiles):
    x = jnp.where(pl.program_id(0) < n_ctx_tiles, xc_ref[...], xl_ref[...])
    ms = jnp.mean(x * x, axis=-1, keepdims=True)
    xn = x * lax.rsqrt(ms + EPS) * g_ref[...]
    o_ref[...] = (xn * (1.0 + sc_ref[0]) + sh_ref[0]).astype(BF16)


def _inproj_kernel(xm_ref, w_ref, o_ref):
    o_ref[...] = jnp.dot(xm_ref[...], w_ref[...].astype(BF16), preferred_element_type=F32)


def _mod_row_fn(tm, n_ctx_rows, t_lat):
    n_ctx_tiles = n_ctx_rows // tm

    def row(i):
        return jnp.where(i < n_ctx_tiles, 0, 1 + ((i - n_ctx_tiles) * tm) // t_lat)

    return row


def _group_specs(tm, width, n_ctx_tiles):
    ctx = lambda i: (jnp.minimum(i, n_ctx_tiles - 1), 0)
    lat = lambda i: (jnp.maximum(i - n_ctx_tiles, 0), 0)
    return [pl.BlockSpec((tm, width), ctx), pl.BlockSpec((tm, width), lat)]


def _inproj(x_ctx, x_lat, mod3, norm_g, w_in, t_lat):
    n_ctx_rows, d = x_ctx.shape
    t_all = n_ctx_rows + x_lat.shape[0]
    n = w_in.shape[1]
    tp = _tile(min(n_ctx_rows, t_lat), 512)
    row = _mod_row_fn(tp, n_ctx_rows, t_lat)
    xm = pl.pallas_call(
        functools.partial(_prenorm_kernel, n_ctx_tiles=n_ctx_rows // tp),
        grid=(t_all // tp,),
        in_specs=_group_specs(tp, d, n_ctx_rows // tp) + [
            pl.BlockSpec((1, 1, d), lambda i: (row(i) * N_MOD + 0, 0, 0)),
            pl.BlockSpec((1, 1, d), lambda i: (row(i) * N_MOD + 1, 0, 0)),
            pl.BlockSpec((1, d), lambda i: (0, 0))],
        out_specs=pl.BlockSpec((tp, d), lambda i: (i, 0)),
        out_shape=jax.ShapeDtypeStruct((t_all, d), BF16),
        compiler_params=_params("arbitrary"),
        name="prenorm",
    )(x_ctx, x_lat, mod3, mod3, norm_g.reshape(1, d))
    tm = _tile(t_all, 1024)
    tn = _tile(n, 1024)
    return pl.pallas_call(
        _inproj_kernel,
        grid=(t_all // tm, n // tn),
        in_specs=[pl.BlockSpec((tm, d), lambda i, j: (i, 0)),
                  pl.BlockSpec((d, tn), lambda i, j: (0, j))],
        out_specs=pl.BlockSpec((tm, tn), lambda i, j: (i, j)),
        out_shape=jax.ShapeDtypeStruct((t_all, n), F32),
        compiler_params=_params("arbitrary", "arbitrary"),
        name="inproj",
    )(xm, w_in)


def _dot_split(m_bf, x):
    hi = x.astype(BF16)
    lo = (x - hi.astype(F32)).astype(BF16)
    return jnp.dot(m_bf, hi, preferred_element_type=F32) + jnp.dot(m_bf, lo, preferred_element_type=F32)


_NT = (((1,), (1,)), ((), ()))
_TN = (((0,), (0,)), ((), ()))


def _hgrn_kernel(*refs, t, hpg, has_state, emit_state):
    hq_ref, hzf_ref, hzb_ref, hi_ref, hgate_ref, lbf_ref, lbb_ref, gn_ref = refs[:8]
    pos = 8
    if has_state:
        s0f_ref, s0b_ref = refs[pos:pos + 2]
        pos += 2
    out_ref = refs[pos]
    pos += 1
    if emit_state:
        sf_ref, sb_ref = refs[pos:pos + 2]
        pos += 2
    (q_in_f, k_end_f, dec_f, q_in_b, k_end_b, dec_b, v_bf, o_f, o_b, st_ref) = refs[pos:]
    heads = [slice(j * LANES, (j + 1) * LANES) for j in range(hpg)]

    blk = min(SCAN_BLOCK, t)
    n_blk = t // blk
    n_chunk = t // CHUNK

    r = lax.broadcasted_iota(jnp.int32, (blk, blk), 0)
    c = lax.broadcasted_iota(jnp.int32, (blk, blk), 1)
    same = (r // CHUNK) == (c // CHUNK)
    low = same & (c <= r)
    upp = same & (c >= r)
    m_pre = jnp.where(low, 1.0, 0.0).astype(BF16)
    m_suf = jnp.where(upp, 1.0, 0.0).astype(BF16)

    def lower_bound(lb_ref):
        a = lb_ref[...]
        e = jnp.exp(a - jnp.max(a, axis=0, keepdims=True))
        return e[0:1] / jnp.sum(e, axis=0, keepdims=True)

    lb_f = lower_bound(lbf_ref)
    lb_b = lower_bound(lbb_ref)

    width = hpg * LANES

    def chunk_total(g, reverse):
        g3 = g.reshape(blk // CHUNK, CHUNK, width)
        edge = g3[:, 0:1, :] if reverse else g3[:, CHUNK - 1:CHUNK, :]
        return jnp.broadcast_to(edge, g3.shape).reshape(blk, width)

    directions = ((hzf_ref, lb_f, m_pre, low, False, q_in_f, k_end_f, dec_f, o_f),
                  (hzb_ref, lb_b, m_suf, upp, True, q_in_b, k_end_b, dec_b, o_b))

    def prep(i, carry):
        rows = pl.ds(pl.multiple_of(i * blk, blk), blk)
        hq = hq_ref[rows, :]
        q = hq * _sigmoid(hq)
        v = hi_ref[rows, :].astype(BF16)
        v_bf[rows, :] = v
        for z_ref, lb, m_cum, mask, reverse, qi, ke, dc, o_ref in directions:
            f = lb + (1.0 - lb) * _sigmoid(z_ref[rows, :])
            k = 1.0 - f
            g = _dot_split(m_cum, jnp.log(f))
            tot = chunk_total(g, reverse)
            q_in = (q * jnp.exp(g)).astype(BF16)
            k_in = (k * jnp.exp(-g)).astype(BF16)
            qi[rows, :] = q_in
            ke[rows, :] = (k * jnp.exp(tot - g)).astype(BF16)
            dc[rows, :] = jnp.exp(tot)
            for hd in heads:
                s = lax.dot_general(q_in[:, hd], k_in[:, hd], _NT, preferred_element_type=F32)
                o_ref[rows, hd] = jnp.dot(jnp.where(mask, s, 0.0).astype(BF16), v[:, hd],
                                          preferred_element_type=F32)
        return carry

    lax.fori_loop(0, n_blk, prep, 0)

    for j in range(hpg):
        for slot, s0_ref in ((2 * j, s0f_ref if has_state else None), (2 * j + 1, s0b_ref if has_state else None)):
            st_ref[slot] = s0_ref[j].T if has_state else jnp.zeros((HG_DV, HG_DK), F32)

    def chunk_step(idx, slot, hd, qi, ke, dc, o_ref):
        start = pl.multiple_of(idx * CHUNK, CHUNK)
        rows = pl.ds(start, CHUNK)
        st = st_ref[slot]
        o_ref[rows, hd] += lax.dot_general(qi[rows, hd], st.astype(BF16), _NT, preferred_element_type=F32)
        upd = lax.dot_general(v_bf[rows, hd], ke[rows, hd], _TN, preferred_element_type=F32)
        st_ref[slot] = dc[pl.ds(start, 1), hd] * st + upd

    def body(i, carry):
        for j, hd in enumerate(heads):
            chunk_step(i, 2 * j, hd, q_in_f, k_end_f, dec_f, o_f)
            chunk_step(n_chunk - 1 - i, 2 * j + 1, hd, q_in_b, k_end_b, dec_b, o_b)
        return carry

    lax.fori_loop(0, n_chunk, body, 0, unroll=min(n_chunk, SCAN_UNROLL))
    if emit_state:
        for j in range(hpg):
            sf_ref[j] = st_ref[2 * j].T
            sb_ref[j] = st_ref[2 * j + 1].T

    for hd in heads:
        o = o_f[:, hd] + o_b[:, hd]
        o = o * lax.rsqrt(jnp.mean(o * o, axis=-1, keepdims=True) + EPS) * gn_ref[...]
        out_ref[:, hd] = (o * _sigmoid(hgate_ref[:, hd])).astype(BF16)


def _hgrn(proj, lb_fwd, lb_bwd, norm_g, states, *, n_seq, t, row_off, emit_state):
    nh = HG_HEADS
    hpg = min(nh, HGRN_HEADS_PER_STEP)
    assert nh % hpg == 0
    ng = nh // hpg
    w = hpg * LANES
    col = lambda k: (lambda b, h: (row_off + b, k * ng + h))
    blk = lambda k: pl.BlockSpec((t, w), col(k))
    in_specs = [blk(0), blk(1), blk(2), blk(3), blk(4),
                pl.BlockSpec((2, w), lambda b, h: (0, h)),
                pl.BlockSpec((2, w), lambda b, h: (0, h)),
                pl.BlockSpec((1, LANES), lambda b, h: (0, 0))]
    args = [proj] * 5 + [lb_fwd, lb_bwd, norm_g.reshape(1, HG_DV)]
    st_spec = pl.BlockSpec((None, None, hpg, HG_DK, HG_DV), lambda b, h: (b, 0, h, 0, 0))
    if states is not None:
        in_specs += [st_spec, st_spec]
        args += list(states)
    out_specs = [pl.BlockSpec((t, w), lambda b, h: (b, h))]
    out_shape = [jax.ShapeDtypeStruct((n_seq * t, nh * HG_DV), BF16)]
    if emit_state:
        out_specs += [st_spec, st_spec]
        out_shape += [jax.ShapeDtypeStruct((n_seq, 1, nh, HG_DK, HG_DV), F32)] * 2
    scratch = ([pltpu.VMEM((t, w), BF16)] * 2 + [pltpu.VMEM((t, w), F32)]) * 2
    scratch += [pltpu.VMEM((t, w), BF16), pltpu.VMEM((t, w), F32), pltpu.VMEM((t, w), F32),
                pltpu.VMEM((2 * hpg, HG_DV, HG_DK), F32)]
    return pl.pallas_call(
        functools.partial(_hgrn_kernel, t=t, hpg=hpg, has_state=states is not None, emit_state=emit_state),
        grid=(n_seq, ng),
        in_specs=in_specs,
        out_specs=out_specs,
        out_shape=out_shape,
        scratch_shapes=scratch,
        compiler_params=_params("arbitrary", "arbitrary"),
        name="hgrn_lat" if states is not None else "hgrn_ctx",
    )(*args)


def _ret_kernel(*refs, t, hpg, has_state, emit_state, use_rope):
    rq_ref, rk_ref, rv_ref, rgate_ref, pf_ref, pb_ref, gn_ref = refs[:7]
    pos = 7
    if use_rope:
        cos_ref, sin_ref = refs[pos:pos + 2]
        pos += 2
    if has_state:
        s0f_ref, s0b_ref = refs[pos:pos + 2]
        pos += 2
    out_ref = refs[pos]
    pos += 1
    if emit_state:
        sf_ref, sb_ref = refs[pos:pos + 2]
        pos += 2
    q_s, k_s, v_s, o_f, o_b, st_ref = refs[pos:]

    ch = min(MXU_DIM, t)
    n_chunk = t // ch

    lane = lax.broadcasted_iota(jnp.int32, (t, RET_DK), 1)
    first_half = (lane % 64) < 32

    def rope(x):
        if not use_rope:
            return x
        swapped = jnp.where(first_half, pltpu.roll(x, RET_DK - 32, 1), pltpu.roll(x, 32, 1))
        return x * cos_ref[...] + swapped * sin_ref[...]

    ri = lax.broadcasted_iota(jnp.int32, (ch, ch), 0)
    ci = lax.broadcasted_iota(jnp.int32, (ch, ch), 1)
    row128 = lax.broadcasted_iota(jnp.int32, (ch, RET_DK), 0).astype(F32)

    def direction(j, p_ref, slot, o_scr, s0_ref, reverse):
        qk = slice(j * RET_DK, (j + 1) * RET_DK)
        vv = slice(j * RET_DV, (j + 1) * RET_DV)
        lg = jnp.log1p(-jnp.exp2(p_ref[j]))
        lg_k = lg[:, :RET_DK]
        dist = (ci - ri) if reverse else (ri - ci)
        keep = dist >= 0
        dmask = jnp.where(keep, jnp.exp(lg[:, :ch] * jnp.where(keep, dist, 0).astype(F32)), 0.0)
        if reverse:
            q_scale = jnp.exp(lg_k * (ch - row128))
            k_scale = jnp.exp(lg_k * row128)
        else:
            q_scale = jnp.exp(lg_k * (row128 + 1.0))
            k_scale = jnp.exp(lg_k * (ch - 1.0 - row128))
        dec = jnp.exp(lg * float(ch))
        st_ref[slot] = s0_ref[j] if has_state else jnp.zeros((RET_DK, RET_DV), F32)
        for step in range(n_chunk):
            cidx = n_chunk - 1 - step if reverse else step
            rows = pl.ds(cidx * ch, ch)
            q = q_s[rows, qk]
            k = k_s[rows, qk]
            v = v_s[rows, vv]
            s = lax.dot_general(q.astype(BF16), k.astype(BF16), _NT, preferred_element_type=F32)
            o = jnp.dot((s * dmask).astype(BF16), v, preferred_element_type=F32)
            st = st_ref[slot]
            o = o + jnp.dot((q * q_scale).astype(BF16), st.astype(BF16), preferred_element_type=F32)
            o_scr[rows, vv] = o
            upd = lax.dot_general((k * k_scale).astype(BF16), v, _TN, preferred_element_type=F32)
            st_ref[slot] = dec * st + upd

    for j in range(hpg):
        qk = slice(j * RET_DK, (j + 1) * RET_DK)
        q_s[:, qk] = rope(rq_ref[:, qk])
        k_s[:, qk] = rope(rk_ref[:, qk] * (RET_DK ** -0.5))
    v_s[...] = rv_ref[...].astype(BF16)
    for j in range(hpg):
        direction(j, pf_ref, 2 * j, o_f, s0f_ref if has_state else None, False)
        direction(j, pb_ref, 2 * j + 1, o_b, s0b_ref if has_state else None, True)
    if emit_state:
        for j in range(hpg):
            sf_ref[j] = st_ref[2 * j]
            sb_ref[j] = st_ref[2 * j + 1]

    for j in range(hpg):
        vv = slice(j * RET_DV, (j + 1) * RET_DV)
        o = o_f[:, vv] + o_b[:, vv]
        oc = o - jnp.mean(o, axis=-1, keepdims=True)
        o = oc * lax.rsqrt(jnp.mean(oc * oc, axis=-1, keepdims=True) + EPS) * gn_ref[...]
        gate = rgate_ref[:, vv]
        out_ref[:, vv] = (o * (gate * _sigmoid(gate))).astype(BF16)


def _ret(proj, p_fwd, p_bwd, norm_g, rope, states, *, n_seq, t, row_off, emit_state):
    nh = RET_HEADS
    base = (3 * HG_HEADS * HG_DK + 2 * HG_HEADS * HG_DV)
    hpg = max(1, min(nh, RET_HEADS_PER_STEP, RET_ROWS_PER_STEP // t))
    while nh % hpg or base % (hpg * RET_DK) or (base + 2 * nh * RET_DK) % (hpg * RET_DV):
        hpg -= 1
    ng = nh // hpg
    wk, wv = hpg * RET_DK, hpg * RET_DV
    qk0 = base // wk
    v0 = (base + 2 * nh * RET_DK) // wv
    in_specs = [pl.BlockSpec((t, wk), lambda b, h: (row_off + b, qk0 + h)),
                pl.BlockSpec((t, wk), lambda b, h: (row_off + b, qk0 + ng + h)),
                pl.BlockSpec((t, wv), lambda b, h: (row_off + b, v0 + h)),
                pl.BlockSpec((t, wv), lambda b, h: (row_off + b, v0 + ng + h)),
                pl.BlockSpec((hpg, 1, RET_DV), lambda b, h: (h, 0, 0)),
                pl.BlockSpec((hpg, 1, RET_DV), lambda b, h: (h, 0, 0)),
                pl.BlockSpec((1, RET_DV), lambda b, h: (0, 0))]
    args = [proj] * 4 + [p_fwd, p_bwd, norm_g.reshape(1, RET_DV)]
    if rope is not None:
        in_specs += [pl.BlockSpec((t, RET_DK), lambda b, h: (0, 0))] * 2
        args += list(rope)
    st_spec = pl.BlockSpec((None, None, hpg, RET_DK, RET_DV), lambda b, h: (b, 0, h, 0, 0))
    if states is not None:
        in_specs += [st_spec, st_spec]
        args += list(states)
    out_specs = [pl.BlockSpec((t, wv), lambda b, h: (b, h))]
    out_shape = [jax.ShapeDtypeStruct((n_seq * t, nh * RET_DV), BF16)]
    if emit_state:
        out_specs += [st_spec, st_spec]
        out_shape += [jax.ShapeDtypeStruct((n_seq, 1, nh, RET_DK, RET_DV), F32)] * 2
    scratch = [pltpu.VMEM((t, wk), F32), pltpu.VMEM((t, wk), F32), pltpu.VMEM((t, wv), BF16),
               pltpu.VMEM((t, wv), F32), pltpu.VMEM((t, wv), F32),
               pltpu.VMEM((2 * hpg, RET_DK, RET_DV), F32)]
    return pl.pallas_call(
        functools.partial(_ret_kernel, t=t, hpg=hpg, has_state=states is not None, emit_state=emit_state,
                          use_rope=rope is not None),
        grid=(n_seq, ng),
        in_specs=in_specs,
        out_specs=out_specs,
        out_shape=out_shape,
        scratch_shapes=scratch,
        compiler_params=_params("arbitrary", "arbitrary"),
        name="ret_lat" if states is not None else "ret_ctx",
    )(*args)


def _rope_tables(t):
    pairs = RET_DK // 4
    pos = jnp.arange(t, dtype=jnp.int32)
    rows = (pos // GRID_W).astype(F32)
    cols = (pos % GRID_W).astype(F32)
    inv = ROPE_BASE ** (-jnp.arange(pairs, dtype=F32) / pairs)
    ar = rows[:, None] * inv
    ac = cols[:, None] * inv
    cos = jnp.concatenate([jnp.cos(ar), jnp.cos(ar), jnp.cos(ac), jnp.cos(ac)], axis=1)
    sin = jnp.concatenate([-jnp.sin(ar), jnp.sin(ar), -jnp.sin(ac), jnp.sin(ac)], axis=1)
    return cos, sin


def _merge_kernel(oac_ref, oal_ref, obc_ref, obl_ref, ga_ref, gb_ref, wa_ref, wb_ref, o_ref, *, n_ctx_tiles):
    is_ctx = pl.program_id(1) < n_ctx_tiles
    oa = jnp.where(is_ctx, oac_ref[...], oal_ref[...])
    ob = jnp.where(is_ctx, obc_ref[...], obl_ref[...])
    a = jnp.dot(oa, wa_ref[...], preferred_element_type=F32)
    b = jnp.dot(ob, wb_ref[...], preferred_element_type=F32)
    o_ref[...] = (_sigmoid(ga_ref[...]) * a + _sigmoid(gb_ref[...]) * b).astype(BF16)


def _merge(oa_ctx, oa_lat, ob_ctx, ob_lat, proj, wa_bf, wb_bf):
    n_ctx_rows, ka = oa_ctx.shape
    t_all = n_ctx_rows + oa_lat.shape[0]
    kb = ob_ctx.shape[1]
    d = wa_bf.shape[1]
    tm = _tile(min(n_ctx_rows, oa_lat.shape[0]), 512)
    tn = _tile(d, 1024)
    nct = n_ctx_rows // tm
    gate0 = (proj.shape[1] - 2 * d) // tn
    ctx = lambda j, i: (jnp.minimum(i, nct - 1), 0)
    lat = lambda j, i: (jnp.maximum(i - nct, 0), 0)
    return pl.pallas_call(
        functools.partial(_merge_kernel, n_ctx_tiles=nct),
        grid=(d // tn, t_all // tm),
        in_specs=[pl.BlockSpec((tm, ka), ctx), pl.BlockSpec((tm, ka), lat),
                  pl.BlockSpec((tm, kb), ctx), pl.BlockSpec((tm, kb), lat),
                  pl.BlockSpec((tm, tn), lambda j, i: (i, gate0 + j)),
                  pl.BlockSpec((tm, tn), lambda j, i: (i, gate0 + d // tn + j)),
                  pl.BlockSpec((ka, tn), lambda j, i: (0, j)),
                  pl.BlockSpec((kb, tn), lambda j, i: (0, j))],
        out_specs=pl.BlockSpec((tm, tn), lambda j, i: (i, j)),
        out_shape=jax.ShapeDtypeStruct((t_all, d), BF16),
        compiler_params=_params("arbitrary", "arbitrary"),
        name="merge",
    )(oa_ctx, oa_lat, ob_ctx, ob_lat, proj, proj, wa_bf, wb_bf)


def _pack_bf16_pairs(x):
    half = x.shape[1] // 2
    lo = lax.bitcast_convert_type(x[:, :half].astype(BF16).astype(F32), jnp.uint32)
    hi = lax.bitcast_convert_type(x[:, half:].astype(BF16).astype(F32), jnp.uint32)
    return (lo >> 16) | (hi & jnp.uint32(0xFFFF0000))


def _unpack_bf16_pairs(w):
    lo = lax.bitcast_convert_type(w << 16, F32).astype(BF16)
    hi = lax.bitcast_convert_type(w & jnp.uint32(0xFFFF0000), F32).astype(BF16)
    return lo, hi


def _router_kernel(m_ref, xc_ref, xl_ref, g1_ref, sh_ref, sc_ref, ng_ref, wo_ref, rw2_ref, rwh_ref, rb_ref,
                   x1_ref, xp_ref, idx_ref, wgt_ref, *, n_ctx_tiles):
    out = jnp.dot(m_ref[...], wo_ref[...], preferred_element_type=F32)
    x = jnp.where(pl.program_id(0) < n_ctx_tiles, xc_ref[...], xl_ref[...])
    x1 = x + g1_ref[0] * out
    x1_ref[...] = x1
    ms = jnp.mean(x1 * x1, axis=-1, keepdims=True)
    xm = x1 * lax.rsqrt(ms + EPS) * ng_ref[...]
    xm = xm * (1.0 + sc_ref[0]) + sh_ref[0]
    packed = _pack_bf16_pairs(xm)
    n_slab = packed.shape[1] // LANES
    for s in range(n_slab):
        xp_ref[pl.ds(s, packed.shape[0], stride=n_slab), :] = packed[:, s * LANES:(s + 1) * LANES]

    x_hi = xm.astype(BF16)
    x_lo = (xm - x_hi.astype(F32)).astype(BF16)
    hh_hl = jnp.dot(x_hi, rw2_ref[...], preferred_element_type=F32)
    lh = jnp.dot(x_lo, rwh_ref[...], preferred_element_type=F32)
    logits = hh_hl[:, :LANES] + (hh_hl[:, LANES:] + lh) + rb_ref[...]
    lane = lax.broadcasted_iota(jnp.int32, logits.shape, 1)
    lane_f = lane.astype(F32)
    idx_out = jnp.zeros(logits.shape, F32)
    val_out = jnp.zeros(logits.shape, F32)
    top = None
    for k in range(TOP_K):
        m = jnp.max(logits, axis=-1, keepdims=True)
        idx = jnp.min(jnp.where(logits == m, lane_f, float(LANES)), axis=-1, keepdims=True)
        if top is None:
            top = m
        idx_out = jnp.where(lane == k, idx, idx_out)
        val_out = jnp.where(lane == k, jnp.exp(m - top), val_out)
        logits = jnp.where(lane_f == idx, -jnp.inf, logits)
    idx_ref[...] = idx_out.astype(jnp.int32)
    wgt_ref[...] = val_out / jnp.sum(val_out, axis=-1, keepdims=True)


def _router(merged, x_ctx, x_lat, mod3, norm_g, wo_bf, rw_pad, rb_pad, t_lat):
    n_ctx_rows, d = x_ctx.shape
    t_all = n_ctx_rows + x_lat.shape[0]
    tm = _tile(min(n_ctx_rows, t_lat), 256)
    row = _mod_row_fn(tm, n_ctx_rows, t_lat)
    mod_spec = lambda k: pl.BlockSpec((1, 1, d), lambda i: (row(i) * N_MOD + k, 0, 0))
    rw_hi = rw_pad.astype(BF16)
    rw_lo = (rw_pad - rw_hi.astype(F32)).astype(BF16)
    rw2 = jnp.concatenate([rw_hi, rw_lo], axis=1)
    return pl.pallas_call(
        functools.partial(_router_kernel, n_ctx_tiles=n_ctx_rows // tm),
        grid=(t_all // tm,),
        in_specs=[pl.BlockSpec((tm, d), lambda i: (i, 0))] + _group_specs(tm, d, n_ctx_rows // tm) + [
                  mod_spec(2), mod_spec(3), mod_spec(4),
                  pl.BlockSpec((1, d), lambda i: (0, 0)),
                  pl.BlockSpec((d, d), lambda i: (0, 0)),
                  pl.BlockSpec((d, 2 * LANES), lambda i: (0, 0)),
                  pl.BlockSpec((d, LANES), lambda i: (0, 0)),
                  pl.BlockSpec((1, LANES), lambda i: (0, 0))],
        out_specs=[pl.BlockSpec((tm, d), lambda i: (i, 0)),
                   pl.BlockSpec((tm * (d // 2 // LANES), LANES), lambda i: (i, 0)),
                   pl.BlockSpec((tm, LANES), lambda i: (i, 0)),
                   pl.BlockSpec((tm, LANES), lambda i: (i, 0))],
        out_shape=[jax.ShapeDtypeStruct((t_all, d), F32),
                   jax.ShapeDtypeStruct((t_all * (d // 2 // LANES), LANES), jnp.uint32),
                   jax.ShapeDtypeStruct((t_all, LANES), jnp.int32),
                   jax.ShapeDtypeStruct((t_all, LANES), F32)],
        compiler_params=_params("arbitrary"),
        name="router",
    )(merged, x_ctx, x_lat, mod3, mod3, mod3, norm_g.reshape(1, d), wo_bf, rw2, rw_hi, rb_pad)


def _moe_kernel(e_ref, start_ref, nsub_ref, src_ref, tail_ref,
                xp_hbm, order_hbm, wgu_ref, wdn_ref, bgu_ref, bdn_ref, perm_ref,
                ys_hbm,
                tok_smem, x_raw, x_lo, x_hi, y_acc, wgu_bf, wdn_bf, tok_sem, row_sem, out_sem,
                *, n_ff_tiles):
    i = pl.program_id(0)
    f = pl.program_id(1)
    nsub = nsub_ref[i]
    start = start_ref[i]
    n_slab = xp_hbm.shape[1]
    g = MOE_GROUP
    d = y_acc.shape[1]
    half = d // 2
    ff2 = wgu_bf.shape[1]

    def order_copy(it):
        first = pl.multiple_of(src_ref[it] & ~(SMEM_TILE - 1), SMEM_TILE)
        return pltpu.make_async_copy(order_hbm.at[pl.ds(first, tok_smem.shape[0])], tok_smem, tok_sem)

    def row_copy(off, base, u):
        tok = lax.shift_right_logical(tok_smem[off + base + u], 2)
        return pltpu.make_async_copy(xp_hbm.at[tok], x_raw.at[pl.ds((base + u) * n_slab, n_slab), :], row_sem)

    def gather_start(it):
        off = src_ref[it] & (SMEM_TILE - 1)

        def issue(r, c):
            for u in range(ROW_UNROLL):
                row_copy(off, r * ROW_UNROLL, u).start()
            return c

        lax.fori_loop(0, nsub_ref[it] * (g // ROW_UNROLL), issue, 0)

    def gather_wait(it):
        off = src_ref[it] & (SMEM_TILE - 1)

        def drain(r, c):
            for u in range(ROW_UNROLL):
                row_copy(off, r * ROW_UNROLL, u).wait()
            return c

        lax.fori_loop(0, nsub_ref[it] * (g // ROW_UNROLL), drain, 0)

    nxt = jnp.minimum(i + 1, pl.num_programs(0) - 1)
    has_next = (i + 1 < pl.num_programs(0)) & (nsub_ref[nxt] > 0)

    @pl.when((i == 0) & (f == 0) & (nsub > 0))
    def _():
        order_copy(0).start()
        order_copy(0).wait()
        gather_start(0)

    @pl.when((f == 0) & (nsub > 0))
    def _():
        def init(sb, c):
            rows = pl.ds(pl.multiple_of(sb * g, g), g)
            y_acc[rows, :] = jnp.broadcast_to(bdn_ref[...], (g, d))
            return c

        lax.fori_loop(0, nsub, init, 0)
        gather_wait(i)

        def unpack(sb, c):
            rows = pl.ds(pl.multiple_of(sb * g, g), g)
            for s in range(n_slab):
                lo, hi = _unpack_bf16_pairs(x_raw[pl.ds(sb * (g * n_slab) + s, g, stride=n_slab), :])
                x_lo[rows, s * LANES:(s + 1) * LANES] = lo
                x_hi[rows, s * LANES:(s + 1) * LANES] = hi
            return c

        lax.fori_loop(0, nsub, unpack, 0)

        @pl.when(has_next)
        def _():
            order_copy(nxt).start()

    @pl.when((f == 1) & (nsub > 0) & has_next)
    def _():
        order_copy(nxt).wait()
        gather_start(nxt)

    @pl.when(nsub > 0)
    def _():
        for j in range(ff2 // MXU_DIM):
            cols = slice(j * MXU_DIM, (j + 1) * MXU_DIM)
            wgu_bf[:, cols] = jnp.dot(wgu_ref[:, cols].astype(BF16), perm_ref[...],
                                      preferred_element_type=F32).astype(BF16)
        wdn_bf[...] = wdn_ref[...].astype(BF16)
        bgu = bgu_ref[...]

        def expert_rows(first_row, n_rows):
            rows = pl.ds(pl.multiple_of(first_row, g), n_rows)
            h = (jnp.dot(x_lo[rows, :], wgu_bf[:half, :], preferred_element_type=F32)
                 + jnp.dot(x_hi[rows, :], wgu_bf[half:, :], preferred_element_type=F32) + bgu)
            acts = []
            for j in range(ff2 // MXU_DIM):
                gate = jnp.minimum(h[:, j * MXU_DIM: j * MXU_DIM + LANES], SWIGLU_LIMIT)
                up = jnp.clip(h[:, j * MXU_DIM + LANES: (j + 1) * MXU_DIM], -SWIGLU_LIMIT, SWIGLU_LIMIT)
                acts.append(((up + 1.0) * gate * _sigmoid(SWIGLU_ALPHA * gate)).astype(BF16))
            act = jnp.concatenate(acts, axis=1)
            y_acc[rows, :] += jnp.dot(act, wdn_bf[...], preferred_element_type=F32)

        big = MOE_GROUPS_PER_TRIP

        def trip(q, c):
            expert_rows(q * (big * g), big * g)
            return c

        lax.fori_loop(0, lax.shift_right_logical(nsub, big.bit_length() - 1), trip, 0)
        rem = nsub & (big - 1)
        part = big // 2
        while part >= 1:
            @pl.when((rem & part) != 0)
            def _(part=part):
                expert_rows((nsub - rem + (rem & ~(2 * part - 1))) * g, part * g)
            part //= 2

    @pl.when((f == n_ff_tiles - 1) & (nsub > 0))
    def _():
        def out_copy(sb):
            rows = pl.ds(pl.multiple_of(sb * g, g), g)
            dst = pl.ds(pl.multiple_of(start * LANES + sb * g, g), g)
            return pltpu.make_async_copy(y_acc.at[rows, :], ys_hbm.at[dst, :], out_sem)

        def issue(sb, c):
            out_copy(sb).start()
            return c

        def drain(sb, c):
            out_copy(sb).wait()
            return c

        lax.fori_loop(0, nsub, issue, 0)
        lax.fori_loop(0, nsub, drain, 0)

    @pl.when((i == pl.num_programs(0) - 1) & (f == n_ff_tiles - 1))
    def _():
        y_acc[0:g, :] = jnp.zeros((g, d), F32)
        first = tail_ref[0]
        n_tail = ys_hbm.shape[0] // g - first

        def tail_copy(k):
            dst = pl.ds(pl.multiple_of((first + k) * g, g), g)
            return pltpu.make_async_copy(y_acc.at[0:g, :], ys_hbm.at[dst, :], out_sem)

        def issue(k, c):
            tail_copy(k).start()
            return c

        def drain(k, c):
            tail_copy(k).wait()
            return c

        lax.fori_loop(0, n_tail, issue, 0)
        lax.fori_loop(0, n_tail, drain, 0)


def _moe(xp, order2d, plan, w_gu, w_dn, b_gu_perm, b_dn, perm, n_rows):
    n_exp, d, ff2_all = w_gu.shape
    ff = ff2_all // 2
    tf = _tile(ff, MOE_FF_TILE)
    n_f = ff // tf
    assert n_f >= 2, "the next item's row gather is started in the second feature step"
    n_items = plan[0].shape[0]

    def f_eff(i, f, nsub):
        return jnp.where(nsub[i] > 0, f, n_f - 1)

    grid_spec = pltpu.PrefetchScalarGridSpec(
        num_scalar_prefetch=len(plan),
        grid=(n_items, n_f),
        in_specs=[pl.BlockSpec(memory_space=pl.ANY),
                  pl.BlockSpec(memory_space=pl.ANY),
                  pl.BlockSpec((None, d, 2 * tf), lambda i, f, e, s, n, *_: (e[i], 0, f_eff(i, f, n))),
                  pl.BlockSpec((None, tf, d), lambda i, f, e, s, n, *_: (e[i], f_eff(i, f, n), 0)),
                  pl.BlockSpec((None, 1, 2 * tf), lambda i, f, e, s, n, *_: (e[i], 0, f_eff(i, f, n))),
                  pl.BlockSpec((None, 1, d), lambda i, f, e, s, n, *_: (e[i], 0, 0)),
                  pl.BlockSpec((MXU_DIM, MXU_DIM), lambda i, f, e, s, n, *_: (0, 0))],
        out_specs=pl.BlockSpec(memory_space=pl.ANY),
        scratch_shapes=[pltpu.SMEM((MOE_ITEM_ROWS + SMEM_TILE,), jnp.int32),
                        pltpu.VMEM((MOE_ITEM_ROWS * xp.shape[1], LANES), jnp.uint32),
                        pltpu.VMEM((MOE_ITEM_ROWS, d // 2), BF16),
                        pltpu.VMEM((MOE_ITEM_ROWS, d // 2), BF16),
                        pltpu.VMEM((MOE_ITEM_ROWS, d), F32),
                        pltpu.VMEM((d, 2 * tf), BF16),
                        pltpu.VMEM((tf, d), BF16),
                        pltpu.SemaphoreType.DMA(()),
                        pltpu.SemaphoreType.DMA(()),
                        pltpu.SemaphoreType.DMA(())],
    )
    return pl.pallas_call(
        functools.partial(_moe_kernel, n_ff_tiles=n_f),
        grid_spec=grid_spec,
        out_shape=jax.ShapeDtypeStruct((n_rows, d), F32),
        compiler_params=_params("arbitrary", "arbitrary"),
        name="moe",
    )(*plan, xp, order2d, w_gu, w_dn,
      b_gu_perm.reshape(n_exp, 1, ff2_all), b_dn.reshape(n_exp, 1, d), perm)


def _moe_plan(top_idx, n_exp):
    t_all = top_idx.shape[0]
    n_assign = t_all * TOP_K
    g = MOE_GROUP
    rows_max = n_assign + n_exp * g
    n_items = n_exp + rows_max // MOE_ITEM_ROWS
    n_rows = rows_max + MOE_ITEM_ROWS

    flat_e = top_idx.reshape(-1).astype(jnp.int32)
    assign = jnp.arange(n_assign, dtype=jnp.int32)
    e_sorted, order = lax.sort_key_val(flat_e, assign)
    experts = jnp.arange(n_exp, dtype=jnp.int32)
    first = jnp.sum(e_sorted[:, None] < experts[None, :], axis=0, dtype=jnp.int32)
    counts = jnp.concatenate([first[1:], jnp.full((1,), n_assign, jnp.int32)]) - first
    padded = (counts + g - 1) // g * g
    pad_end = jnp.cumsum(padded)
    pad_start = pad_end - padded
    dest_sorted = pad_start[e_sorted] + assign - first[e_sorted]
    _, dest = lax.sort_key_val(order, dest_sorted)
    assert n_assign % SMEM_TILE == 0
    order2d = jnp.concatenate([order, jnp.zeros((MOE_ITEM_ROWS + SMEM_TILE,), jnp.int32)])

    chunks = (padded + MOE_ITEM_ROWS - 1) // MOE_ITEM_ROWS
    chunk_end = jnp.cumsum(chunks)
    items = jnp.arange(n_items, dtype=jnp.int32)
    n_used = chunk_end[-1]
    last_e = jnp.max(jnp.where(chunks > 0, jnp.arange(n_exp, dtype=jnp.int32), 0))
    e_of = jnp.minimum(jnp.searchsorted(chunk_end, items, side="right"), n_exp - 1).astype(jnp.int32)
    local = items - (chunk_end[e_of] - chunks[e_of])
    used = items < n_used
    item_e = jnp.where(used, e_of, last_e).astype(jnp.int32)
    left = padded[e_of] - local * MOE_ITEM_ROWS
    item_nsub = jnp.where(used, jnp.clip(left, 0, MOE_ITEM_ROWS) // g, 0).astype(jnp.int32)
    item_start = jnp.where(used, (pad_start[e_of] + local * MOE_ITEM_ROWS) // LANES, 0).astype(jnp.int32)
    item_src = jnp.where(used, first[e_of] + local * MOE_ITEM_ROWS, 0).astype(jnp.int32)
    tail = (jnp.sum(padded) // g).astype(jnp.int32).reshape(1)
    return dest, order2d, (item_e, item_start, item_nsub, item_src, tail), n_rows


def _combine_kernel(dest_ref, ys_hbm, wgt_ref, x1_ref, g2_ref, fg_ref, o_ref, gbuf, sem, *, tm):
    tok_per_trip = ROW_UNROLL // TOP_K

    def row_copy(trip, u):
        d = dest_ref[trip * ROW_UNROLL + u]
        slot = (u % TOP_K) * tm + trip * tok_per_trip + u // TOP_K
        return pltpu.make_async_copy(ys_hbm.at[pl.ds(d, 1), :], gbuf.at[pl.ds(slot, 1), :], sem)

    def issue(trip, c):
        for u in range(ROW_UNROLL):
            row_copy(trip, u).start()
        return c

    def drain(trip, c):
        for u in range(ROW_UNROLL):
            row_copy(trip, u).wait()
        return c

    lax.fori_loop(0, tm * TOP_K // ROW_UNROLL, issue, 0)
    lax.fori_loop(0, tm * TOP_K // ROW_UNROLL, drain, 0)

    w = wgt_ref[...]
    y = w[:, 0:1] * gbuf[0:tm, :]
    for k in range(1, TOP_K):
        y = y + w[:, k:k + 1] * gbuf[k * tm:(k + 1) * tm, :]
    x = x1_ref[...] + g2_ref[0] * y
    ms = jnp.mean(x * x, axis=-1, keepdims=True)
    o_ref[...] = x * lax.rsqrt(ms + EPS) * fg_ref[...]


def _combine(dest2d, ys, wgt, x1, mod3, final_g, n_ctx_rows, t_lat, *, first_row, n_out_rows, name):
    d = x1.shape[1]
    tm = _tile(min(n_ctx_rows, t_lat), 256)
    row = _mod_row_fn(tm, n_ctx_rows, t_lat)
    assert (tm * TOP_K) % SMEM_TILE == 0
    t0 = first_row // tm
    return pl.pallas_call(
        functools.partial(_combine_kernel, tm=tm),
        grid=(n_out_rows // tm,),
        in_specs=[pl.BlockSpec((tm * TOP_K,), lambda i: (t0 + i,), memory_space=pltpu.SMEM),
                  pl.BlockSpec(memory_space=pl.ANY),
                  pl.BlockSpec((tm, LANES), lambda i: (t0 + i, 0)),
                  pl.BlockSpec((tm, d), lambda i: (t0 + i, 0)),
                  pl.BlockSpec((1, 1, d), lambda i: (row(t0 + i) * N_MOD + 5, 0, 0)),
                  pl.BlockSpec((1, d), lambda i: (0, 0))],
        out_specs=pl.BlockSpec((tm, d), lambda i: (i, 0)),
        out_shape=jax.ShapeDtypeStruct((n_out_rows, d), F32),
        scratch_shapes=[pltpu.VMEM((tm * TOP_K, d), F32), pltpu.SemaphoreType.DMA(())],
        compiler_params=_params("arbitrary"),
        name=name,
    )(dest2d, ys, wgt, x1, mod3, final_g.reshape(1, d))


def kernel(x_prompt, x_sample, state_hgrn_fwd, state_hgrn_bwd, state_ret_fwd, state_ret_bwd, c, c_ctx, ada_w, ada_b, norm1_g, norm2_g, final_norm_g, w_in, hg_lb_fwd, hg_lb_bwd, hg_norm_g, ret_log2_fwd, ret_log2_bwd, ret_norm_g, w_proj_hgrn, w_proj_ret, w_out, router_w, router_b, moe_w_gu, moe_b_gu, moe_w_dn, moe_b_dn):
    assert ada_w.shape[0] == 1, "one trunk layer"
    b_ctx, t_ctx, d = x_prompt.shape
    b_lat, t_lat, _ = x_sample.shape
    n_ctx_rows = b_ctx * t_ctx
    assert n_ctx_rows % t_lat == 0 and b_lat + 1 <= COND_ROWS
    n_exp = router_w.shape[2]

    x_ctx = x_prompt.reshape(n_ctx_rows, d)
    x_lat = x_sample.reshape(b_lat * t_lat, d)
    cond = jnp.zeros((COND_ROWS, d), F32).at[0].set(c_ctx).at[1:1 + b_lat].set(c)
    mod3 = _adaln(cond, ada_w[0], ada_b[0]).reshape(COND_ROWS * N_MOD, 1, d)

    proj = _inproj(x_ctx, x_lat, mod3, norm1_g[0], w_in[0], t_lat)

    hg_args = (proj, hg_lb_fwd, hg_lb_bwd, hg_norm_g[0])
    oa_ctx, new_hf, new_hb = _hgrn(*hg_args, None, n_seq=b_ctx, t=t_ctx, row_off=0, emit_state=True)
    (oa_lat,) = _hgrn(*hg_args, (state_hgrn_fwd, state_hgrn_bwd), n_seq=b_lat, t=t_lat,
                      row_off=n_ctx_rows // t_lat, emit_state=False)

    p_f = jnp.broadcast_to(ret_log2_fwd[0][:, None, None], (RET_HEADS, 1, RET_DV))
    p_b = jnp.broadcast_to(ret_log2_bwd[0][:, None, None], (RET_HEADS, 1, RET_DV))
    ret_args = (proj, p_f, p_b, ret_norm_g[0])
    ob_ctx, new_rf, new_rb = _ret(*ret_args, None, None, n_seq=b_ctx, t=t_ctx, row_off=0, emit_state=True)
    (ob_lat,) = _ret(*ret_args, _rope_tables(t_lat), (state_ret_fwd, state_ret_bwd), n_seq=b_lat, t=t_lat,
                     row_off=n_ctx_rows // t_lat, emit_state=False)

    merged = _merge(oa_ctx, oa_lat, ob_ctx, ob_lat, proj, w_proj_hgrn[0].astype(BF16), w_proj_ret[0].astype(BF16))

    rw_pad = jnp.zeros((d, LANES), F32).at[:, :n_exp].set(router_w[0])
    rb_pad = jnp.full((1, LANES), -1e30, F32).at[0, :n_exp].set(router_b[0])
    x1, xp, top_idx, top_w = _router(merged, x_ctx, x_lat, mod3, norm2_g[0], w_out[0].astype(BF16),
                                     rw_pad, rb_pad, t_lat)

    dest, order2d, plan, n_rows = _moe_plan(top_idx[:, :TOP_K], n_exp)
    ff = moe_w_dn.shape[2]
    b_gu_perm = moe_b_gu[0].reshape(n_exp, ff // LANES, LANES, 2).transpose(0, 1, 3, 2).reshape(n_exp, 2 * ff)
    src = jnp.arange(MXU_DIM)
    perm = jnp.zeros((MXU_DIM, MXU_DIM), BF16).at[src, (src % 2) * LANES + src // 2].set(1)
    xp = xp.reshape(xp.shape[0] // (d // 2 // LANES), d // 2 // LANES, LANES)
    ys = _moe(xp, order2d, plan, moe_w_gu[0], moe_w_dn[0], b_gu_perm, moe_b_dn[0], perm, n_rows)

    comb_args = (dest, ys, top_w, x1, mod3, final_norm_g, n_ctx_rows, t_lat)
    y_prompt = _combine(*comb_args, first_row=0, n_out_rows=n_ctx_rows, name="combine_ctx")
    y_sample = _combine(*comb_args, first_row=n_ctx_rows, n_out_rows=b_lat * t_lat, name="combine_lat")
    return (y_prompt.reshape(b_ctx, t_ctx, d), y_sample.reshape(b_lat, t_lat, d), new_hf, new_hb, new_rf, new_rb)
```

```python
import functools

import jax
import jax.numpy as jnp
from jax import lax
from jax.experimental import pallas as pl
from jax.experimental.pallas import tpu as pltpu

F32 = jnp.float32
BF16 = jnp.bfloat16

HG_HEADS = 8
HG_DK = 128
HG_DV = 128
RET_HEADS = 8
RET_DK = 128
RET_DV = 256
CHUNK = 32
GRID_W = 64
ROPE_BASE = 10000.0
TOP_K = 4
SWIGLU_LIMIT = 7.0
SWIGLU_ALPHA = 1.702
N_MOD = 6
EPS = 1e-6

LANES = 128
F32_SUBLANES = 8
MXU_DIM = 256
VMEM_LIMIT = 56 * 1024 * 1024
COND_ROWS = 16
WIDE_TILE = 1024
MID_TILE = 512
ROW_TILE = 256
SCAN_BLOCK = 256
SCAN_UNROLL = 8
HGRN_ROWS_PER_STEP = 4096
RET_HEADS_PER_STEP = 4
RET_ROWS_PER_STEP = 2048
SMEM_TILE = 1024
MOE_GROUP = 128
MOE_ITEM_ROWS = 2048
MOE_GROUPS_PER_TRIP = 8
MOE_FF_TILE = 256
ROW_UNROLL = 8


def _sigmoid(x):
    return 0.5 * jnp.tanh(0.5 * x) + 0.5


def _params(*sem):
    return pltpu.CompilerParams(dimension_semantics=sem, vmem_limit_bytes=VMEM_LIMIT)


def _tile(n, pref):
    if n <= pref:
        return n
    for t in range(pref - pref % LANES, 0, -LANES):
        if n % t == 0:
            return t
    raise ValueError((n, pref))


def _adaln_kernel(c_ref, w_ref, b_ref, o_ref):
    c = c_ref[...]
    s = (c * _sigmoid(c)).astype(BF16)
    o_ref[...] = jnp.dot(s, w_ref[...].astype(BF16), preferred_element_type=F32) + b_ref[...]


def _adaln(cond, ada_w, ada_b):
    d, n = ada_w.shape
    tn = _tile(n, WIDE_TILE)
    return pl.pallas_call(
        _adaln_kernel,
        grid=(n // tn,),
        in_specs=[pl.BlockSpec((COND_ROWS, d), lambda j: (0, 0)),
                  pl.BlockSpec((d, tn), lambda j: (0, j)),
                  pl.BlockSpec((1, tn), lambda j: (0, j))],
        out_specs=pl.BlockSpec((COND_ROWS, tn), lambda j: (0, j)),
        out_shape=jax.ShapeDtypeStruct((COND_ROWS, n), F32),
        compiler_params=_params("arbitrary"),
        name="adaln",
    )(cond, ada_w, ada_b.reshape(1, n))


def _prenorm_kernel(xc_ref, xl_ref, sh_ref, sc_ref, g_ref, o_ref, *, n_ctx_tiles):
    x = jnp.where(pl.program_id(0) < n_ctx_tiles, xc_ref[...], xl_ref[...])
    ms = jnp.mean(x * x, axis=-1, keepdims=True)
    xn = x * lax.rsqrt(ms + EPS) * g_ref[...]
    o_ref[...] = (xn * (1.0 + sc_ref[0]) + sh_ref[0]).astype(BF16)


def _inproj_kernel(xm_ref, w_ref, o_ref):
    o_ref[...] = jnp.dot(xm_ref[...], w_ref[...].astype(BF16), preferred_element_type=F32)


def _mod_row_fn(tm, n_ctx_rows, t_lat):
    n_ctx_tiles = n_ctx_rows // tm

    def row(i):
        return jnp.where(i < n_ctx_tiles, 0, 1 + ((i - n_ctx_tiles) * tm) // t_lat)

    return row


def _group_specs(tm, width, n_ctx_tiles):
    ctx = lambda i: (jnp.minimum(i, n_ctx_tiles - 1), 0)
    lat = lambda i: (jnp.maximum(i - n_ctx_tiles, 0), 0)
    return [pl.BlockSpec((tm, width), ctx), pl.BlockSpec((tm, width), lat)]


def _inproj(x_ctx, x_lat, mod3, norm_g, w_in, t_lat):
    n_ctx_rows, d = x_ctx.shape
    t_all = n_ctx_rows + x_lat.shape[0]
    n = w_in.shape[1]
    tp = _tile(min(n_ctx_rows, t_lat), MID_TILE)
    row = _mod_row_fn(tp, n_ctx_rows, t_lat)
    xm = pl.pallas_call(
        functools.partial(_prenorm_kernel, n_ctx_tiles=n_ctx_rows // tp),
        grid=(t_all // tp,),
        in_specs=_group_specs(tp, d, n_ctx_rows // tp) + [
            pl.BlockSpec((1, 1, d), lambda i: (row(i) * N_MOD + 0, 0, 0)),
            pl.BlockSpec((1, 1, d), lambda i: (row(i) * N_MOD + 1, 0, 0)),
            pl.BlockSpec((1, d), lambda i: (0, 0))],
        out_specs=pl.BlockSpec((tp, d), lambda i: (i, 0)),
        out_shape=jax.ShapeDtypeStruct((t_all, d), BF16),
        compiler_params=_params("arbitrary"),
        name="prenorm",
    )(x_ctx, x_lat, mod3, mod3, norm_g.reshape(1, d))
    tm = _tile(t_all, WIDE_TILE)
    tn = _tile(n, WIDE_TILE)
    return pl.pallas_call(
        _inproj_kernel,
        grid=(t_all // tm, n // tn),
        in_specs=[pl.BlockSpec((tm, d), lambda i, j: (i, 0)),
                  pl.BlockSpec((d, tn), lambda i, j: (0, j))],
        out_specs=pl.BlockSpec((tm, tn), lambda i, j: (i, j)),
        out_shape=jax.ShapeDtypeStruct((t_all, n), F32),
        compiler_params=_params("arbitrary", "arbitrary"),
        name="inproj",
    )(xm, w_in)


def _dot_split(m_bf, x):
    hi = x.astype(BF16)
    lo = (x - hi.astype(F32)).astype(BF16)
    return jnp.dot(m_bf, hi, preferred_element_type=F32) + jnp.dot(m_bf, lo, preferred_element_type=F32)


_NT = (((1,), (1,)), ((), ()))
_TN = (((0,), (0,)), ((), ()))


def _hgrn_kernel(*refs, t, hpg, has_state, emit_state):
    hq_ref, hzf_ref, hzb_ref, hi_ref, hgate_ref, lbf_ref, lbb_ref, gn_ref = refs[:8]
    pos = 8
    if has_state:
        s0f_ref, s0b_ref = refs[pos:pos + 2]
        pos += 2
    out_ref = refs[pos]
    pos += 1
    if emit_state:
        sf_ref, sb_ref = refs[pos:pos + 2]
        pos += 2
    (q_in_f, k_end_f, dec_f, q_in_b, k_end_b, dec_b, v_bf, o_f, o_b, st_ref) = refs[pos:]
    heads = [slice(j * LANES, (j + 1) * LANES) for j in range(hpg)]

    blk = min(SCAN_BLOCK, t)
    n_blk = t // blk
    n_chunk = t // CHUNK

    r = lax.broadcasted_iota(jnp.int32, (blk, blk), 0)
    c = lax.broadcasted_iota(jnp.int32, (blk, blk), 1)
    same = (r // CHUNK) == (c // CHUNK)
    low = same & (c <= r)
    upp = same & (c >= r)
    m_pre = jnp.where(low, 1.0, 0.0).astype(BF16)
    m_suf = jnp.where(upp, 1.0, 0.0).astype(BF16)

    def lower_bound(lb_ref):
        a = lb_ref[...]
        e = jnp.exp(a - jnp.max(a, axis=0, keepdims=True))
        return e[0:1] / jnp.sum(e, axis=0, keepdims=True)

    lb_f = lower_bound(lbf_ref)
    lb_b = lower_bound(lbb_ref)

    width = hpg * LANES

    def chunk_total(g, reverse):
        g3 = g.reshape(blk // CHUNK, CHUNK, width)
        edge = g3[:, 0:1, :] if reverse else g3[:, CHUNK - 1:CHUNK, :]
        return jnp.broadcast_to(edge, g3.shape).reshape(blk, width)

    directions = ((hzf_ref, lb_f, m_pre, low, False, q_in_f, k_end_f, dec_f, o_f),
                  (hzb_ref, lb_b, m_suf, upp, True, q_in_b, k_end_b, dec_b, o_b))

    def prep(i, carry):
        rows = pl.ds(pl.multiple_of(i * blk, blk), blk)
        hq = hq_ref[rows, :]
        q = hq * _sigmoid(hq)
        v = hi_ref[rows, :].astype(BF16)
        v_bf[rows, :] = v
        for z_ref, lb, m_cum, mask, reverse, qi, ke, dc, o_ref in directions:
            f = lb + (1.0 - lb) * _sigmoid(z_ref[rows, :])
            k = 1.0 - f
            g = _dot_split(m_cum, jnp.log(f))
            tot = chunk_total(g, reverse)
            q_in = (q * jnp.exp(g)).astype(BF16)
            k_in = (k * jnp.exp(-g)).astype(BF16)
            qi[rows, :] = q_in
            ke[rows, :] = (k * jnp.exp(tot - g)).astype(BF16)
            dc[rows, :] = jnp.exp(tot)
            for hd in heads:
                s = lax.dot_general(q_in[:, hd], k_in[:, hd], _NT, preferred_element_type=F32)
                o_ref[rows, hd] = jnp.dot(jnp.where(mask, s, 0.0).astype(BF16), v[:, hd],
                                          preferred_element_type=F32)
        return carry

    lax.fori_loop(0, n_blk, prep, 0)

    for j in range(hpg):
        for slot, s0_ref in ((2 * j, s0f_ref if has_state else None), (2 * j + 1, s0b_ref if has_state else None)):
            st_ref[slot] = s0_ref[j].T if has_state else jnp.zeros((HG_DV, HG_DK), F32)

    def chunk_step(idx, slot, hd, qi, ke, dc, o_ref):
        start = pl.multiple_of(idx * CHUNK, CHUNK)
        rows = pl.ds(start, CHUNK)
        st = st_ref[slot]
        o_ref[rows, hd] += lax.dot_general(qi[rows, hd], st.astype(BF16), _NT, preferred_element_type=F32)
        upd = lax.dot_general(v_bf[rows, hd], ke[rows, hd], _TN, preferred_element_type=F32)
        st_ref[slot] = dc[pl.ds(start, 1), hd] * st + upd

    def body(i, carry):
        for j, hd in enumerate(heads):
            chunk_step(i, 2 * j, hd, q_in_f, k_end_f, dec_f, o_f)
            chunk_step(n_chunk - 1 - i, 2 * j + 1, hd, q_in_b, k_end_b, dec_b, o_b)
        return carry

    lax.fori_loop(0, n_chunk, body, 0, unroll=min(n_chunk, SCAN_UNROLL))
    if emit_state:
        for j in range(hpg):
            sf_ref[j] = st_ref[2 * j].T
            sb_ref[j] = st_ref[2 * j + 1].T

    for hd in heads:
        o = o_f[:, hd] + o_b[:, hd]
        o = o * lax.rsqrt(jnp.mean(o * o, axis=-1, keepdims=True) + EPS) * gn_ref[...]
        out_ref[:, hd] = (o * _sigmoid(hgate_ref[:, hd])).astype(BF16)


def _hgrn(proj, lb_fwd, lb_bwd, norm_g, states, *, n_seq, t, row_off, emit_state):
    nh = HG_HEADS
    hpg = max(1, min(nh, HGRN_ROWS_PER_STEP // t))
    assert nh % hpg == 0
    ng = nh // hpg
    w = hpg * LANES
    col = lambda k: (lambda b, h: (row_off + b, k * ng + h))
    blk = lambda k: pl.BlockSpec((t, w), col(k))
    in_specs = [blk(0), blk(1), blk(2), blk(3), blk(4),
                pl.BlockSpec((2, w), lambda b, h: (0, h)),
                pl.BlockSpec((2, w), lambda b, h: (0, h)),
                pl.BlockSpec((1, LANES), lambda b, h: (0, 0))]
    args = [proj] * 5 + [lb_fwd, lb_bwd, norm_g.reshape(1, HG_DV)]
    st_spec = pl.BlockSpec((None, None, hpg, HG_DK, HG_DV), lambda b, h: (b, 0, h, 0, 0))
    if states is not None:
        in_specs += [st_spec, st_spec]
        args += list(states)
    out_specs = [pl.BlockSpec((t, w), lambda b, h: (b, h))]
    out_shape = [jax.ShapeDtypeStruct((n_seq * t, nh * HG_DV), BF16)]
    if emit_state:
        out_specs += [st_spec, st_spec]
        out_shape += [jax.ShapeDtypeStruct((n_seq, 1, nh, HG_DK, HG_DV), F32)] * 2
    scratch = ([pltpu.VMEM((t, w), BF16)] * 2 + [pltpu.VMEM((t, w), F32)]) * 2
    scratch += [pltpu.VMEM((t, w), BF16), pltpu.VMEM((t, w), F32), pltpu.VMEM((t, w), F32),
                pltpu.VMEM((2 * hpg, HG_DV, HG_DK), F32)]
    return pl.pallas_call(
        functools.partial(_hgrn_kernel, t=t, hpg=hpg, has_state=states is not None, emit_state=emit_state),
        grid=(n_seq, ng),
        in_specs=in_specs,
        out_specs=out_specs,
        out_shape=out_shape,
        scratch_shapes=scratch,
        compiler_params=_params("arbitrary", "arbitrary"),
        name="hgrn_lat" if states is not None else "hgrn_ctx",
    )(*args)


def _ret_kernel(*refs, t, hpg, has_state, emit_state, use_rope):
    rq_ref, rk_ref, rv_ref, rgate_ref, pf_ref, pb_ref, gn_ref = refs[:7]
    pos = 7
    if use_rope:
        cos_ref, sin_ref = refs[pos:pos + 2]
        pos += 2
    if has_state:
        s0f_ref, s0b_ref = refs[pos:pos + 2]
        pos += 2
    out_ref = refs[pos]
    pos += 1
    if emit_state:
        sf_ref, sb_ref = refs[pos:pos + 2]
        pos += 2
    q_s, k_s, v_s, o_f, o_b, st_ref = refs[pos:]

    ch = min(MXU_DIM, t)
    n_chunk = t // ch

    lane = lax.broadcasted_iota(jnp.int32, (t, RET_DK), 1)
    first_half = (lane % 64) < 32

    def rope(x):
        if not use_rope:
            return x
        swapped = jnp.where(first_half, pltpu.roll(x, RET_DK - 32, 1), pltpu.roll(x, 32, 1))
        return x * cos_ref[...] + swapped * sin_ref[...]

    ri = lax.broadcasted_iota(jnp.int32, (ch, ch), 0)
    ci = lax.broadcasted_iota(jnp.int32, (ch, ch), 1)
    row128 = lax.broadcasted_iota(jnp.int32, (ch, RET_DK), 0).astype(F32)

    def direction(j, p_ref, slot, o_scr, s0_ref, reverse):
        qk = slice(j * RET_DK, (j + 1) * RET_DK)
        vv = slice(j * RET_DV, (j + 1) * RET_DV)
        lg = jnp.log1p(-jnp.exp2(p_ref[j]))
        lg_k = lg[:, :RET_DK]
        dist = (ci - ri) if reverse else (ri - ci)
        keep = dist >= 0
        dmask = jnp.where(keep, jnp.exp(lg[:, :ch] * jnp.where(keep, dist, 0).astype(F32)), 0.0)
        if reverse:
            q_scale = jnp.exp(lg_k * (ch - row128))
            k_scale = jnp.exp(lg_k * row128)
        else:
            q_scale = jnp.exp(lg_k * (row128 + 1.0))
            k_scale = jnp.exp(lg_k * (ch - 1.0 - row128))
        dec = jnp.exp(lg * float(ch))
        st_ref[slot] = s0_ref[j] if has_state else jnp.zeros((RET_DK, RET_DV), F32)
        for step in range(n_chunk):
            cidx = n_chunk - 1 - step if reverse else step
            rows = pl.ds(cidx * ch, ch)
            q = q_s[rows, qk]
            k = k_s[rows, qk]
            v = v_s[rows, vv]
            s = lax.dot_general(q.astype(BF16), k.astype(BF16), _NT, preferred_element_type=F32)
            o = jnp.dot((s * dmask).astype(BF16), v, preferred_element_type=F32)
            st = st_ref[slot]
            o = o + jnp.dot((q * q_scale).astype(BF16), st.astype(BF16), preferred_element_type=F32)
            o_scr[rows, vv] = o
            upd = lax.dot_general((k * k_scale).astype(BF16), v, _TN, preferred_element_type=F32)
            st_ref[slot] = dec * st + upd

    for j in range(hpg):
        qk = slice(j * RET_DK, (j + 1) * RET_DK)
        q_s[:, qk] = rope(rq_ref[:, qk])
        k_s[:, qk] = rope(rk_ref[:, qk] * (RET_DK ** -0.5))
    v_s[...] = rv_ref[...].astype(BF16)
    for j in range(hpg):
        direction(j, pf_ref, 2 * j, o_f, s0f_ref if has_state else None, False)
        direction(j, pb_ref, 2 * j + 1, o_b, s0b_ref if has_state else None, True)
    if emit_state:
        for j in range(hpg):
            sf_ref[j] = st_ref[2 * j]
            sb_ref[j] = st_ref[2 * j + 1]

    for j in range(hpg):
        vv = slice(j * RET_DV, (j + 1) * RET_DV)
        o = o_f[:, vv] + o_b[:, vv]
        oc = o - jnp.mean(o, axis=-1, keepdims=True)
        o = oc * lax.rsqrt(jnp.mean(oc * oc, axis=-1, keepdims=True) + EPS) * gn_ref[...]
        gate = rgate_ref[:, vv]
        out_ref[:, vv] = (o * (gate * _sigmoid(gate))).astype(BF16)


def _ret(proj, p_fwd, p_bwd, norm_g, rope, states, *, n_seq, t, row_off, emit_state):
    nh = RET_HEADS
    base = (3 * HG_HEADS * HG_DK + 2 * HG_HEADS * HG_DV)
    hpg = max(1, min(nh, RET_HEADS_PER_STEP, RET_ROWS_PER_STEP // t))
    while nh % hpg or base % (hpg * RET_DK) or (base + 2 * nh * RET_DK) % (hpg * RET_DV):
        hpg -= 1
    ng = nh // hpg
    wk, wv = hpg * RET_DK, hpg * RET_DV
    qk0 = base // wk
    v0 = (base + 2 * nh * RET_DK) // wv
    in_specs = [pl.BlockSpec((t, wk), lambda b, h: (row_off + b, qk0 + h)),
                pl.BlockSpec((t, wk), lambda b, h: (row_off + b, qk0 + ng + h)),
                pl.BlockSpec((t, wv), lambda b, h: (row_off + b, v0 + h)),
                pl.BlockSpec((t, wv), lambda b, h: (row_off + b, v0 + ng + h)),
                pl.BlockSpec((hpg, 1, RET_DV), lambda b, h: (h, 0, 0)),
                pl.BlockSpec((hpg, 1, RET_DV), lambda b, h: (h, 0, 0)),
                pl.BlockSpec((1, RET_DV), lambda b, h: (0, 0))]
    args = [proj] * 4 + [p_fwd, p_bwd, norm_g.reshape(1, RET_DV)]
    if rope is not None:
        in_specs += [pl.BlockSpec((t, RET_DK), lambda b, h: (0, 0))] * 2
        args += list(rope)
    st_spec = pl.BlockSpec((None, None, hpg, RET_DK, RET_DV), lambda b, h: (b, 0, h, 0, 0))
    if states is not None:
        in_specs += [st_spec, st_spec]
        args += list(states)
    out_specs = [pl.BlockSpec((t, wv), lambda b, h: (b, h))]
    out_shape = [jax.ShapeDtypeStruct((n_seq * t, nh * RET_DV), BF16)]
    if emit_state:
        out_specs += [st_spec, st_spec]
        out_shape += [jax.ShapeDtypeStruct((n_seq, 1, nh, RET_DK, RET_DV), F32)] * 2
    scratch = [pltpu.VMEM((t, wk), F32), pltpu.VMEM((t, wk), F32), pltpu.VMEM((t, wv), BF16),
               pltpu.VMEM((t, wv), F32), pltpu.VMEM((t, wv), F32),
               pltpu.VMEM((2 * hpg, RET_DK, RET_DV), F32)]
    return pl.pallas_call(
        functools.partial(_ret_kernel, t=t, hpg=hpg, has_state=states is not None, emit_state=emit_state,
                          use_rope=rope is not None),
        grid=(n_seq, ng),
        in_specs=in_specs,
        out_specs=out_specs,
        out_shape=out_shape,
        scratch_shapes=scratch,
        compiler_params=_params("arbitrary", "arbitrary"),
        name="ret_lat" if states is not None else "ret_ctx",
    )(*args)


def _rope_tables(t):
    pairs = RET_DK // 4
    pos = jnp.arange(t, dtype=jnp.int32)
    rows = (pos // GRID_W).astype(F32)
    cols = (pos % GRID_W).astype(F32)
    inv = ROPE_BASE ** (-jnp.arange(pairs, dtype=F32) / pairs)
    ar = rows[:, None] * inv
    ac = cols[:, None] * inv
    cos = jnp.concatenate([jnp.cos(ar), jnp.cos(ar), jnp.cos(ac), jnp.cos(ac)], axis=1)
    sin = jnp.concatenate([-jnp.sin(ar), jnp.sin(ar), -jnp.sin(ac), jnp.sin(ac)], axis=1)
    return cos, sin


def _merge_kernel(oac_ref, oal_ref, obc_ref, obl_ref, ga_ref, gb_ref, wa_ref, wb_ref, o_ref, *, n_ctx_tiles):
    is_ctx = pl.program_id(1) < n_ctx_tiles
    oa = jnp.where(is_ctx, oac_ref[...], oal_ref[...])
    ob = jnp.where(is_ctx, obc_ref[...], obl_ref[...])
    a = jnp.dot(oa, wa_ref[...], preferred_element_type=F32)
    b = jnp.dot(ob, wb_ref[...], preferred_element_type=F32)
    o_ref[...] = (_sigmoid(ga_ref[...]) * a + _sigmoid(gb_ref[...]) * b).astype(BF16)


def _merge(oa_ctx, oa_lat, ob_ctx, ob_lat, proj, wa_bf, wb_bf):
    n_ctx_rows, ka = oa_ctx.shape
    t_all = n_ctx_rows + oa_lat.shape[0]
    kb = ob_ctx.shape[1]
    d = wa_bf.shape[1]
    tm = _tile(min(n_ctx_rows, oa_lat.shape[0]), MID_TILE)
    tn = _tile(d, WIDE_TILE)
    nct = n_ctx_rows // tm
    gate0 = (proj.shape[1] - 2 * d) // tn
    ctx = lambda j, i: (jnp.minimum(i, nct - 1), 0)
    lat = lambda j, i: (jnp.maximum(i - nct, 0), 0)
    return pl.pallas_call(
        functools.partial(_merge_kernel, n_ctx_tiles=nct),
        grid=(d // tn, t_all // tm),
        in_specs=[pl.BlockSpec((tm, ka), ctx), pl.BlockSpec((tm, ka), lat),
                  pl.BlockSpec((tm, kb), ctx), pl.BlockSpec((tm, kb), lat),
                  pl.BlockSpec((tm, tn), lambda j, i: (i, gate0 + j)),
                  pl.BlockSpec((tm, tn), lambda j, i: (i, gate0 + d // tn + j)),
                  pl.BlockSpec((ka, tn), lambda j, i: (0, j)),
                  pl.BlockSpec((kb, tn), lambda j, i: (0, j))],
        out_specs=pl.BlockSpec((tm, tn), lambda j, i: (i, j)),
        out_shape=jax.ShapeDtypeStruct((t_all, d), BF16),
        compiler_params=_params("arbitrary", "arbitrary"),
        name="merge",
    )(oa_ctx, oa_lat, ob_ctx, ob_lat, proj, proj, wa_bf, wb_bf)


def _pack_bf16_pairs(x):
    half = x.shape[1] // 2
    lo = lax.bitcast_convert_type(x[:, :half].astype(BF16).astype(F32), jnp.uint32)
    hi = lax.bitcast_convert_type(x[:, half:].astype(BF16).astype(F32), jnp.uint32)
    return (lo >> 16) | (hi & jnp.uint32(0xFFFF0000))


def _unpack_bf16_pairs(w):
    lo = lax.bitcast_convert_type(w << 16, F32).astype(BF16)
    hi = lax.bitcast_convert_type(w & jnp.uint32(0xFFFF0000), F32).astype(BF16)
    return lo, hi


def _router_kernel(m_ref, xc_ref, xl_ref, g1_ref, sh_ref, sc_ref, ng_ref, wo_ref, rw2_ref, rwh_ref, rb_ref,
                   x1_ref, xp_ref, idx_ref, wgt_ref, *, n_ctx_tiles):
    out = jnp.dot(m_ref[...], wo_ref[...], preferred_element_type=F32)
    x = jnp.where(pl.program_id(0) < n_ctx_tiles, xc_ref[...], xl_ref[...])
    x1 = x + g1_ref[0] * out
    x1_ref[...] = x1
    ms = jnp.mean(x1 * x1, axis=-1, keepdims=True)
    xm = x1 * lax.rsqrt(ms + EPS) * ng_ref[...]
    xm = xm * (1.0 + sc_ref[0]) + sh_ref[0]
    packed = _pack_bf16_pairs(xm)
    n_slab = packed.shape[1] // LANES
    for s in range(n_slab):
        xp_ref[pl.ds(s, packed.shape[0], stride=n_slab), :] = packed[:, s * LANES:(s + 1) * LANES]

    x_hi = xm.astype(BF16)
    x_lo = (xm - x_hi.astype(F32)).astype(BF16)
    hh_hl = jnp.dot(x_hi, rw2_ref[...], preferred_element_type=F32)
    lh = jnp.dot(x_lo, rwh_ref[...], preferred_element_type=F32)
    logits = hh_hl[:, :LANES] + (hh_hl[:, LANES:] + lh) + rb_ref[...]
    lane = lax.broadcasted_iota(jnp.int32, logits.shape, 1)
    lane_f = lane.astype(F32)
    idx_out = jnp.zeros(logits.shape, F32)
    val_out = jnp.zeros(logits.shape, F32)
    top = None
    for k in range(TOP_K):
        m = jnp.max(logits, axis=-1, keepdims=True)
        idx = jnp.min(jnp.where(logits == m, lane_f, float(LANES)), axis=-1, keepdims=True)
        if top is None:
            top = m
        idx_out = jnp.where(lane == k, idx, idx_out)
        val_out = jnp.where(lane == k, jnp.exp(m - top), val_out)
        logits = jnp.where(lane_f == idx, -jnp.inf, logits)
    idx_ref[...] = idx_out.astype(jnp.int32)
    wgt_ref[...] = val_out / jnp.sum(val_out, axis=-1, keepdims=True)


def _router(merged, x_ctx, x_lat, mod3, norm_g, wo_bf, rw_pad, rb_pad, t_lat):
    n_ctx_rows, d = x_ctx.shape
    t_all = n_ctx_rows + x_lat.shape[0]
    tm = _tile(min(n_ctx_rows, t_lat), ROW_TILE)
    row = _mod_row_fn(tm, n_ctx_rows, t_lat)
    mod_spec = lambda k: pl.BlockSpec((1, 1, d), lambda i: (row(i) * N_MOD + k, 0, 0))
    rw_hi = rw_pad.astype(BF16)
    rw_lo = (rw_pad - rw_hi.astype(F32)).astype(BF16)
    rw2 = jnp.concatenate([rw_hi, rw_lo], axis=1)
    return pl.pallas_call(
        functools.partial(_router_kernel, n_ctx_tiles=n_ctx_rows // tm),
        grid=(t_all // tm,),
        in_specs=[pl.BlockSpec((tm, d), lambda i: (i, 0))] + _group_specs(tm, d, n_ctx_rows // tm) + [
                  mod_spec(2), mod_spec(3), mod_spec(4),
                  pl.BlockSpec((1, d), lambda i: (0, 0)),
                  pl.BlockSpec((d, d), lambda i: (0, 0)),
                  pl.BlockSpec((d, 2 * LANES), lambda i: (0, 0)),
                  pl.BlockSpec((d, LANES), lambda i: (0, 0)),
                  pl.BlockSpec((1, LANES), lambda i: (0, 0))],
        out_specs=[pl.BlockSpec((tm, d), lambda i: (i, 0)),
                   pl.BlockSpec((tm * (d // 2 // LANES), LANES), lambda i: (i, 0)),
                   pl.BlockSpec((tm, LANES), lambda i: (i, 0)),
                   pl.BlockSpec((tm, LANES), lambda i: (i, 0))],
        out_shape=[jax.ShapeDtypeStruct((t_all, d), F32),
                   jax.ShapeDtypeStruct((t_all * (d // 2 // LANES), LANES), jnp.uint32),
                   jax.ShapeDtypeStruct((t_all, LANES), jnp.int32),
                   jax.ShapeDtypeStruct((t_all, LANES), F32)],
        compiler_params=_params("arbitrary"),
        name="router",
    )(merged, x_ctx, x_lat, mod3, mod3, mod3, norm_g.reshape(1, d), wo_bf, rw2, rw_hi, rb_pad)


def _moe_kernel(e_ref, start_ref, nsub_ref, src_ref, tail_ref,
                xp_hbm, order_hbm, wgu_ref, wdn_ref, bgu_ref, bdn_ref, perm_ref,
                ys_hbm,
                tok_smem, x_raw, x_lo, x_hi, y_acc, wgu_bf, wdn_bf, tok_sem, row_sem, out_sem,
                *, n_ff_tiles):
    i = pl.program_id(0)
    f = pl.program_id(1)
    nsub = nsub_ref[i]
    start = start_ref[i]
    n_slab = xp_hbm.shape[1]
    g = MOE_GROUP
    d = y_acc.shape[1]
    half = d // 2
    ff2 = wgu_bf.shape[1]

    def order_copy(it):
        first = pl.multiple_of(src_ref[it] & ~(SMEM_TILE - 1), SMEM_TILE)
        return pltpu.make_async_copy(order_hbm.at[pl.ds(first, tok_smem.shape[0])], tok_smem, tok_sem)

    def row_copy(off, base, u):
        tok = lax.shift_right_logical(tok_smem[off + base + u], 2)
        return pltpu.make_async_copy(xp_hbm.at[tok], x_raw.at[pl.ds((base + u) * n_slab, n_slab), :], row_sem)

    def gather_start(it):
        off = src_ref[it] & (SMEM_TILE - 1)

        def issue(r, c):
            for u in range(ROW_UNROLL):
                row_copy(off, r * ROW_UNROLL, u).start()
            return c

        lax.fori_loop(0, nsub_ref[it] * (g // ROW_UNROLL), issue, 0)

    def gather_wait(it):
        off = src_ref[it] & (SMEM_TILE - 1)

        def drain(r, c):
            for u in range(ROW_UNROLL):
                row_copy(off, r * ROW_UNROLL, u).wait()
            return c

        lax.fori_loop(0, nsub_ref[it] * (g // ROW_UNROLL), drain, 0)

    nxt = jnp.minimum(i + 1, pl.num_programs(0) - 1)
    has_next = (i + 1 < pl.num_programs(0)) & (nsub_ref[nxt] > 0)

    @pl.when((i == 0) & (f == 0) & (nsub > 0))
    def _():
        order_copy(0).start()
        order_copy(0).wait()
        gather_start(0)

    @pl.when((f == 0) & (nsub > 0))
    def _():
        def init(sb, c):
            rows = pl.ds(pl.multiple_of(sb * g, g), g)
            y_acc[rows, :] = jnp.broadcast_to(bdn_ref[...], (g, d))
            return c

        lax.fori_loop(0, nsub, init, 0)
        gather_wait(i)

        def unpack(sb, c):
            rows = pl.ds(pl.multiple_of(sb * g, g), g)
            for s in range(n_slab):
                lo, hi = _unpack_bf16_pairs(x_raw[pl.ds(sb * (g * n_slab) + s, g, stride=n_slab), :])
                x_lo[rows, s * LANES:(s + 1) * LANES] = lo
                x_hi[rows, s * LANES:(s + 1) * LANES] = hi
            return c

        lax.fori_loop(0, nsub, unpack, 0)

        @pl.when(has_next)
        def _():
            order_copy(nxt).start()

    @pl.when((f == 1) & (nsub > 0) & has_next)
    def _():
        order_copy(nxt).wait()
        gather_start(nxt)

    @pl.when(nsub > 0)
    def _():
        for j in range(ff2 // MXU_DIM):
            cols = slice(j * MXU_DIM, (j + 1) * MXU_DIM)
            wgu_bf[:, cols] = jnp.dot(wgu_ref[:, cols].astype(BF16), perm_ref[...],
                                      preferred_element_type=F32).astype(BF16)
        wdn_bf[...] = wdn_ref[...].astype(BF16)
        bgu = bgu_ref[...]

        def expert_rows(first_row, n_rows):
            rows = pl.ds(pl.multiple_of(first_row, g), n_rows)
            h = (jnp.dot(x_lo[rows, :], wgu_bf[:half, :], preferred_element_type=F32)
                 + jnp.dot(x_hi[rows, :], wgu_bf[half:, :], preferred_element_type=F32) + bgu)
            acts = []
            for j in range(ff2 // MXU_DIM):
                gate = jnp.minimum(h[:, j * MXU_DIM: j * MXU_DIM + LANES], SWIGLU_LIMIT)
                up = jnp.clip(h[:, j * MXU_DIM + LANES: (j + 1) * MXU_DIM], -SWIGLU_LIMIT, SWIGLU_LIMIT)
                acts.append(((up + 1.0) * gate * _sigmoid(SWIGLU_ALPHA * gate)).astype(BF16))
            act = jnp.concatenate(acts, axis=1)
            y_acc[rows, :] += jnp.dot(act, wdn_bf[...], preferred_element_type=F32)

        big = MOE_GROUPS_PER_TRIP

        def trip(q, c):
            expert_rows(q * (big * g), big * g)
            return c

        lax.fori_loop(0, lax.shift_right_logical(nsub, big.bit_length() - 1), trip, 0)
        rem = nsub & (big - 1)
        part = big // 2
        while part >= 1:
            @pl.when((rem & part) != 0)
            def _(part=part):
                expert_rows((nsub - rem + (rem & ~(2 * part - 1))) * g, part * g)
            part //= 2

    @pl.when((f == n_ff_tiles - 1) & (nsub > 0))
    def _():
        def out_copy(sb):
            rows = pl.ds(pl.multiple_of(sb * g, g), g)
            dst = pl.ds(pl.multiple_of(start * LANES + sb * g, g), g)
            return pltpu.make_async_copy(y_acc.at[rows, :], ys_hbm.at[dst, :], out_sem)

        def issue(sb, c):
            out_copy(sb).start()
            return c

        def drain(sb, c):
            out_copy(sb).wait()
            return c

        lax.fori_loop(0, nsub, issue, 0)
        lax.fori_loop(0, nsub, drain, 0)

    @pl.when((i == pl.num_programs(0) - 1) & (f == n_ff_tiles - 1))
    def _():
        y_acc[0:g, :] = jnp.zeros((g, d), F32)
        first = tail_ref[0]
        n_tail = ys_hbm.shape[0] // g - first

        def tail_copy(k):
            dst = pl.ds(pl.multiple_of((first + k) * g, g), g)
            return pltpu.make_async_copy(y_acc.at[0:g, :], ys_hbm.at[dst, :], out_sem)

        def issue(k, c):
            tail_copy(k).start()
            return c

        def drain(k, c):
            tail_copy(k).wait()
            return c

        lax.fori_loop(0, n_tail, issue, 0)
        lax.fori_loop(0, n_tail, drain, 0)


def _moe(xp, order2d, plan, w_gu, w_dn, b_gu_perm, b_dn, perm, n_rows):
    n_exp, d, ff2_all = w_gu.shape
    ff = ff2_all // 2
    tf = _tile(ff, MOE_FF_TILE)
    n_f = ff // tf
    assert n_f >= 2, "the next item's row gather is started in the second feature step"
    n_items = plan[0].shape[0]

    def f_eff(i, f, nsub):
        return jnp.where(nsub[i] > 0, f, n_f - 1)

    grid_spec = pltpu.PrefetchScalarGridSpec(
        num_scalar_prefetch=len(plan),
        grid=(n_items, n_f),
        in_specs=[pl.BlockSpec(memory_space=pl.ANY),
                  pl.BlockSpec(memory_space=pl.ANY),
                  pl.BlockSpec((None, d, 2 * tf), lambda i, f, e, s, n, *_: (e[i], 0, f_eff(i, f, n))),
                  pl.BlockSpec((None, tf, d), lambda i, f, e, s, n, *_: (e[i], f_eff(i, f, n), 0)),
                  pl.BlockSpec((None, 1, 2 * tf), lambda i, f, e, s, n, *_: (e[i], 0, f_eff(i, f, n))),
                  pl.BlockSpec((None, 1, d), lambda i, f, e, s, n, *_: (e[i], 0, 0)),
                  pl.BlockSpec((MXU_DIM, MXU_DIM), lambda i, f, e, s, n, *_: (0, 0))],
        out_specs=pl.BlockSpec(memory_space=pl.ANY),
        scratch_shapes=[pltpu.SMEM((MOE_ITEM_ROWS + SMEM_TILE,), jnp.int32),
                        pltpu.VMEM((MOE_ITEM_ROWS * xp.shape[1], LANES), jnp.uint32),
                        pltpu.VMEM((MOE_ITEM_ROWS, d // 2), BF16),
                        pltpu.VMEM((MOE_ITEM_ROWS, d // 2), BF16),
                        pltpu.VMEM((MOE_ITEM_ROWS, d), F32),
                        pltpu.VMEM((d, 2 * tf), BF16),
                        pltpu.VMEM((tf, d), BF16),
                        pltpu.SemaphoreType.DMA(()),
                        pltpu.SemaphoreType.DMA(()),
                        pltpu.SemaphoreType.DMA(())],
    )
    return pl.pallas_call(
        functools.partial(_moe_kernel, n_ff_tiles=n_f),
        grid_spec=grid_spec,
        out_shape=jax.ShapeDtypeStruct((n_rows, d), F32),
        compiler_params=_params("arbitrary", "arbitrary"),
        name="moe",
    )(*plan, xp, order2d, w_gu, w_dn,
      b_gu_perm.reshape(n_exp, 1, ff2_all), b_dn.reshape(n_exp, 1, d), perm)


def _moe_plan(top_idx, n_exp):
    t_all = top_idx.shape[0]
    n_assign = t_all * TOP_K
    g = MOE_GROUP
    rows_max = n_assign + n_exp * g
    n_items = n_exp + rows_max // MOE_ITEM_ROWS
    n_rows = rows_max + MOE_ITEM_ROWS

    flat_e = top_idx.reshape(-1).astype(jnp.int32)
    assign = jnp.arange(n_assign, dtype=jnp.int32)
    e_sorted, order = lax.sort_key_val(flat_e, assign)
    experts = jnp.arange(n_exp, dtype=jnp.int32)
    first = jnp.sum(e_sorted[:, None] < experts[None, :], axis=0, dtype=jnp.int32)
    counts = jnp.concatenate([first[1:], jnp.full((1,), n_assign, jnp.int32)]) - first
    padded = (counts + g - 1) // g * g
    pad_end = jnp.cumsum(padded)
    pad_start = pad_end - padded
    dest_sorted = pad_start[e_sorted] + assign - first[e_sorted]
    _, dest = lax.sort_key_val(order, dest_sorted)
    assert n_assign % SMEM_TILE == 0
    order2d = jnp.concatenate([order, jnp.zeros((MOE_ITEM_ROWS + SMEM_TILE,), jnp.int32)])

    chunks = (padded + MOE_ITEM_ROWS - 1) // MOE_ITEM_ROWS
    chunk_end = jnp.cumsum(chunks)
    items = jnp.arange(n_items, dtype=jnp.int32)
    n_used = chunk_end[-1]
    last_e = jnp.max(jnp.where(chunks > 0, jnp.arange(n_exp, dtype=jnp.int32), 0))
    e_of = jnp.minimum(jnp.searchsorted(chunk_end, items, side="right"), n_exp - 1).astype(jnp.int32)
    local = items - (chunk_end[e_of] - chunks[e_of])
    used = items < n_used
    item_e = jnp.where(used, e_of, last_e).astype(jnp.int32)
    left = padded[e_of] - local * MOE_ITEM_ROWS
    item_nsub = jnp.where(used, jnp.clip(left, 0, MOE_ITEM_ROWS) // g, 0).astype(jnp.int32)
    item_start = jnp.where(used, (pad_start[e_of] + local * MOE_ITEM_ROWS) // LANES, 0).astype(jnp.int32)
    item_src = jnp.where(used, first[e_of] + local * MOE_ITEM_ROWS, 0).astype(jnp.int32)
    tail = (jnp.sum(padded) // g).astype(jnp.int32).reshape(1)
    return dest, order2d, (item_e, item_start, item_nsub, item_src, tail), n_rows


def _combine_kernel(dest_ref, ys_hbm, wgt_ref, x1_ref, g2_ref, fg_ref, o_ref, gbuf, sem, *, tm):
    d_model = gbuf.shape[-1]

    def row_copy(trip, j, k):
        d = dest_ref[trip * (F32_SUBLANES * TOP_K) + j * TOP_K + k]
        return pltpu.make_async_copy(ys_hbm.at[pl.ds(d, 1), :], gbuf.at[k, trip, pl.ds(j, 1), :], sem)

    def issue(trip, c):
        for j in range(F32_SUBLANES):
            for k in range(TOP_K):
                row_copy(trip, j, k).start()
        return c

    def drain(trip, c):
        for j in range(F32_SUBLANES):
            for k in range(TOP_K):
                row_copy(trip, j, k).wait()
        return c

    lax.fori_loop(0, tm // F32_SUBLANES, issue, 0)
    lax.fori_loop(0, tm // F32_SUBLANES, drain, 0)

    w = wgt_ref[...]
    y = w[:, 0:1] * gbuf[0].reshape(tm, d_model)
    for k in range(1, TOP_K):
        y = y + w[:, k:k + 1] * gbuf[k].reshape(tm, d_model)
    x = x1_ref[...] + g2_ref[0] * y
    ms = jnp.mean(x * x, axis=-1, keepdims=True)
    o_ref[...] = x * lax.rsqrt(ms + EPS) * fg_ref[...]


def _combine(dest2d, ys, wgt, x1, mod3, final_g, n_ctx_rows, t_lat, *, first_row, n_out_rows, name):
    d = x1.shape[1]
    tm = _tile(min(n_ctx_rows, t_lat), ROW_TILE)
    row = _mod_row_fn(tm, n_ctx_rows, t_lat)
    assert (tm * TOP_K) % SMEM_TILE == 0
    t0 = first_row // tm
    return pl.pallas_call(
        functools.partial(_combine_kernel, tm=tm),
        grid=(n_out_rows // tm,),
        in_specs=[pl.BlockSpec((tm * TOP_K,), lambda i: (t0 + i,), memory_space=pltpu.SMEM),
                  pl.BlockSpec(memory_space=pl.ANY),
                  pl.BlockSpec((tm, LANES), lambda i: (t0 + i, 0)),
                  pl.BlockSpec((tm, d), lambda i: (t0 + i, 0)),
                  pl.BlockSpec((1, 1, d), lambda i: (row(t0 + i) * N_MOD + 5, 0, 0)),
                  pl.BlockSpec((1, d), lambda i: (0, 0))],
        out_specs=pl.BlockSpec((tm, d), lambda i: (i, 0)),
        out_shape=jax.ShapeDtypeStruct((n_out_rows, d), F32),
        scratch_shapes=[pltpu.VMEM((TOP_K, tm // F32_SUBLANES, F32_SUBLANES, d), F32),
                        pltpu.SemaphoreType.DMA(())],
        compiler_params=_params("arbitrary"),
        name=name,
    )(dest2d, ys, wgt, x1, mod3, final_g.reshape(1, d))


def kernel(x_prompt, x_sample, state_hgrn_fwd, state_hgrn_bwd, state_ret_fwd, state_ret_bwd, c, c_ctx, ada_w, ada_b, norm1_g, norm2_g, final_norm_g, w_in, hg_lb_fwd, hg_lb_bwd, hg_norm_g, ret_log2_fwd, ret_log2_bwd, ret_norm_g, w_proj_hgrn, w_proj_ret, w_out, router_w, router_b, moe_w_gu, moe_b_gu, moe_w_dn, moe_b_dn):
    assert ada_w.shape[0] == 1, "one trunk layer"
    b_ctx, t_ctx, d = x_prompt.shape
    b_lat, t_lat, _ = x_sample.shape
    n_ctx_rows = b_ctx * t_ctx
    assert n_ctx_rows % t_lat == 0 and b_lat + 1 <= COND_ROWS
    n_exp = router_w.shape[2]

    x_ctx = x_prompt.reshape(n_ctx_rows, d)
    x_lat = x_sample.reshape(b_lat * t_lat, d)
    cond = jnp.zeros((COND_ROWS, d), F32).at[0].set(c_ctx).at[1:1 + b_lat].set(c)
    mod3 = _adaln(cond, ada_w[0], ada_b[0]).reshape(COND_ROWS * N_MOD, 1, d)

    proj = _inproj(x_ctx, x_lat, mod3, norm1_g[0], w_in[0], t_lat)

    hg_args = (proj, hg_lb_fwd, hg_lb_bwd, hg_norm_g[0])
    oa_ctx, new_hf, new_hb = _hgrn(*hg_args, None, n_seq=b_ctx, t=t_ctx, row_off=0, emit_state=True)
    (oa_lat,) = _hgrn(*hg_args, (state_hgrn_fwd, state_hgrn_bwd), n_seq=b_lat, t=t_lat,
                      row_off=n_ctx_rows // t_lat, emit_state=False)

    p_f = jnp.broadcast_to(ret_log2_fwd[0][:, None, None], (RET_HEADS, 1, RET_DV))
    p_b = jnp.broadcast_to(ret_log2_bwd[0][:, None, None], (RET_HEADS, 1, RET_DV))
    ret_args = (proj, p_f, p_b, ret_norm_g[0])
    ob_ctx, new_rf, new_rb = _ret(*ret_args, None, None, n_seq=b_ctx, t=t_ctx, row_off=0, emit_state=True)
    (ob_lat,) = _ret(*ret_args, _rope_tables(t_lat), (state_ret_fwd, state_ret_bwd), n_seq=b_lat, t=t_lat,
                     row_off=n_ctx_rows // t_lat, emit_state=False)

    merged = _merge(oa_ctx, oa_lat, ob_ctx, ob_lat, proj, w_proj_hgrn[0].astype(BF16), w_proj_ret[0].astype(BF16))

    rw_pad = jnp.zeros((d, LANES), F32).at[:, :n_exp].set(router_w[0])
    rb_pad = jnp.full((1, LANES), -1e30, F32).at[0, :n_exp].set(router_b[0])
    x1, xp, top_idx, top_w = _router(merged, x_ctx, x_lat, mod3, norm2_g[0], w_out[0].astype(BF16),
                                     rw_pad, rb_pad, t_lat)

    dest, order2d, plan, n_rows = _moe_plan(top_idx[:, :TOP_K], n_exp)
    ff = moe_w_dn.shape[2]
    b_gu_perm = moe_b_gu[0].reshape(n_exp, ff // LANES, LANES, 2).transpose(0, 1, 3, 2).reshape(n_exp, 2 * ff)
    src = jnp.arange(MXU_DIM)
    perm = jnp.zeros((MXU_DIM, MXU_DIM), BF16).at[src, (src % 2) * LANES + src // 2].set(1)
    xp = xp.reshape(xp.shape[0] // (d // 2 // LANES), d // 2 // LANES, LANES)
    ys = _moe(xp, order2d, plan, moe_w_gu[0], moe_w_dn[0], b_gu_perm, moe_b_dn[0], perm, n_rows)

    comb_args = (dest, ys, top_w, x1, mod3, final_norm_g, n_ctx_rows, t_lat)
    y_prompt = _combine(*comb_args, first_row=0, n_out_rows=n_ctx_rows, name="combine_ctx")
    y_sample = _combine(*comb_args, first_row=n_ctx_rows, n_out_rows=b_lat * t_lat, name="combine_lat")
    return (y_prompt.reshape(b_ctx, t_ctx, d), y_sample.reshape(b_lat, t_lat, d), new_hf, new_hb, new_rf, new_rb)
```

```python
import functools

import jax
import jax.numpy as jnp
from jax import lax
from jax.experimental import pallas as pl
from jax.experimental.pallas import tpu as pltpu

F32 = jnp.float32
BF16 = jnp.bfloat16

HG_HEADS = 8
HG_DK = 128
HG_DV = 128
RET_HEADS = 8
RET_DK = 128
RET_DV = 256
CHUNK = 32
GRID_W = 64
ROPE_BASE = 10000.0
TOP_K = 4
SWIGLU_LIMIT = 7.0
SWIGLU_ALPHA = 1.702
N_MOD = 6
EPS = 1e-6

LANES = 128
F32_SUBLANES = 8
MXU_DIM = 256
VMEM_LIMIT = 56 * 1024 * 1024
COND_ROWS = 16
WIDE_TILE = 1024
MID_TILE = 512
ROW_TILE = 256
SCAN_BLOCK = 256
SCAN_UNROLL = 8
HGRN_ROWS_PER_STEP = 4096
RET_HEADS_PER_STEP = 4
RET_ROWS_PER_STEP = 2048
SMEM_TILE = 1024
MOE_GROUP = 128
MOE_ITEM_ROWS = 2048
MOE_GROUPS_PER_TRIP = 8
MOE_FF_TILE = 256
ROW_UNROLL = 8


def _sigmoid(x):
    return 0.5 * jnp.tanh(0.5 * x) + 0.5


def _params(*sem):
    return pltpu.CompilerParams(dimension_semantics=sem, vmem_limit_bytes=VMEM_LIMIT)


def _tile(n, pref):
    if n <= pref:
        return n
    for t in range(pref - pref % LANES, 0, -LANES):
        if n % t == 0:
            return t
    raise ValueError((n, pref))


def _adaln_kernel(c_ref, w_ref, b_ref, o_ref):
    c = c_ref[...]
    s = (c * _sigmoid(c)).astype(BF16)
    o_ref[...] = jnp.dot(s, w_ref[...].astype(BF16), preferred_element_type=F32) + b_ref[...]


def _adaln(cond, ada_w, ada_b):
    d, n = ada_w.shape
    tn = _tile(n, WIDE_TILE)
    return pl.pallas_call(
        _adaln_kernel,
        grid=(n // tn,),
        in_specs=[pl.BlockSpec((COND_ROWS, d), lambda j: (0, 0)),
                  pl.BlockSpec((d, tn), lambda j: (0, j)),
                  pl.BlockSpec((1, tn), lambda j: (0, j))],
        out_specs=pl.BlockSpec((COND_ROWS, tn), lambda j: (0, j)),
        out_shape=jax.ShapeDtypeStruct((COND_ROWS, n), F32),
        compiler_params=_params("arbitrary"),
        name="adaln",
    )(cond, ada_w, ada_b.reshape(1, n))


def _prenorm_kernel(xc_ref, xl_ref, sh_ref, sc_ref, g_ref, o_ref, *, n_ctx_tiles):
    x = jnp.where(pl.program_id(0) < n_ctx_tiles, xc_ref[...], xl_ref[...])
    ms = jnp.mean(x * x, axis=-1, keepdims=True)
    xn = x * lax.rsqrt(ms + EPS) * g_ref[...]
    o_ref[...] = (xn * (1.0 + sc_ref[0]) + sh_ref[0]).astype(BF16)


def _inproj_kernel(xm_ref, w_ref, o_ref):
    o_ref[...] = jnp.dot(xm_ref[...], w_ref[...].astype(BF16), preferred_element_type=F32)


def _mod_row_fn(tm, n_ctx_rows, t_lat):
    n_ctx_tiles = n_ctx_rows // tm

    def row(i):
        return jnp.where(i < n_ctx_tiles, 0, 1 + ((i - n_ctx_tiles) * tm) // t_lat)

    return row


def _group_specs(tm, width, n_ctx_tiles):
    ctx = lambda i: (jnp.minimum(i, n_ctx_tiles - 1), 0)
    lat = lambda i: (jnp.maximum(i - n_ctx_tiles, 0), 0)
    return [pl.BlockSpec((tm, width), ctx), pl.BlockSpec((tm, width), lat)]


def _inproj(x_ctx, x_lat, mod3, norm_g, w_in, t_lat):
    n_ctx_rows, d = x_ctx.shape
    t_all = n_ctx_rows + x_lat.shape[0]
    n = w_in.shape[1]
    tp = _tile(min(n_ctx_rows, t_lat), MID_TILE)
    row = _mod_row_fn(tp, n_ctx_rows, t_lat)
    xm = pl.pallas_call(
        functools.partial(_prenorm_kernel, n_ctx_tiles=n_ctx_rows // tp),
        grid=(t_all // tp,),
        in_specs=_group_specs(tp, d, n_ctx_rows // tp) + [
            pl.BlockSpec((1, 1, d), lambda i: (row(i) * N_MOD + 0, 0, 0)),
            pl.BlockSpec((1, 1, d), lambda i: (row(i) * N_MOD + 1, 0, 0)),
            pl.BlockSpec((1, d), lambda i: (0, 0))],
        out_specs=pl.BlockSpec((tp, d), lambda i: (i, 0)),
        out_shape=jax.ShapeDtypeStruct((t_all, d), BF16),
        compiler_params=_params("arbitrary"),
        name="prenorm",
    )(x_ctx, x_lat, mod3, mod3, norm_g.reshape(1, d))
    tm = _tile(t_all, WIDE_TILE)
    tn = _tile(n, WIDE_TILE)
    return pl.pallas_call(
        _inproj_kernel,
        grid=(t_all // tm, n // tn),
        in_specs=[pl.BlockSpec((tm, d), lambda i, j: (i, 0)),
                  pl.BlockSpec((d, tn), lambda i, j: (0, j))],
        out_specs=pl.BlockSpec((tm, tn), lambda i, j: (i, j)),
        out_shape=jax.ShapeDtypeStruct((t_all, n), F32),
        compiler_params=_params("arbitrary", "arbitrary"),
        name="inproj",
    )(xm, w_in)


def _dot_split(m_bf, x):
    hi = x.astype(BF16)
    lo = (x - hi.astype(F32)).astype(BF16)
    return jnp.dot(m_bf, hi, preferred_element_type=F32) + jnp.dot(m_bf, lo, preferred_element_type=F32)


_NT = (((1,), (1,)), ((), ()))
_TN = (((0,), (0,)), ((), ()))


def _hgrn_kernel(*refs, t, hpg, has_state, emit_state):
    hq_ref, hzf_ref, hzb_ref, hi_ref, hgate_ref, lbf_ref, lbb_ref, gn_ref = refs[:8]
    pos = 8
    if has_state:
        s0f_ref, s0b_ref = refs[pos:pos + 2]
        pos += 2
    out_ref = refs[pos]
    pos += 1
    if emit_state:
        sf_ref, sb_ref = refs[pos:pos + 2]
        pos += 2
    (q_in_f, k_end_f, dec_f, q_in_b, k_end_b, dec_b, v_bf, o_f, o_b, st_ref) = refs[pos:]
    heads = [slice(j * LANES, (j + 1) * LANES) for j in range(hpg)]

    blk = min(SCAN_BLOCK, t)
    n_blk = t // blk
    n_chunk = t // CHUNK

    r = lax.broadcasted_iota(jnp.int32, (blk, blk), 0)
    c = lax.broadcasted_iota(jnp.int32, (blk, blk), 1)
    same = (r // CHUNK) == (c // CHUNK)
    low = same & (c <= r)
    upp = same & (c >= r)
    m_pre = jnp.where(low, 1.0, 0.0).astype(BF16)
    m_suf = jnp.where(upp, 1.0, 0.0).astype(BF16)

    def lower_bound(lb_ref):
        a = lb_ref[...]
        e = jnp.exp(a - jnp.max(a, axis=0, keepdims=True))
        return e[0:1] / jnp.sum(e, axis=0, keepdims=True)

    lb_f = lower_bound(lbf_ref)
    lb_b = lower_bound(lbb_ref)

    width = hpg * LANES

    def chunk_total(g, reverse):
        g3 = g.reshape(blk // CHUNK, CHUNK, width)
        edge = g3[:, 0:1, :] if reverse else g3[:, CHUNK - 1:CHUNK, :]
        return jnp.broadcast_to(edge, g3.shape).reshape(blk, width)

    directions = ((hzf_ref, lb_f, m_pre, low, False, q_in_f, k_end_f, dec_f, o_f),
                  (hzb_ref, lb_b, m_suf, upp, True, q_in_b, k_end_b, dec_b, o_b))

    def prep(i, carry):
        rows = pl.ds(pl.multiple_of(i * blk, blk), blk)
        hq = hq_ref[rows, :]
        q = hq * _sigmoid(hq)
        v = hi_ref[rows, :].astype(BF16)
        v_bf[rows, :] = v
        for z_ref, lb, m_cum, mask, reverse, qi, ke, dc, o_ref in directions:
            f = lb + (1.0 - lb) * _sigmoid(z_ref[rows, :])
            k = 1.0 - f
            g = _dot_split(m_cum, jnp.log(f))
            tot = chunk_total(g, reverse)
            q_in = (q * jnp.exp(g)).astype(BF16)
            k_in = (k * jnp.exp(-g)).astype(BF16)
            qi[rows, :] = q_in
            ke[rows, :] = (k * jnp.exp(tot - g)).astype(BF16)
            dc[rows, :] = jnp.exp(tot)
            for hd in heads:
                s = lax.dot_general(q_in[:, hd], k_in[:, hd], _NT, preferred_element_type=F32)
                o_ref[rows, hd] = jnp.dot(jnp.where(mask, s, 0.0).astype(BF16), v[:, hd],
                                          preferred_element_type=F32)
        return carry

    lax.fori_loop(0, n_blk, prep, 0)

    for j in range(hpg):
        for slot, s0_ref in ((2 * j, s0f_ref if has_state else None), (2 * j + 1, s0b_ref if has_state else None)):
            st_ref[slot] = s0_ref[j].T if has_state else jnp.zeros((HG_DV, HG_DK), F32)

    def chunk_step(idx, slot, hd, qi, ke, dc, o_ref):
        start = pl.multiple_of(idx * CHUNK, CHUNK)
        rows = pl.ds(start, CHUNK)
        st = st_ref[slot]
        o_ref[rows, hd] += lax.dot_general(qi[rows, hd], st.astype(BF16), _NT, preferred_element_type=F32)
        upd = lax.dot_general(v_bf[rows, hd], ke[rows, hd], _TN, preferred_element_type=F32)
        st_ref[slot] = dc[pl.ds(start, 1), hd] * st + upd

    def body(i, carry):
        for j, hd in enumerate(heads):
            chunk_step(i, 2 * j, hd, q_in_f, k_end_f, dec_f, o_f)
            chunk_step(n_chunk - 1 - i, 2 * j + 1, hd, q_in_b, k_end_b, dec_b, o_b)
        return carry

    lax.fori_loop(0, n_chunk, body, 0, unroll=min(n_chunk, SCAN_UNROLL))
    if emit_state:
        for j in range(hpg):
            sf_ref[j] = st_ref[2 * j].T
            sb_ref[j] = st_ref[2 * j + 1].T

    for hd in heads:
        o = o_f[:, hd] + o_b[:, hd]
        o = o * lax.rsqrt(jnp.mean(o * o, axis=-1, keepdims=True) + EPS) * gn_ref[...]
        out_ref[:, hd] = (o * _sigmoid(hgate_ref[:, hd])).astype(BF16)


def _hgrn(proj, lb_fwd, lb_bwd, norm_g, states, *, n_seq, t, row_off, emit_state):
    nh = HG_HEADS
    hpg = max(1, min(nh, HGRN_ROWS_PER_STEP // t))
    assert nh % hpg == 0
    ng = nh // hpg
    w = hpg * LANES
    col = lambda k: (lambda b, h: (row_off + b, k * ng + h))
    blk = lambda k: pl.BlockSpec((t, w), col(k))
    in_specs = [blk(0), blk(1), blk(2), blk(3), blk(4),
                pl.BlockSpec((2, w), lambda b, h: (0, h)),
                pl.BlockSpec((2, w), lambda b, h: (0, h)),
                pl.BlockSpec((1, LANES), lambda b, h: (0, 0))]
    args = [proj] * 5 + [lb_fwd, lb_bwd, norm_g.reshape(1, HG_DV)]
    st_spec = pl.BlockSpec((None, None, hpg, HG_DK, HG_DV), lambda b, h: (b, 0, h, 0, 0))
    if states is not None:
        in_specs += [st_spec, st_spec]
        args += list(states)
    out_specs = [pl.BlockSpec((t, w), lambda b, h: (b, h))]
    out_shape = [jax.ShapeDtypeStruct((n_seq * t, nh * HG_DV), BF16)]
    if emit_state:
        out_specs += [st_spec, st_spec]
        out_shape += [jax.ShapeDtypeStruct((n_seq, 1, nh, HG_DK, HG_DV), F32)] * 2
    scratch = ([pltpu.VMEM((t, w), BF16)] * 2 + [pltpu.VMEM((t, w), F32)]) * 2
    scratch += [pltpu.VMEM((t, w), BF16), pltpu.VMEM((t, w), F32), pltpu.VMEM((t, w), F32),
                pltpu.VMEM((2 * hpg, HG_DV, HG_DK), F32)]
    return pl.pallas_call(
        functools.partial(_hgrn_kernel, t=t, hpg=hpg, has_state=states is not None, emit_state=emit_state),
        grid=(n_seq, ng),
        in_specs=in_specs,
        out_specs=out_specs,
        out_shape=out_shape,
        scratch_shapes=scratch,
        compiler_params=_params("arbitrary", "arbitrary"),
        name="hgrn_lat" if states is not None else "hgrn_ctx",
    )(*args)


def _ret_kernel(*refs, t, hpg, has_state, emit_state, use_rope):
    rq_ref, rk_ref, rv_ref, rgate_ref, pf_ref, pb_ref, gn_ref = refs[:7]
    pos = 7
    if use_rope:
        cos_ref, sin_ref = refs[pos:pos + 2]
        pos += 2
    if has_state:
        s0f_ref, s0b_ref = refs[pos:pos + 2]
        pos += 2
    out_ref = refs[pos]
    pos += 1
    if emit_state:
        sf_ref, sb_ref = refs[pos:pos + 2]
        pos += 2
    q_s, k_s, v_s, o_f, o_b, st_ref = refs[pos:]

    ch = min(MXU_DIM, t)
    n_chunk = t // ch

    lane = lax.broadcasted_iota(jnp.int32, (t, RET_DK), 1)
    first_half = (lane % 64) < 32

    def rope(x):
        if not use_rope:
            return x
        swapped = jnp.where(first_half, pltpu.roll(x, RET_DK - 32, 1), pltpu.roll(x, 32, 1))
        return x * cos_ref[...] + swapped * sin_ref[...]

    ri = lax.broadcasted_iota(jnp.int32, (ch, ch), 0)
    ci = lax.broadcasted_iota(jnp.int32, (ch, ch), 1)
    row128 = lax.broadcasted_iota(jnp.int32, (ch, RET_DK), 0).astype(F32)

    def direction(j, p_ref, slot, o_scr, s0_ref, reverse):
        qk = slice(j * RET_DK, (j + 1) * RET_DK)
        vv = slice(j * RET_DV, (j + 1) * RET_DV)
        lg = jnp.log1p(-jnp.exp2(p_ref[j]))
        lg_k = lg[:, :RET_DK]
        dist = (ci - ri) if reverse else (ri - ci)
        keep = dist >= 0
        dmask = jnp.where(keep, jnp.exp(lg[:, :ch] * jnp.where(keep, dist, 0).astype(F32)), 0.0)
        if reverse:
            q_scale = jnp.exp(lg_k * (ch - row128))
            k_scale = jnp.exp(lg_k * row128)
        else:
            q_scale = jnp.exp(lg_k * (row128 + 1.0))
            k_scale = jnp.exp(lg_k * (ch - 1.0 - row128))
        dec = jnp.exp(lg * float(ch))
        st_ref[slot] = s0_ref[j] if has_state else jnp.zeros((RET_DK, RET_DV), F32)
        for step in range(n_chunk):
            cidx = n_chunk - 1 - step if reverse else step
            rows = pl.ds(cidx * ch, ch)
            q = q_s[rows, qk]
            k = k_s[rows, qk]
            v = v_s[rows, vv]
            s = lax.dot_general(q.astype(BF16), k.astype(BF16), _NT, preferred_element_type=F32)
            o = jnp.dot((s * dmask).astype(BF16), v, preferred_element_type=F32)
            st = st_ref[slot]
            o = o + jnp.dot((q * q_scale).astype(BF16), st.astype(BF16), preferred_element_type=F32)
            o_scr[rows, vv] = o
            upd = lax.dot_general((k * k_scale).astype(BF16), v, _TN, preferred_element_type=F32)
            st_ref[slot] = dec * st + upd

    for j in range(hpg):
        qk = slice(j * RET_DK, (j + 1) * RET_DK)
        q_s[:, qk] = rope(rq_ref[:, qk])
        k_s[:, qk] = rope(rk_ref[:, qk] * (RET_DK ** -0.5))
    v_s[...] = rv_ref[...].astype(BF16)
    for j in range(hpg):
        direction(j, pf_ref, 2 * j, o_f, s0f_ref if has_state else None, False)
        direction(j, pb_ref, 2 * j + 1, o_b, s0b_ref if has_state else None, True)
    if emit_state:
        for j in range(hpg):
            sf_ref[j] = st_ref[2 * j]
            sb_ref[j] = st_ref[2 * j + 1]

    for j in range(hpg):
        vv = slice(j * RET_DV, (j + 1) * RET_DV)
        o = o_f[:, vv] + o_b[:, vv]
        oc = o - jnp.mean(o, axis=-1, keepdims=True)
        o = oc * lax.rsqrt(jnp.mean(oc * oc, axis=-1, keepdims=True) + EPS) * gn_ref[...]
        gate = rgate_ref[:, vv]
        out_ref[:, vv] = (o * (gate * _sigmoid(gate))).astype(BF16)


def _ret(proj, p_fwd, p_bwd, norm_g, rope, states, *, n_seq, t, row_off, emit_state):
    nh = RET_HEADS
    base = (3 * HG_HEADS * HG_DK + 2 * HG_HEADS * HG_DV)
    hpg = max(1, min(nh, RET_HEADS_PER_STEP, RET_ROWS_PER_STEP // t))
    while nh % hpg or base % (hpg * RET_DK) or (base + 2 * nh * RET_DK) % (hpg * RET_DV):
        hpg -= 1
    ng = nh // hpg
    wk, wv = hpg * RET_DK, hpg * RET_DV
    qk0 = base // wk
    v0 = (base + 2 * nh * RET_DK) // wv
    in_specs = [pl.BlockSpec((t, wk), lambda b, h: (row_off + b, qk0 + h)),
                pl.BlockSpec((t, wk), lambda b, h: (row_off + b, qk0 + ng + h)),
                pl.BlockSpec((t, wv), lambda b, h: (row_off + b, v0 + h)),
                pl.BlockSpec((t, wv), lambda b, h: (row_off + b, v0 + ng + h)),
                pl.BlockSpec((hpg, 1, RET_DV), lambda b, h: (h, 0, 0)),
                pl.BlockSpec((hpg, 1, RET_DV), lambda b, h: (h, 0, 0)),
                pl.BlockSpec((1, RET_DV), lambda b, h: (0, 0))]
    args = [proj] * 4 + [p_fwd, p_bwd, norm_g.reshape(1, RET_DV)]
    if rope is not None:
        in_specs += [pl.BlockSpec((t, RET_DK), lambda b, h: (0, 0))] * 2
        args += list(rope)
    st_spec = pl.BlockSpec((None, None, hpg, RET_DK, RET_DV), lambda b, h: (b, 0, h, 0, 0))
    if states is not None:
        in_specs += [st_spec, st_spec]
        args += list(states)
    out_specs = [pl.BlockSpec((t, wv), lambda b, h: (b, h))]
    out_shape = [jax.ShapeDtypeStruct((n_seq * t, nh * RET_DV), BF16)]
    if emit_state:
        out_specs += [st_spec, st_spec]
        out_shape += [jax.ShapeDtypeStruct((n_seq, 1, nh, RET_DK, RET_DV), F32)] * 2
    scratch = [pltpu.VMEM((t, wk), F32), pltpu.VMEM((t, wk), F32), pltpu.VMEM((t, wv), BF16),
               pltpu.VMEM((t, wv), F32), pltpu.VMEM((t, wv), F32),
               pltpu.VMEM((2 * hpg, RET_DK, RET_DV), F32)]
    return pl.pallas_call(
        functools.partial(_ret_kernel, t=t, hpg=hpg, has_state=states is not None, emit_state=emit_state,
                          use_rope=rope is not None),
        grid=(n_seq, ng),
        in_specs=in_specs,
        out_specs=out_specs,
        out_shape=out_shape,
        scratch_shapes=scratch,
        compiler_params=_params("arbitrary", "arbitrary"),
        name="ret_lat" if states is not None else "ret_ctx",
    )(*args)


def _rope_tables(t):
    pairs = RET_DK // 4
    pos = jnp.arange(t, dtype=jnp.int32)
    rows = (pos // GRID_W).astype(F32)
    cols = (pos % GRID_W).astype(F32)
    inv = ROPE_BASE ** (-jnp.arange(pairs, dtype=F32) / pairs)
    ar = rows[:, None] * inv
    ac = cols[:, None] * inv
    cos = jnp.concatenate([jnp.cos(ar), jnp.cos(ar), jnp.cos(ac), jnp.cos(ac)], axis=1)
    sin = jnp.concatenate([-jnp.sin(ar), jnp.sin(ar), -jnp.sin(ac), jnp.sin(ac)], axis=1)
    return cos, sin


def _merge_kernel(oac_ref, oal_ref, obc_ref, obl_ref, ga_ref, gb_ref, wa_ref, wb_ref, o_ref, *, n_ctx_tiles):
    is_ctx = pl.program_id(1) < n_ctx_tiles
    oa = jnp.where(is_ctx, oac_ref[...], oal_ref[...])
    ob = jnp.where(is_ctx, obc_ref[...], obl_ref[...])
    a = jnp.dot(oa, wa_ref[...], preferred_element_type=F32)
    b = jnp.dot(ob, wb_ref[...], preferred_element_type=F32)
    o_ref[...] = (_sigmoid(ga_ref[...]) * a + _sigmoid(gb_ref[...]) * b).astype(BF16)


def _merge(oa_ctx, oa_lat, ob_ctx, ob_lat, proj, wa_bf, wb_bf):
    n_ctx_rows, ka = oa_ctx.shape
    t_all = n_ctx_rows + oa_lat.shape[0]
    kb = ob_ctx.shape[1]
    d = wa_bf.shape[1]
    tm = _tile(min(n_ctx_rows, oa_lat.shape[0]), MID_TILE)
    tn = _tile(d, WIDE_TILE)
    nct = n_ctx_rows // tm
    gate0 = (proj.shape[1] - 2 * d) // tn
    ctx = lambda j, i: (jnp.minimum(i, nct - 1), 0)
    lat = lambda j, i: (jnp.maximum(i - nct, 0), 0)
    return pl.pallas_call(
        functools.partial(_merge_kernel, n_ctx_tiles=nct),
        grid=(d // tn, t_all // tm),
        in_specs=[pl.BlockSpec((tm, ka), ctx), pl.BlockSpec((tm, ka), lat),
                  pl.BlockSpec((tm, kb), ctx), pl.BlockSpec((tm, kb), lat),
                  pl.BlockSpec((tm, tn), lambda j, i: (i, gate0 + j)),
                  pl.BlockSpec((tm, tn), lambda j, i: (i, gate0 + d // tn + j)),
                  pl.BlockSpec((ka, tn), lambda j, i: (0, j)),
                  pl.BlockSpec((kb, tn), lambda j, i: (0, j))],
        out_specs=pl.BlockSpec((tm, tn), lambda j, i: (i, j)),
        out_shape=jax.ShapeDtypeStruct((t_all, d), BF16),
        compiler_params=_params("arbitrary", "arbitrary"),
        name="merge",
    )(oa_ctx, oa_lat, ob_ctx, ob_lat, proj, proj, wa_bf, wb_bf)


def _pack_bf16_pairs(x):
    half = x.shape[1] // 2
    lo = lax.bitcast_convert_type(x[:, :half].astype(BF16).astype(F32), jnp.uint32)
    hi = lax.bitcast_convert_type(x[:, half:].astype(BF16).astype(F32), jnp.uint32)
    return (lo >> 16) | (hi & jnp.uint32(0xFFFF0000))


def _unpack_bf16_pairs(w):
    lo = lax.bitcast_convert_type(w << 16, F32).astype(BF16)
    hi = lax.bitcast_convert_type(w & jnp.uint32(0xFFFF0000), F32).astype(BF16)
    return lo, hi


def _router_kernel(m_ref, xc_ref, xl_ref, g1_ref, sh_ref, sc_ref, ng_ref, wo_ref, rw2_ref, rwh_ref, rb_ref,
                   x1_ref, xp_ref, idx_ref, wgt_ref, *, n_ctx_tiles):
    out = jnp.dot(m_ref[...], wo_ref[...], preferred_element_type=F32)
    x = jnp.where(pl.program_id(0) < n_ctx_tiles, xc_ref[...], xl_ref[...])
    x1 = x + g1_ref[0] * out
    x1_ref[...] = x1
    ms = jnp.mean(x1 * x1, axis=-1, keepdims=True)
    xm = x1 * lax.rsqrt(ms + EPS) * ng_ref[...]
    xm = xm * (1.0 + sc_ref[0]) + sh_ref[0]
    packed = _pack_bf16_pairs(xm)
    n_slab = packed.shape[1] // LANES
    for s in range(n_slab):
        xp_ref[pl.ds(s, packed.shape[0], stride=n_slab), :] = packed[:, s * LANES:(s + 1) * LANES]

    x_hi = xm.astype(BF16)
    x_lo = (xm - x_hi.astype(F32)).astype(BF16)
    hh_hl = jnp.dot(x_hi, rw2_ref[...], preferred_element_type=F32)
    lh = jnp.dot(x_lo, rwh_ref[...], preferred_element_type=F32)
    logits = hh_hl[:, :LANES] + (hh_hl[:, LANES:] + lh) + rb_ref[...]
    lane = lax.broadcasted_iota(jnp.int32, logits.shape, 1)
    lane_f = lane.astype(F32)
    idx_out = jnp.zeros(logits.shape, F32)
    val_out = jnp.zeros(logits.shape, F32)
    top = None
    for k in range(TOP_K):
        m = jnp.max(logits, axis=-1, keepdims=True)
        idx = jnp.min(jnp.where(logits == m, lane_f, float(LANES)), axis=-1, keepdims=True)
        if top is None:
            top = m
        idx_out = jnp.where(lane == k, idx, idx_out)
        val_out = jnp.where(lane == k, jnp.exp(m - top), val_out)
        logits = jnp.where(lane_f == idx, -jnp.inf, logits)
    idx_ref[...] = idx_out.astype(jnp.int32)
    wgt_ref[...] = val_out / jnp.sum(val_out, axis=-1, keepdims=True)


def _router(merged, x_ctx, x_lat, mod3, norm_g, wo_bf, rw_pad, rb_pad, t_lat):
    n_ctx_rows, d = x_ctx.shape
    t_all = n_ctx_rows + x_lat.shape[0]
    tm = _tile(min(n_ctx_rows, t_lat), ROW_TILE)
    row = _mod_row_fn(tm, n_ctx_rows, t_lat)
    mod_spec = lambda k: pl.BlockSpec((1, 1, d), lambda i: (row(i) * N_MOD + k, 0, 0))
    rw_hi = rw_pad.astype(BF16)
    rw_lo = (rw_pad - rw_hi.astype(F32)).astype(BF16)
    rw2 = jnp.concatenate([rw_hi, rw_lo], axis=1)
    return pl.pallas_call(
        functools.partial(_router_kernel, n_ctx_tiles=n_ctx_rows // tm),
        grid=(t_all // tm,),
        in_specs=[pl.BlockSpec((tm, d), lambda i: (i, 0))] + _group_specs(tm, d, n_ctx_rows // tm) + [
                  mod_spec(2), mod_spec(3), mod_spec(4),
                  pl.BlockSpec((1, d), lambda i: (0, 0)),
                  pl.BlockSpec((d, d), lambda i: (0, 0)),
                  pl.BlockSpec((d, 2 * LANES), lambda i: (0, 0)),
                  pl.BlockSpec((d, LANES), lambda i: (0, 0)),
                  pl.BlockSpec((1, LANES), lambda i: (0, 0))],
        out_specs=[pl.BlockSpec((tm, d), lambda i: (i, 0)),
                   pl.BlockSpec((tm * (d // 2 // LANES), LANES), lambda i: (i, 0)),
                   pl.BlockSpec((tm, LANES), lambda i: (i, 0)),
                   pl.BlockSpec((tm, LANES), lambda i: (i, 0))],
        out_shape=[jax.ShapeDtypeStruct((t_all, d), F32),
                   jax.ShapeDtypeStruct((t_all * (d // 2 // LANES), LANES), jnp.uint32),
                   jax.ShapeDtypeStruct((t_all, LANES), jnp.int32),
                   jax.ShapeDtypeStruct((t_all, LANES), F32)],
        compiler_params=_params("arbitrary"),
        name="router",
    )(merged, x_ctx, x_lat, mod3, mod3, mod3, norm_g.reshape(1, d), wo_bf, rw2, rw_hi, rb_pad)


def _moe_kernel(e_ref, start_ref, nsub_ref, src_ref, tail_ref,
                xp_hbm, order_hbm, wgu_ref, wdn_ref, bgu_ref, bdn_ref, perm_ref,
                ys_hbm,
                tok_smem, x_raw, x_lo, x_hi, y_acc, wgu_bf, wdn_bf, pend, tok_sem, row_sem, out_sem,
                *, n_ff_tiles):
    i = pl.program_id(0)
    f = pl.program_id(1)
    nsub = nsub_ref[i]
    start = start_ref[i]
    n_slab = xp_hbm.shape[1]
    g = MOE_GROUP
    d = y_acc.shape[1]
    half = d // 2
    ff2 = wgu_bf.shape[1]

    def order_copy(it):
        first = pl.multiple_of(src_ref[it] & ~(SMEM_TILE - 1), SMEM_TILE)
        return pltpu.make_async_copy(order_hbm.at[pl.ds(first, tok_smem.shape[0])], tok_smem, tok_sem)

    def row_copy(off, base, u):
        tok = lax.shift_right_logical(tok_smem[off + base + u], 2)
        return pltpu.make_async_copy(xp_hbm.at[tok], x_raw.at[pl.ds((base + u) * n_slab, n_slab), :], row_sem)

    def gather_start(it):
        off = src_ref[it] & (SMEM_TILE - 1)

        def issue(r, c):
            for u in range(ROW_UNROLL):
                row_copy(off, r * ROW_UNROLL, u).start()
            return c

        lax.fori_loop(0, nsub_ref[it] * (g // ROW_UNROLL), issue, 0)

    def gather_wait(it):
        off = src_ref[it] & (SMEM_TILE - 1)

        def drain(r, c):
            for u in range(ROW_UNROLL):
                row_copy(off, r * ROW_UNROLL, u).wait()
            return c

        lax.fori_loop(0, nsub_ref[it] * (g // ROW_UNROLL), drain, 0)

    nxt = jnp.minimum(i + 1, pl.num_programs(0) - 1)
    has_next = (i + 1 < pl.num_programs(0)) & (nsub_ref[nxt] > 0)

    def out_drain():
        def drain(k, c):
            pltpu.make_async_copy(y_acc.at[0:g, :], ys_hbm.at[0:g, :], out_sem).wait()
            return c

        lax.fori_loop(0, pend[0], drain, 0)
        pend[0] = 0

    @pl.when((i == 0) & (f == 0))
    def _():
        pend[0] = 0

    @pl.when((i == 0) & (f == 0) & (nsub > 0))
    def _():
        order_copy(0).start()
        order_copy(0).wait()
        gather_start(0)

    @pl.when((f == 0) & (nsub > 0))
    def _():
        gather_wait(i)

        def unpack(sb, c):
            rows = pl.ds(pl.multiple_of(sb * g, g), g)
            for s in range(n_slab):
                lo, hi = _unpack_bf16_pairs(x_raw[pl.ds(sb * (g * n_slab) + s, g, stride=n_slab), :])
                x_lo[rows, s * LANES:(s + 1) * LANES] = lo
                x_hi[rows, s * LANES:(s + 1) * LANES] = hi
            return c

        lax.fori_loop(0, nsub, unpack, 0)
        out_drain()

        def init(sb, c):
            rows = pl.ds(pl.multiple_of(sb * g, g), g)
            y_acc[rows, :] = jnp.broadcast_to(bdn_ref[...], (g, d))
            return c

        lax.fori_loop(0, nsub, init, 0)

        @pl.when(has_next)
        def _():
            order_copy(nxt).start()

    @pl.when((f == 1) & (nsub > 0) & has_next)
    def _():
        order_copy(nxt).wait()
        gather_start(nxt)

    @pl.when(nsub > 0)
    def _():
        for j in range(ff2 // MXU_DIM):
            cols = slice(j * MXU_DIM, (j + 1) * MXU_DIM)
            wgu_bf[:, cols] = jnp.dot(wgu_ref[:, cols].astype(BF16), perm_ref[...],
                                      preferred_element_type=F32).astype(BF16)
        wdn_bf[...] = wdn_ref[...].astype(BF16)
        bgu = bgu_ref[...]

        def expert_rows(first_row, n_rows):
            rows = pl.ds(pl.multiple_of(first_row, g), n_rows)
            h = (jnp.dot(x_lo[rows, :], wgu_bf[:half, :], preferred_element_type=F32)
                 + jnp.dot(x_hi[rows, :], wgu_bf[half:, :], preferred_element_type=F32) + bgu)
            acts = []
            for j in range(ff2 // MXU_DIM):
                gate = jnp.minimum(h[:, j * MXU_DIM: j * MXU_DIM + LANES], SWIGLU_LIMIT)
                up = jnp.clip(h[:, j * MXU_DIM + LANES: (j + 1) * MXU_DIM], -SWIGLU_LIMIT, SWIGLU_LIMIT)
                acts.append(((up + 1.0) * gate * _sigmoid(SWIGLU_ALPHA * gate)).astype(BF16))
            act = jnp.concatenate(acts, axis=1)
            y_acc[rows, :] += jnp.dot(act, wdn_bf[...], preferred_element_type=F32)

        big = MOE_GROUPS_PER_TRIP

        def trip(q, c):
            expert_rows(q * (big * g), big * g)
            return c

        lax.fori_loop(0, lax.shift_right_logical(nsub, big.bit_length() - 1), trip, 0)
        rem = nsub & (big - 1)
        part = big // 2
        while part >= 1:
            @pl.when((rem & part) != 0)
            def _(part=part):
                expert_rows((nsub - rem + (rem & ~(2 * part - 1))) * g, part * g)
            part //= 2

    @pl.when((f == n_ff_tiles - 1) & (nsub > 0))
    def _():
        def out_copy(sb):
            rows = pl.ds(pl.multiple_of(sb * g, g), g)
            dst = pl.ds(pl.multiple_of(start * LANES + sb * g, g), g)
            return pltpu.make_async_copy(y_acc.at[rows, :], ys_hbm.at[dst, :], out_sem)

        def issue(sb, c):
            out_copy(sb).start()
            return c

        lax.fori_loop(0, nsub, issue, 0)
        pend[0] = nsub

    @pl.when((i == pl.num_programs(0) - 1) & (f == n_ff_tiles - 1))
    def _():
        out_drain()
        y_acc[0:g, :] = jnp.zeros((g, d), F32)
        first = tail_ref[0]
        n_tail = ys_hbm.shape[0] // g - first

        def tail_copy(k):
            dst = pl.ds(pl.multiple_of((first + k) * g, g), g)
            return pltpu.make_async_copy(y_acc.at[0:g, :], ys_hbm.at[dst, :], out_sem)

        def issue(k, c):
            tail_copy(k).start()
            return c

        def drain(k, c):
            tail_copy(k).wait()
            return c

        lax.fori_loop(0, n_tail, issue, 0)
        lax.fori_loop(0, n_tail, drain, 0)


def _moe(xp, order2d, plan, w_gu, w_dn, b_gu_perm, b_dn, perm, n_rows):
    n_exp, d, ff2_all = w_gu.shape
    ff = ff2_all // 2
    tf = _tile(ff, MOE_FF_TILE)
    n_f = ff // tf
    assert n_f >= 2, "the next item's row gather is started in the second feature step"
    n_items = plan[0].shape[0]

    def f_eff(i, f, nsub):
        return jnp.where(nsub[i] > 0, f, n_f - 1)

    grid_spec = pltpu.PrefetchScalarGridSpec(
        num_scalar_prefetch=len(plan),
        grid=(n_items, n_f),
        in_specs=[pl.BlockSpec(memory_space=pl.ANY),
                  pl.BlockSpec(memory_space=pl.ANY),
                  pl.BlockSpec((None, d, 2 * tf), lambda i, f, e, s, n, *_: (e[i], 0, f_eff(i, f, n))),
                  pl.BlockSpec((None, tf, d), lambda i, f, e, s, n, *_: (e[i], f_eff(i, f, n), 0)),
                  pl.BlockSpec((None, 1, 2 * tf), lambda i, f, e, s, n, *_: (e[i], 0, f_eff(i, f, n))),
                  pl.BlockSpec((None, 1, d), lambda i, f, e, s, n, *_: (e[i], 0, 0)),
                  pl.BlockSpec((MXU_DIM, MXU_DIM), lambda i, f, e, s, n, *_: (0, 0))],
        out_specs=pl.BlockSpec(memory_space=pl.ANY),
        scratch_shapes=[pltpu.SMEM((MOE_ITEM_ROWS + SMEM_TILE,), jnp.int32),
                        pltpu.VMEM((MOE_ITEM_ROWS * xp.shape[1], LANES), jnp.uint32),
                        pltpu.VMEM((MOE_ITEM_ROWS, d // 2), BF16),
                        pltpu.VMEM((MOE_ITEM_ROWS, d // 2), BF16),
                        pltpu.VMEM((MOE_ITEM_ROWS, d), F32),
                        pltpu.VMEM((d, 2 * tf), BF16),
                        pltpu.VMEM((tf, d), BF16),
                        pltpu.SMEM((1,), jnp.int32),
                        pltpu.SemaphoreType.DMA(()),
                        pltpu.SemaphoreType.DMA(()),
                        pltpu.SemaphoreType.DMA(())],
    )
    return pl.pallas_call(
        functools.partial(_moe_kernel, n_ff_tiles=n_f),
        grid_spec=grid_spec,
        out_shape=jax.ShapeDtypeStruct((n_rows, d), F32),
        compiler_params=_params("arbitrary", "arbitrary"),
        name="moe",
    )(*plan, xp, order2d, w_gu, w_dn,
      b_gu_perm.reshape(n_exp, 1, ff2_all), b_dn.reshape(n_exp, 1, d), perm)


def _moe_plan(top_idx, n_exp):
    t_all = top_idx.shape[0]
    n_assign = t_all * TOP_K
    g = MOE_GROUP
    rows_max = n_assign + n_exp * g
    n_items = n_exp + rows_max // MOE_ITEM_ROWS
    n_rows = rows_max + MOE_ITEM_ROWS

    flat_e = top_idx.reshape(-1).astype(jnp.int32)
    assign = jnp.arange(n_assign, dtype=jnp.int32)
    e_sorted, order = lax.sort_key_val(flat_e, assign)
    experts = jnp.arange(n_exp, dtype=jnp.int32)
    first = jnp.sum(e_sorted[:, None] < experts[None, :], axis=0, dtype=jnp.int32)
    counts = jnp.concatenate([first[1:], jnp.full((1,), n_assign, jnp.int32)]) - first
    padded = (counts + g - 1) // g * g
    pad_end = jnp.cumsum(padded)
    pad_start = pad_end - padded
    dest_sorted = pad_start[e_sorted] + assign - first[e_sorted]
    _, dest = lax.sort_key_val(order, dest_sorted)
    assert n_assign % SMEM_TILE == 0
    order2d = jnp.concatenate([order, jnp.zeros((MOE_ITEM_ROWS + SMEM_TILE,), jnp.int32)])

    chunks = (padded + MOE_ITEM_ROWS - 1) // MOE_ITEM_ROWS
    chunk_end = jnp.cumsum(chunks)
    items = jnp.arange(n_items, dtype=jnp.int32)
    n_used = chunk_end[-1]
    last_e = jnp.max(jnp.where(chunks > 0, jnp.arange(n_exp, dtype=jnp.int32), 0))
    e_of = jnp.minimum(jnp.searchsorted(chunk_end, items, side="right"), n_exp - 1).astype(jnp.int32)
    local = items - (chunk_end[e_of] - chunks[e_of])
    used = items < n_used
    item_e = jnp.where(used, e_of, last_e).astype(jnp.int32)
    left = padded[e_of] - local * MOE_ITEM_ROWS
    item_nsub = jnp.where(used, jnp.clip(left, 0, MOE_ITEM_ROWS) // g, 0).astype(jnp.int32)
    item_start = jnp.where(used, (pad_start[e_of] + local * MOE_ITEM_ROWS) // LANES, 0).astype(jnp.int32)
    item_src = jnp.where(used, first[e_of] + local * MOE_ITEM_ROWS, 0).astype(jnp.int32)
    tail = (jnp.sum(padded) // g).astype(jnp.int32).reshape(1)
    return dest, order2d, (item_e, item_start, item_nsub, item_src, tail), n_rows


def _combine_kernel(dest_ref, ys_hbm, wgt_ref, x1_ref, g2_ref, fg_ref, o_ref, gbuf, sem, *, tm):
    d_model = gbuf.shape[-1]

    def row_copy(trip, j, k):
        d = dest_ref[trip * (F32_SUBLANES * TOP_K) + j * TOP_K + k]
        return pltpu.make_async_copy(ys_hbm.at[pl.ds(d, 1), :], gbuf.at[k, trip, pl.ds(j, 1), :], sem)

    def issue(trip, c):
        for j in range(F32_SUBLANES):
            for k in range(TOP_K):
                row_copy(trip, j, k).start()
        return c

    def drain(trip, c):
        for j in range(F32_SUBLANES):
            for k in range(TOP_K):
                row_copy(trip, j, k).wait()
        return c

    lax.fori_loop(0, tm // F32_SUBLANES, issue, 0)
    lax.fori_loop(0, tm // F32_SUBLANES, drain, 0)

    w = wgt_ref[...]
    y = w[:, 0:1] * gbuf[0].reshape(tm, d_model)
    for k in range(1, TOP_K):
        y = y + w[:, k:k + 1] * gbuf[k].reshape(tm, d_model)
    x = x1_ref[...] + g2_ref[0] * y
    ms = jnp.mean(x * x, axis=-1, keepdims=True)
    o_ref[...] = x * lax.rsqrt(ms + EPS) * fg_ref[...]


def _combine(dest2d, ys, wgt, x1, mod3, final_g, n_ctx_rows, t_lat, *, first_row, n_out_rows, name):
    d = x1.shape[1]
    tm = _tile(min(n_ctx_rows, t_lat), ROW_TILE)
    row = _mod_row_fn(tm, n_ctx_rows, t_lat)
    assert (tm * TOP_K) % SMEM_TILE == 0
    t0 = first_row // tm
    return pl.pallas_call(
        functools.partial(_combine_kernel, tm=tm),
        grid=(n_out_rows // tm,),
        in_specs=[pl.BlockSpec((tm * TOP_K,), lambda i: (t0 + i,), memory_space=pltpu.SMEM),
                  pl.BlockSpec(memory_space=pl.ANY),
                  pl.BlockSpec((tm, LANES), lambda i: (t0 + i, 0)),
                  pl.BlockSpec((tm, d), lambda i: (t0 + i, 0)),
                  pl.BlockSpec((1, 1, d), lambda i: (row(t0 + i) * N_MOD + 5, 0, 0)),
                  pl.BlockSpec((1, d), lambda i: (0, 0))],
        out_specs=pl.BlockSpec((tm, d), lambda i: (i, 0)),
        out_shape=jax.ShapeDtypeStruct((n_out_rows, d), F32),
        scratch_shapes=[pltpu.VMEM((TOP_K, tm // F32_SUBLANES, F32_SUBLANES, d), F32),
                        pltpu.SemaphoreType.DMA(())],
        compiler_params=_params("arbitrary"),
        name=name,
    )(dest2d, ys, wgt, x1, mod3, final_g.reshape(1, d))


def kernel(x_prompt, x_sample, state_hgrn_fwd, state_hgrn_bwd, state_ret_fwd, state_ret_bwd, c, c_ctx, ada_w, ada_b, norm1_g, norm2_g, final_norm_g, w_in, hg_lb_fwd, hg_lb_bwd, hg_norm_g, ret_log2_fwd, ret_log2_bwd, ret_norm_g, w_proj_hgrn, w_proj_ret, w_out, router_w, router_b, moe_w_gu, moe_b_gu, moe_w_dn, moe_b_dn):
    assert ada_w.shape[0] == 1, "one trunk layer"
    b_ctx, t_ctx, d = x_prompt.shape
    b_lat, t_lat, _ = x_sample.shape
    n_ctx_rows = b_ctx * t_ctx
    assert n_ctx_rows % t_lat == 0 and b_lat + 1 <= COND_ROWS
    n_exp = router_w.shape[2]

    x_ctx = x_prompt.reshape(n_ctx_rows, d)
    x_lat = x_sample.reshape(b_lat * t_lat, d)
    cond = jnp.zeros((COND_ROWS, d), F32).at[0].set(c_ctx).at[1:1 + b_lat].set(c)
    mod3 = _adaln(cond, ada_w[0], ada_b[0]).reshape(COND_ROWS * N_MOD, 1, d)

    proj = _inproj(x_ctx, x_lat, mod3, norm1_g[0], w_in[0], t_lat)

    hg_args = (proj, hg_lb_fwd, hg_lb_bwd, hg_norm_g[0])
    oa_ctx, new_hf, new_hb = _hgrn(*hg_args, None, n_seq=b_ctx, t=t_ctx, row_off=0, emit_state=True)
    (oa_lat,) = _hgrn(*hg_args, (state_hgrn_fwd, state_hgrn_bwd), n_seq=b_lat, t=t_lat,
                      row_off=n_ctx_rows // t_lat, emit_state=False)

    p_f = jnp.broadcast_to(ret_log2_fwd[0][:, None, None], (RET_HEADS, 1, RET_DV))
    p_b = jnp.broadcast_to(ret_log2_bwd[0][:, None, None], (RET_HEADS, 1, RET_DV))
    ret_args = (proj, p_f, p_b, ret_norm_g[0])
    ob_ctx, new_rf, new_rb = _ret(*ret_args, None, None, n_seq=b_ctx, t=t_ctx, row_off=0, emit_state=True)
    (ob_lat,) = _ret(*ret_args, _rope_tables(t_lat), (state_ret_fwd, state_ret_bwd), n_seq=b_lat, t=t_lat,
                     row_off=n_ctx_rows // t_lat, emit_state=False)

    merged = _merge(oa_ctx, oa_lat, ob_ctx, ob_lat, proj, w_proj_hgrn[0].astype(BF16), w_proj_ret[0].astype(BF16))

    rw_pad = jnp.zeros((d, LANES), F32).at[:, :n_exp].set(router_w[0])
    rb_pad = jnp.full((1, LANES), -1e30, F32).at[0, :n_exp].set(router_b[0])
    x1, xp, top_idx, top_w = _router(merged, x_ctx, x_lat, mod3, norm2_g[0], w_out[0].astype(BF16),
                                     rw_pad, rb_pad, t_lat)

    dest, order2d, plan, n_rows = _moe_plan(top_idx[:, :TOP_K], n_exp)
    ff = moe_w_dn.shape[2]
    b_gu_perm = moe_b_gu[0].reshape(n_exp, ff // LANES, LANES, 2).transpose(0, 1, 3, 2).reshape(n_exp, 2 * ff)
    src = jnp.arange(MXU_DIM)
    perm = jnp.zeros((MXU_DIM, MXU_DIM), BF16).at[src, (src % 2) * LANES + src // 2].set(1)
    xp = xp.reshape(xp.shape[0] // (d // 2 // LANES), d // 2 // LANES, LANES)
    ys = _moe(xp, order2d, plan, moe_w_gu[0], moe_w_dn[0], b_gu_perm, moe_b_dn[0], perm, n_rows)

    comb_args = (dest, ys, top_w, x1, mod3, final_norm_g, n_ctx_rows, t_lat)
    y_prompt = _combine(*comb_args, first_row=0, n_out_rows=n_ctx_rows, name="combine_ctx")
    y_sample = _combine(*comb_args, first_row=n_ctx_rows, n_out_rows=b_lat * t_lat, name="combine_lat")
    return (y_prompt.reshape(b_ctx, t_ctx, d), y_sample.reshape(b_lat, t_lat, d), new_hf, new_hb, new_rf, new_rb)
```

```python
import functools

import jax
import jax.numpy as jnp
from jax import lax
from jax.experimental import pallas as pl
from jax.experimental.pallas import tpu as pltpu

F32 = jnp.float32
BF16 = jnp.bfloat16

HG_HEADS = 8
HG_DK = 128
HG_DV = 128
RET_HEADS = 8
RET_DK = 128
RET_DV = 256
CHUNK = 32
GRID_W = 64
ROPE_BASE = 10000.0
TOP_K = 4
SWIGLU_LIMIT = 7.0
SWIGLU_ALPHA = 1.702
N_MOD = 6
EPS = 1e-6

LANES = 128
F32_SUBLANES = 8
MXU_DIM = 256
VMEM_LIMIT = 56 * 1024 * 1024
COND_ROWS = 16
WIDE_TILE = 1024
MID_TILE = 512
ROW_TILE = 256
SCAN_BLOCK = 256
SCAN_UNROLL = 8
HGRN_ROWS_PER_STEP = 4096
RET_HEADS_PER_STEP = 4
RET_ROWS_PER_STEP = 2048
SMEM_TILE = 1024
MOE_GROUP = 128
MOE_ITEM_ROWS = 2048
MOE_GROUPS_PER_TRIP = 8
MOE_FF_TILE = 256
ROW_UNROLL = 16


def _sigmoid(x):
    return 0.5 * jnp.tanh(0.5 * x) + 0.5


def _params(*sem):
    return pltpu.CompilerParams(dimension_semantics=sem, vmem_limit_bytes=VMEM_LIMIT)


def _tile(n, pref):
    if n <= pref:
        return n
    for t in range(pref - pref % LANES, 0, -LANES):
        if n % t == 0:
            return t
    raise ValueError((n, pref))


def _adaln_kernel(c_ref, w_ref, b_ref, o_ref):
    c = c_ref[...]
    s = (c * _sigmoid(c)).astype(BF16)
    o_ref[...] = jnp.dot(s, w_ref[...].astype(BF16), preferred_element_type=F32) + b_ref[...]


def _adaln(cond, ada_w, ada_b):
    d, n = ada_w.shape
    tn = _tile(n, WIDE_TILE)
    return pl.pallas_call(
        _adaln_kernel,
        grid=(n // tn,),
        in_specs=[pl.BlockSpec((COND_ROWS, d), lambda j: (0, 0)),
                  pl.BlockSpec((d, tn), lambda j: (0, j)),
                  pl.BlockSpec((1, tn), lambda j: (0, j))],
        out_specs=pl.BlockSpec((COND_ROWS, tn), lambda j: (0, j)),
        out_shape=jax.ShapeDtypeStruct((COND_ROWS, n), F32),
        compiler_params=_params("arbitrary"),
        name="adaln",
    )(cond, ada_w, ada_b.reshape(1, n))


def _prenorm_kernel(xc_ref, xl_ref, sh_ref, sc_ref, g_ref, o_ref, *, n_ctx_tiles):
    x = jnp.where(pl.program_id(0) < n_ctx_tiles, xc_ref[...], xl_ref[...])
    ms = jnp.mean(x * x, axis=-1, keepdims=True)
    xn = x * lax.rsqrt(ms + EPS) * g_ref[...]
    o_ref[...] = (xn * (1.0 + sc_ref[0]) + sh_ref[0]).astype(BF16)


def _inproj_kernel(xm_ref, w_ref, o_ref):
    o_ref[...] = jnp.dot(xm_ref[...], w_ref[...].astype(BF16), preferred_element_type=F32)


def _mod_row_fn(tm, n_ctx_rows, t_lat):
    n_ctx_tiles = n_ctx_rows // tm

    def row(i):
        return jnp.where(i < n_ctx_tiles, 0, 1 + ((i - n_ctx_tiles) * tm) // t_lat)

    return row


def _group_specs(tm, width, n_ctx_tiles):
    ctx = lambda i: (jnp.minimum(i, n_ctx_tiles - 1), 0)
    lat = lambda i: (jnp.maximum(i - n_ctx_tiles, 0), 0)
    return [pl.BlockSpec((tm, width), ctx), pl.BlockSpec((tm, width), lat)]


def _inproj(x_ctx, x_lat, mod3, norm_g, w_in, t_lat):
    n_ctx_rows, d = x_ctx.shape
    t_all = n_ctx_rows + x_lat.shape[0]
    n = w_in.shape[1]
    tp = _tile(min(n_ctx_rows, t_lat), MID_TILE)
    row = _mod_row_fn(tp, n_ctx_rows, t_lat)
    xm = pl.pallas_call(
        functools.partial(_prenorm_kernel, n_ctx_tiles=n_ctx_rows // tp),
        grid=(t_all // tp,),
        in_specs=_group_specs(tp, d, n_ctx_rows // tp) + [
            pl.BlockSpec((1, 1, d), lambda i: (row(i) * N_MOD + 0, 0, 0)),
            pl.BlockSpec((1, 1, d), lambda i: (row(i) * N_MOD + 1, 0, 0)),
            pl.BlockSpec((1, d), lambda i: (0, 0))],
        out_specs=pl.BlockSpec((tp, d), lambda i: (i, 0)),
        out_shape=jax.ShapeDtypeStruct((t_all, d), BF16),
        compiler_params=_params("arbitrary"),
        name="prenorm",
    )(x_ctx, x_lat, mod3, mod3, norm_g.reshape(1, d))
    tm = _tile(t_all, WIDE_TILE)
    tn = _tile(n, WIDE_TILE)
    return pl.pallas_call(
        _inproj_kernel,
        grid=(t_all // tm, n // tn),
        in_specs=[pl.BlockSpec((tm, d), lambda i, j: (i, 0)),
                  pl.BlockSpec((d, tn), lambda i, j: (0, j))],
        out_specs=pl.BlockSpec((tm, tn), lambda i, j: (i, j)),
        out_shape=jax.ShapeDtypeStruct((t_all, n), F32),
        compiler_params=_params("arbitrary", "arbitrary"),
        name="inproj",
    )(xm, w_in)


def _dot_split(m_bf, x):
    hi = x.astype(BF16)
    lo = (x - hi.astype(F32)).astype(BF16)
    return jnp.dot(m_bf, hi, preferred_element_type=F32) + jnp.dot(m_bf, lo, preferred_element_type=F32)


_NT = (((1,), (1,)), ((), ()))
_TN = (((0,), (0,)), ((), ()))


def _hgrn_kernel(*refs, t, hpg, has_state, emit_state):
    hq_ref, hzf_ref, hzb_ref, hi_ref, hgate_ref, lbf_ref, lbb_ref, gn_ref = refs[:8]
    pos = 8
    if has_state:
        s0f_ref, s0b_ref = refs[pos:pos + 2]
        pos += 2
    out_ref = refs[pos]
    pos += 1
    if emit_state:
        sf_ref, sb_ref = refs[pos:pos + 2]
        pos += 2
    (q_in_f, k_end_f, dec_f, q_in_b, k_end_b, dec_b, v_bf, o_f, o_b, st_ref) = refs[pos:]
    heads = [slice(j * LANES, (j + 1) * LANES) for j in range(hpg)]

    blk = min(SCAN_BLOCK, t)
    n_blk = t // blk
    n_chunk = t // CHUNK

    r = lax.broadcasted_iota(jnp.int32, (blk, blk), 0)
    c = lax.broadcasted_iota(jnp.int32, (blk, blk), 1)
    same = (r // CHUNK) == (c // CHUNK)
    low = same & (c <= r)
    upp = same & (c >= r)
    m_pre = jnp.where(low, 1.0, 0.0).astype(BF16)
    m_suf = jnp.where(upp, 1.0, 0.0).astype(BF16)

    def lower_bound(lb_ref):
        a = lb_ref[...]
        e = jnp.exp(a - jnp.max(a, axis=0, keepdims=True))
        return e[0:1] / jnp.sum(e, axis=0, keepdims=True)

    lb_f = lower_bound(lbf_ref)
    lb_b = lower_bound(lbb_ref)

    width = hpg * LANES

    def chunk_total(g, reverse):
        g3 = g.reshape(blk // CHUNK, CHUNK, width)
        edge = g3[:, 0:1, :] if reverse else g3[:, CHUNK - 1:CHUNK, :]
        return jnp.broadcast_to(edge, g3.shape).reshape(blk, width)

    directions = ((hzf_ref, lb_f, m_pre, low, False, q_in_f, k_end_f, dec_f, o_f),
                  (hzb_ref, lb_b, m_suf, upp, True, q_in_b, k_end_b, dec_b, o_b))

    def prep(i, carry):
        rows = pl.ds(pl.multiple_of(i * blk, blk), blk)
        hq = hq_ref[rows, :]
        q = hq * _sigmoid(hq)
        v = hi_ref[rows, :].astype(BF16)
        v_bf[rows, :] = v
        for z_ref, lb, m_cum, mask, reverse, qi, ke, dc, o_ref in directions:
            f = lb + (1.0 - lb) * _sigmoid(z_ref[rows, :])
            k = 1.0 - f
            g = _dot_split(m_cum, jnp.log(f))
            tot = chunk_total(g, reverse)
            q_in = (q * jnp.exp(g)).astype(BF16)
            k_in = (k * jnp.exp(-g)).astype(BF16)
            qi[rows, :] = q_in
            ke[rows, :] = (k * jnp.exp(tot - g)).astype(BF16)
            dc[rows, :] = jnp.exp(tot)
            for hd in heads:
                s = lax.dot_general(q_in[:, hd], k_in[:, hd], _NT, preferred_element_type=F32)
                o_ref[rows, hd] = jnp.dot(jnp.where(mask, s, 0.0).astype(BF16), v[:, hd],
                                          preferred_element_type=F32)
        return carry

    lax.fori_loop(0, n_blk, prep, 0)

    for j in range(hpg):
        for slot, s0_ref in ((2 * j, s0f_ref if has_state else None), (2 * j + 1, s0b_ref if has_state else None)):
            st_ref[slot] = s0_ref[j].T if has_state else jnp.zeros((HG_DV, HG_DK), F32)

    def chunk_step(idx, slot, hd, qi, ke, dc, o_ref):
        start = pl.multiple_of(idx * CHUNK, CHUNK)
        rows = pl.ds(start, CHUNK)
        st = st_ref[slot]
        o_ref[rows, hd] += lax.dot_general(qi[rows, hd], st.astype(BF16), _NT, preferred_element_type=F32)
        upd = lax.dot_general(v_bf[rows, hd], ke[rows, hd], _TN, preferred_element_type=F32)
        st_ref[slot] = dc[pl.ds(start, 1), hd] * st + upd

    def body(i, carry):
        for j, hd in enumerate(heads):
            chunk_step(i, 2 * j, hd, q_in_f, k_end_f, dec_f, o_f)
            chunk_step(n_chunk - 1 - i, 2 * j + 1, hd, q_in_b, k_end_b, dec_b, o_b)
        return carry

    lax.fori_loop(0, n_chunk, body, 0, unroll=min(n_chunk, SCAN_UNROLL))
    if emit_state:
        for j in range(hpg):
            sf_ref[j] = st_ref[2 * j].T
            sb_ref[j] = st_ref[2 * j + 1].T

    for hd in heads:
        o = o_f[:, hd] + o_b[:, hd]
        o = o * lax.rsqrt(jnp.mean(o * o, axis=-1, keepdims=True) + EPS) * gn_ref[...]
        out_ref[:, hd] = (o * _sigmoid(hgate_ref[:, hd])).astype(BF16)


def _hgrn(proj, lb_fwd, lb_bwd, norm_g, states, *, n_seq, t, row_off, emit_state):
    nh = HG_HEADS
    hpg = max(1, min(nh, HGRN_ROWS_PER_STEP // t))
    assert nh % hpg == 0
    ng = nh // hpg
    w = hpg * LANES
    col = lambda k: (lambda b, h: (row_off + b, k * ng + h))
    blk = lambda k: pl.BlockSpec((t, w), col(k))
    in_specs = [blk(0), blk(1), blk(2), blk(3), blk(4),
                pl.BlockSpec((2, w), lambda b, h: (0, h)),
                pl.BlockSpec((2, w), lambda b, h: (0, h)),
                pl.BlockSpec((1, LANES), lambda b, h: (0, 0))]
    args = [proj] * 5 + [lb_fwd, lb_bwd, norm_g.reshape(1, HG_DV)]
    st_spec = pl.BlockSpec((None, None, hpg, HG_DK, HG_DV), lambda b, h: (b, 0, h, 0, 0))
    if states is not None:
        in_specs += [st_spec, st_spec]
        args += list(states)
    out_specs = [pl.BlockSpec((t, w), lambda b, h: (b, h))]
    out_shape = [jax.ShapeDtypeStruct((n_seq * t, nh * HG_DV), BF16)]
    if emit_state:
        out_specs += [st_spec, st_spec]
        out_shape += [jax.ShapeDtypeStruct((n_seq, 1, nh, HG_DK, HG_DV), F32)] * 2
    scratch = ([pltpu.VMEM((t, w), BF16)] * 2 + [pltpu.VMEM((t, w), F32)]) * 2
    scratch += [pltpu.VMEM((t, w), BF16), pltpu.VMEM((t, w), F32), pltpu.VMEM((t, w), F32),
                pltpu.VMEM((2 * hpg, HG_DV, HG_DK), F32)]
    return pl.pallas_call(
        functools.partial(_hgrn_kernel, t=t, hpg=hpg, has_state=states is not None, emit_state=emit_state),
        grid=(n_seq, ng),
        in_specs=in_specs,
        out_specs=out_specs,
        out_shape=out_shape,
        scratch_shapes=scratch,
        compiler_params=_params("arbitrary", "arbitrary"),
        name="hgrn_lat" if states is not None else "hgrn_ctx",
    )(*args)


def _ret_kernel(*refs, t, hpg, has_state, emit_state, use_rope):
    rq_ref, rk_ref, rv_ref, rgate_ref, pf_ref, pb_ref, gn_ref = refs[:7]
    pos = 7
    if use_rope:
        cos_ref, sin_ref = refs[pos:pos + 2]
        pos += 2
    if has_state:
        s0f_ref, s0b_ref = refs[pos:pos + 2]
        pos += 2
    out_ref = refs[pos]
    pos += 1
    if emit_state:
        sf_ref, sb_ref = refs[pos:pos + 2]
        pos += 2
    q_s, k_s, v_s, o_f, o_b, st_ref = refs[pos:]

    ch = min(MXU_DIM, t)
    n_chunk = t // ch

    lane = lax.broadcasted_iota(jnp.int32, (t, RET_DK), 1)
    first_half = (lane % 64) < 32

    def rope(x):
        if not use_rope:
            return x
        swapped = jnp.where(first_half, pltpu.roll(x, RET_DK - 32, 1), pltpu.roll(x, 32, 1))
        return x * cos_ref[...] + swapped * sin_ref[...]

    ri = lax.broadcasted_iota(jnp.int32, (ch, ch), 0)
    ci = lax.broadcasted_iota(jnp.int32, (ch, ch), 1)
    row128 = lax.broadcasted_iota(jnp.int32, (ch, RET_DK), 0).astype(F32)

    def direction(j, p_ref, slot, o_scr, s0_ref, reverse):
        qk = slice(j * RET_DK, (j + 1) * RET_DK)
        vv = slice(j * RET_DV, (j + 1) * RET_DV)
        lg = jnp.log1p(-jnp.exp2(p_ref[j]))
        lg_k = lg[:, :RET_DK]
        dist = (ci - ri) if reverse else (ri - ci)
        keep = dist >= 0
        dmask = jnp.where(keep, jnp.exp(lg[:, :ch] * jnp.where(keep, dist, 0).astype(F32)), 0.0)
        if reverse:
            q_scale = jnp.exp(lg_k * (ch - row128))
            k_scale = jnp.exp(lg_k * row128)
        else:
            q_scale = jnp.exp(lg_k * (row128 + 1.0))
            k_scale = jnp.exp(lg_k * (ch - 1.0 - row128))
        dec = jnp.exp(lg * float(ch))
        st_ref[slot] = s0_ref[j] if has_state else jnp.zeros((RET_DK, RET_DV), F32)
        for step in range(n_chunk):
            cidx = n_chunk - 1 - step if reverse else step
            rows = pl.ds(cidx * ch, ch)
            q = q_s[rows, qk]
            k = k_s[rows, qk]
            v = v_s[rows, vv]
            s = lax.dot_general(q.astype(BF16), k.astype(BF16), _NT, preferred_element_type=F32)
            o = jnp.dot((s * dmask).astype(BF16), v, preferred_element_type=F32)
            st = st_ref[slot]
            o = o + jnp.dot((q * q_scale).astype(BF16), st.astype(BF16), preferred_element_type=F32)
            o_scr[rows, vv] = o
            upd = lax.dot_general((k * k_scale).astype(BF16), v, _TN, preferred_element_type=F32)
            st_ref[slot] = dec * st + upd

    for j in range(hpg):
        qk = slice(j * RET_DK, (j + 1) * RET_DK)
        q_s[:, qk] = rope(rq_ref[:, qk])
        k_s[:, qk] = rope(rk_ref[:, qk] * (RET_DK ** -0.5))
    v_s[...] = rv_ref[...].astype(BF16)
    for j in range(hpg):
        direction(j, pf_ref, 2 * j, o_f, s0f_ref if has_state else None, False)
        direction(j, pb_ref, 2 * j + 1, o_b, s0b_ref if has_state else None, True)
    if emit_state:
        for j in range(hpg):
            sf_ref[j] = st_ref[2 * j]
            sb_ref[j] = st_ref[2 * j + 1]

    for j in range(hpg):
        vv = slice(j * RET_DV, (j + 1) * RET_DV)
        o = o_f[:, vv] + o_b[:, vv]
        oc = o - jnp.mean(o, axis=-1, keepdims=True)
        o = oc * lax.rsqrt(jnp.mean(oc * oc, axis=-1, keepdims=True) + EPS) * gn_ref[...]
        gate = rgate_ref[:, vv]
        out_ref[:, vv] = (o * (gate * _sigmoid(gate))).astype(BF16)


def _ret(proj, p_fwd, p_bwd, norm_g, rope, states, *, n_seq, t, row_off, emit_state):
    nh = RET_HEADS
    base = (3 * HG_HEADS * HG_DK + 2 * HG_HEADS * HG_DV)
    hpg = max(1, min(nh, RET_HEADS_PER_STEP, RET_ROWS_PER_STEP // t))
    while nh % hpg or base % (hpg * RET_DK) or (base + 2 * nh * RET_DK) % (hpg * RET_DV):
        hpg -= 1
    ng = nh // hpg
    wk, wv = hpg * RET_DK, hpg * RET_DV
    qk0 = base // wk
    v0 = (base + 2 * nh * RET_DK) // wv
    in_specs = [pl.BlockSpec((t, wk), lambda b, h: (row_off + b, qk0 + h)),
                pl.BlockSpec((t, wk), lambda b, h: (row_off + b, qk0 + ng + h)),
                pl.BlockSpec((t, wv), lambda b, h: (row_off + b, v0 + h)),
                pl.BlockSpec((t, wv), lambda b, h: (row_off + b, v0 + ng + h)),
                pl.BlockSpec((hpg, 1, RET_DV), lambda b, h: (h, 0, 0)),
                pl.BlockSpec((hpg, 1, RET_DV), lambda b, h: (h, 0, 0)),
                pl.BlockSpec((1, RET_DV), lambda b, h: (0, 0))]
    args = [proj] * 4 + [p_fwd, p_bwd, norm_g.reshape(1, RET_DV)]
    if rope is not None:
        in_specs += [pl.BlockSpec((t, RET_DK), lambda b, h: (0, 0))] * 2
        args += list(rope)
    st_spec = pl.BlockSpec((None, None, hpg, RET_DK, RET_DV), lambda b, h: (b, 0, h, 0, 0))
    if states is not None:
        in_specs += [st_spec, st_spec]
        args += list(states)
    out_specs = [pl.BlockSpec((t, wv), lambda b, h: (b, h))]
    out_shape = [jax.ShapeDtypeStruct((n_seq * t, nh * RET_DV), BF16)]
    if emit_state:
        out_specs += [st_spec, st_spec]
        out_shape += [jax.ShapeDtypeStruct((n_seq, 1, nh, RET_DK, RET_DV), F32)] * 2
    scratch = [pltpu.VMEM((t, wk), F32), pltpu.VMEM((t, wk), F32), pltpu.VMEM((t, wv), BF16),
               pltpu.VMEM((t, wv), F32), pltpu.VMEM((t, wv), F32),
               pltpu.VMEM((2 * hpg, RET_DK, RET_DV), F32)]
    return pl.pallas_call(
        functools.partial(_ret_kernel, t=t, hpg=hpg, has_state=states is not None, emit_state=emit_state,
                          use_rope=rope is not None),
        grid=(n_seq, ng),
        in_specs=in_specs,
        out_specs=out_specs,
        out_shape=out_shape,
        scratch_shapes=scratch,
        compiler_params=_params("arbitrary", "arbitrary"),
        name="ret_lat" if states is not None else "ret_ctx",
    )(*args)


def _rope_tables(t):
    pairs = RET_DK // 4
    pos = jnp.arange(t, dtype=jnp.int32)
    rows = (pos // GRID_W).astype(F32)
    cols = (pos % GRID_W).astype(F32)
    inv = ROPE_BASE ** (-jnp.arange(pairs, dtype=F32) / pairs)
    ar = rows[:, None] * inv
    ac = cols[:, None] * inv
    cos = jnp.concatenate([jnp.cos(ar), jnp.cos(ar), jnp.cos(ac), jnp.cos(ac)], axis=1)
    sin = jnp.concatenate([-jnp.sin(ar), jnp.sin(ar), -jnp.sin(ac), jnp.sin(ac)], axis=1)
    return cos, sin


def _merge_kernel(oac_ref, oal_ref, obc_ref, obl_ref, ga_ref, gb_ref, wa_ref, wb_ref, o_ref, *, n_ctx_tiles):
    is_ctx = pl.program_id(1) < n_ctx_tiles
    oa = jnp.where(is_ctx, oac_ref[...], oal_ref[...])
    ob = jnp.where(is_ctx, obc_ref[...], obl_ref[...])
    a = jnp.dot(oa, wa_ref[...], preferred_element_type=F32)
    b = jnp.dot(ob, wb_ref[...], preferred_element_type=F32)
    o_ref[...] = (_sigmoid(ga_ref[...]) * a + _sigmoid(gb_ref[...]) * b).astype(BF16)


def _merge(oa_ctx, oa_lat, ob_ctx, ob_lat, proj, wa_bf, wb_bf):
    n_ctx_rows, ka = oa_ctx.shape
    t_all = n_ctx_rows + oa_lat.shape[0]
    kb = ob_ctx.shape[1]
    d = wa_bf.shape[1]
    tm = _tile(min(n_ctx_rows, oa_lat.shape[0]), MID_TILE)
    tn = _tile(d, WIDE_TILE)
    nct = n_ctx_rows // tm
    gate0 = (proj.shape[1] - 2 * d) // tn
    ctx = lambda j, i: (jnp.minimum(i, nct - 1), 0)
    lat = lambda j, i: (jnp.maximum(i - nct, 0), 0)
    return pl.pallas_call(
        functools.partial(_merge_kernel, n_ctx_tiles=nct),
        grid=(d // tn, t_all // tm),
        in_specs=[pl.BlockSpec((tm, ka), ctx), pl.BlockSpec((tm, ka), lat),
                  pl.BlockSpec((tm, kb), ctx), pl.BlockSpec((tm, kb), lat),
                  pl.BlockSpec((tm, tn), lambda j, i: (i, gate0 + j)),
                  pl.BlockSpec((tm, tn), lambda j, i: (i, gate0 + d // tn + j)),
                  pl.BlockSpec((ka, tn), lambda j, i: (0, j)),
                  pl.BlockSpec((kb, tn), lambda j, i: (0, j))],
        out_specs=pl.BlockSpec((tm, tn), lambda j, i: (i, j)),
        out_shape=jax.ShapeDtypeStruct((t_all, d), BF16),
        compiler_params=_params("arbitrary", "arbitrary"),
        name="merge",
    )(oa_ctx, oa_lat, ob_ctx, ob_lat, proj, proj, wa_bf, wb_bf)


def _pack_bf16_pairs(x):
    half = x.shape[1] // 2
    lo = lax.bitcast_convert_type(x[:, :half].astype(BF16).astype(F32), jnp.uint32)
    hi = lax.bitcast_convert_type(x[:, half:].astype(BF16).astype(F32), jnp.uint32)
    return (lo >> 16) | (hi & jnp.uint32(0xFFFF0000))


def _unpack_bf16_pairs(w):
    lo = lax.bitcast_convert_type(w << 16, F32).astype(BF16)
    hi = lax.bitcast_convert_type(w & jnp.uint32(0xFFFF0000), F32).astype(BF16)
    return lo, hi


def _router_kernel(m_ref, xc_ref, xl_ref, g1_ref, sh_ref, sc_ref, ng_ref, wo_ref, rw2_ref, rwh_ref, rb_ref,
                   x1_ref, xp_ref, idx_ref, wgt_ref, *, n_ctx_tiles):
    out = jnp.dot(m_ref[...], wo_ref[...], preferred_element_type=F32)
    x = jnp.where(pl.program_id(0) < n_ctx_tiles, xc_ref[...], xl_ref[...])
    x1 = x + g1_ref[0] * out
    x1_ref[...] = x1
    ms = jnp.mean(x1 * x1, axis=-1, keepdims=True)
    xm = x1 * lax.rsqrt(ms + EPS) * ng_ref[...]
    xm = xm * (1.0 + sc_ref[0]) + sh_ref[0]
    packed = _pack_bf16_pairs(xm)
    n_slab = packed.shape[1] // LANES
    for s in range(n_slab):
        xp_ref[pl.ds(s, packed.shape[0], stride=n_slab), :] = packed[:, s * LANES:(s + 1) * LANES]

    x_hi = xm.astype(BF16)
    x_lo = (xm - x_hi.astype(F32)).astype(BF16)
    hh_hl = jnp.dot(x_hi, rw2_ref[...], preferred_element_type=F32)
    lh = jnp.dot(x_lo, rwh_ref[...], preferred_element_type=F32)
    logits = hh_hl[:, :LANES] + (hh_hl[:, LANES:] + lh) + rb_ref[...]
    lane = lax.broadcasted_iota(jnp.int32, logits.shape, 1)
    lane_f = lane.astype(F32)
    idx_out = jnp.zeros(logits.shape, F32)
    val_out = jnp.zeros(logits.shape, F32)
    top = None
    for k in range(TOP_K):
        m = jnp.max(logits, axis=-1, keepdims=True)
        idx = jnp.min(jnp.where(logits == m, lane_f, float(LANES)), axis=-1, keepdims=True)
        if top is None:
            top = m
        idx_out = jnp.where(lane == k, idx, idx_out)
        val_out = jnp.where(lane == k, jnp.exp(m - top), val_out)
        logits = jnp.where(lane_f == idx, -jnp.inf, logits)
    idx_ref[...] = idx_out.astype(jnp.int32)
    wgt_ref[...] = val_out / jnp.sum(val_out, axis=-1, keepdims=True)


def _router(merged, x_ctx, x_lat, mod3, norm_g, wo_bf, rw_pad, rb_pad, t_lat):
    n_ctx_rows, d = x_ctx.shape
    t_all = n_ctx_rows + x_lat.shape[0]
    tm = _tile(min(n_ctx_rows, t_lat), ROW_TILE)
    row = _mod_row_fn(tm, n_ctx_rows, t_lat)
    mod_spec = lambda k: pl.BlockSpec((1, 1, d), lambda i: (row(i) * N_MOD + k, 0, 0))
    rw_hi = rw_pad.astype(BF16)
    rw_lo = (rw_pad - rw_hi.astype(F32)).astype(BF16)
    rw2 = jnp.concatenate([rw_hi, rw_lo], axis=1)
    return pl.pallas_call(
        functools.partial(_router_kernel, n_ctx_tiles=n_ctx_rows // tm),
        grid=(t_all // tm,),
        in_specs=[pl.BlockSpec((tm, d), lambda i: (i, 0))] + _group_specs(tm, d, n_ctx_rows // tm) + [
                  mod_spec(2), mod_spec(3), mod_spec(4),
                  pl.BlockSpec((1, d), lambda i: (0, 0)),
                  pl.BlockSpec((d, d), lambda i: (0, 0)),
                  pl.BlockSpec((d, 2 * LANES), lambda i: (0, 0)),
                  pl.BlockSpec((d, LANES), lambda i: (0, 0)),
                  pl.BlockSpec((1, LANES), lambda i: (0, 0))],
        out_specs=[pl.BlockSpec((tm, d), lambda i: (i, 0)),
                   pl.BlockSpec((tm * (d // 2 // LANES), LANES), lambda i: (i, 0)),
                   pl.BlockSpec((tm, LANES), lambda i: (i, 0)),
                   pl.BlockSpec((tm, LANES), lambda i: (i, 0))],
        out_shape=[jax.ShapeDtypeStruct((t_all, d), F32),
                   jax.ShapeDtypeStruct((t_all * (d // 2 // LANES), LANES), jnp.uint32),
                   jax.ShapeDtypeStruct((t_all, LANES), jnp.int32),
                   jax.ShapeDtypeStruct((t_all, LANES), F32)],
        compiler_params=_params("arbitrary"),
        name="router",
    )(merged, x_ctx, x_lat, mod3, mod3, mod3, norm_g.reshape(1, d), wo_bf, rw2, rw_hi, rb_pad)


def _moe_kernel(e_ref, start_ref, nsub_ref, src_ref, tail_ref,
                xp_hbm, order_hbm, wgu_ref, wdn_ref, bgu_ref, bdn_ref, perm_ref,
                ys_hbm,
                tok_smem, x_raw, x_lo, x_hi, y_acc, wgu_bf, wdn_bf, pend, tok_sem, row_sem, out_sem,
                *, n_ff_tiles):
    i = pl.program_id(0)
    f = pl.program_id(1)
    nsub = nsub_ref[i]
    start = start_ref[i]
    n_slab = xp_hbm.shape[1]
    g = MOE_GROUP
    d = y_acc.shape[1]
    half = d // 2
    ff2 = wgu_bf.shape[1]

    def order_copy(it):
        first = pl.multiple_of(src_ref[it] & ~(SMEM_TILE - 1), SMEM_TILE)
        return pltpu.make_async_copy(order_hbm.at[pl.ds(first, tok_smem.shape[0])], tok_smem, tok_sem)

    def row_copy(off, base, u):
        tok = lax.shift_right_logical(tok_smem[off + base + u], 2)
        return pltpu.make_async_copy(xp_hbm.at[tok], x_raw.at[pl.ds((base + u) * n_slab, n_slab), :], row_sem)

    def gather_start(it):
        off = src_ref[it] & (SMEM_TILE - 1)

        def issue(r, c):
            for u in range(ROW_UNROLL):
                row_copy(off, r * ROW_UNROLL, u).start()
            return c

        lax.fori_loop(0, nsub_ref[it] * (g // ROW_UNROLL), issue, 0)

    def gather_wait(it):
        off = src_ref[it] & (SMEM_TILE - 1)

        def drain(r, c):
            for u in range(ROW_UNROLL):
                row_copy(off, r * ROW_UNROLL, u).wait()
            return c

        lax.fori_loop(0, nsub_ref[it] * (g // ROW_UNROLL), drain, 0)

    nxt = jnp.minimum(i + 1, pl.num_programs(0) - 1)
    has_next = (i + 1 < pl.num_programs(0)) & (nsub_ref[nxt] > 0)

    def out_drain():
        def drain(k, c):
            pltpu.make_async_copy(y_acc.at[0:g, :], ys_hbm.at[0:g, :], out_sem).wait()
            return c

        lax.fori_loop(0, pend[0], drain, 0)
        pend[0] = 0

    @pl.when((i == 0) & (f == 0))
    def _():
        pend[0] = 0

    @pl.when((i == 0) & (f == 0) & (nsub > 0))
    def _():
        order_copy(0).start()
        order_copy(0).wait()
        gather_start(0)

    @pl.when((f == 0) & (nsub > 0))
    def _():
        gather_wait(i)

        def unpack(sb, c):
            rows = pl.ds(pl.multiple_of(sb * g, g), g)
            for s in range(n_slab):
                lo, hi = _unpack_bf16_pairs(x_raw[pl.ds(sb * (g * n_slab) + s, g, stride=n_slab), :])
                x_lo[rows, s * LANES:(s + 1) * LANES] = lo
                x_hi[rows, s * LANES:(s + 1) * LANES] = hi
            return c

        lax.fori_loop(0, nsub, unpack, 0)
        out_drain()

        def init(sb, c):
            rows = pl.ds(pl.multiple_of(sb * g, g), g)
            y_acc[rows, :] = jnp.broadcast_to(bdn_ref[...], (g, d))
            return c

        lax.fori_loop(0, nsub, init, 0)

        @pl.when(has_next)
        def _():
            order_copy(nxt).start()

    @pl.when((f == 1) & (nsub > 0) & has_next)
    def _():
        order_copy(nxt).wait()
        gather_start(nxt)

    @pl.when(nsub > 0)
    def _():
        for j in range(ff2 // MXU_DIM):
            cols = slice(j * MXU_DIM, (j + 1) * MXU_DIM)
            wgu_bf[:, cols] = jnp.dot(wgu_ref[:, cols].astype(BF16), perm_ref[...],
                                      preferred_element_type=F32).astype(BF16)
        wdn_bf[...] = wdn_ref[...].astype(BF16)
        bgu = bgu_ref[...]

        def expert_rows(first_row, n_rows):
            rows = pl.ds(pl.multiple_of(first_row, g), n_rows)
            h = (jnp.dot(x_lo[rows, :], wgu_bf[:half, :], preferred_element_type=F32)
                 + jnp.dot(x_hi[rows, :], wgu_bf[half:, :], preferred_element_type=F32) + bgu)
            acts = []
            for j in range(ff2 // MXU_DIM):
                gate = jnp.minimum(h[:, j * MXU_DIM: j * MXU_DIM + LANES], SWIGLU_LIMIT)
                up = jnp.clip(h[:, j * MXU_DIM + LANES: (j + 1) * MXU_DIM], -SWIGLU_LIMIT, SWIGLU_LIMIT)
                acts.append(((up + 1.0) * gate * _sigmoid(SWIGLU_ALPHA * gate)).astype(BF16))
            act = jnp.concatenate(acts, axis=1)
            y_acc[rows, :] += jnp.dot(act, wdn_bf[...], preferred_element_type=F32)

        big = MOE_GROUPS_PER_TRIP

        def trip(q, c):
            expert_rows(q * (big * g), big * g)
            return c

        lax.fori_loop(0, lax.shift_right_logical(nsub, big.bit_length() - 1), trip, 0)
        rem = nsub & (big - 1)
        part = big // 2
        while part >= 1:
            @pl.when((rem & part) != 0)
            def _(part=part):
                expert_rows((nsub - rem + (rem & ~(2 * part - 1))) * g, part * g)
            part //= 2

    @pl.when((f == n_ff_tiles - 1) & (nsub > 0))
    def _():
        def out_copy(sb):
            rows = pl.ds(pl.multiple_of(sb * g, g), g)
            dst = pl.ds(pl.multiple_of(start * LANES + sb * g, g), g)
            return pltpu.make_async_copy(y_acc.at[rows, :], ys_hbm.at[dst, :], out_sem)

        def issue(sb, c):
            out_copy(sb).start()
            return c

        lax.fori_loop(0, nsub, issue, 0)
        pend[0] = nsub

    @pl.when((i == pl.num_programs(0) - 1) & (f == n_ff_tiles - 1))
    def _():
        out_drain()
        y_acc[0:g, :] = jnp.zeros((g, d), F32)
        first = tail_ref[0]
        n_tail = ys_hbm.shape[0] // g - first

        def tail_copy(k):
            dst = pl.ds(pl.multiple_of((first + k) * g, g), g)
            return pltpu.make_async_copy(y_acc.at[0:g, :], ys_hbm.at[dst, :], out_sem)

        def issue(k, c):
            tail_copy(k).start()
            return c

        def drain(k, c):
            tail_copy(k).wait()
            return c

        lax.fori_loop(0, n_tail, issue, 0)
        lax.fori_loop(0, n_tail, drain, 0)


def _moe(xp, order2d, plan, w_gu, w_dn, b_gu_perm, b_dn, perm, n_rows):
    n_exp, d, ff2_all = w_gu.shape
    ff = ff2_all // 2
    tf = _tile(ff, MOE_FF_TILE)
    n_f = ff // tf
    assert n_f >= 2, "the next item's row gather is started in the second feature step"
    n_items = plan[0].shape[0]

    def f_eff(i, f, nsub):
        return jnp.where(nsub[i] > 0, f, n_f - 1)

    grid_spec = pltpu.PrefetchScalarGridSpec(
        num_scalar_prefetch=len(plan),
        grid=(n_items, n_f),
        in_specs=[pl.BlockSpec(memory_space=pl.ANY),
                  pl.BlockSpec(memory_space=pl.ANY),
                  pl.BlockSpec((None, d, 2 * tf), lambda i, f, e, s, n, *_: (e[i], 0, f_eff(i, f, n))),
                  pl.BlockSpec((None, tf, d), lambda i, f, e, s, n, *_: (e[i], f_eff(i, f, n), 0)),
                  pl.BlockSpec((None, 1, 2 * tf), lambda i, f, e, s, n, *_: (e[i], 0, f_eff(i, f, n))),
                  pl.BlockSpec((None, 1, d), lambda i, f, e, s, n, *_: (e[i], 0, 0)),
                  pl.BlockSpec((MXU_DIM, MXU_DIM), lambda i, f, e, s, n, *_: (0, 0))],
        out_specs=pl.BlockSpec(memory_space=pl.ANY),
        scratch_shapes=[pltpu.SMEM((MOE_ITEM_ROWS + SMEM_TILE,), jnp.int32),
                        pltpu.VMEM((MOE_ITEM_ROWS * xp.shape[1], LANES), jnp.uint32),
                        pltpu.VMEM((MOE_ITEM_ROWS, d // 2), BF16),
                        pltpu.VMEM((MOE_ITEM_ROWS, d // 2), BF16),
                        pltpu.VMEM((MOE_ITEM_ROWS, d), F32),
                        pltpu.VMEM((d, 2 * tf), BF16),
                        pltpu.VMEM((tf, d), BF16),
                        pltpu.SMEM((1,), jnp.int32),
                        pltpu.SemaphoreType.DMA(()),
                        pltpu.SemaphoreType.DMA(()),
                        pltpu.SemaphoreType.DMA(())],
    )
    return pl.pallas_call(
        functools.partial(_moe_kernel, n_ff_tiles=n_f),
        grid_spec=grid_spec,
        out_shape=jax.ShapeDtypeStruct((n_rows, d), F32),
        compiler_params=_params("arbitrary", "arbitrary"),
        name="moe",
    )(*plan, xp, order2d, w_gu, w_dn,
      b_gu_perm.reshape(n_exp, 1, ff2_all), b_dn.reshape(n_exp, 1, d), perm)


def _moe_plan(top_idx, n_exp):
    t_all = top_idx.shape[0]
    n_assign = t_all * TOP_K
    g = MOE_GROUP
    rows_max = n_assign + n_exp * g
    n_items = n_exp + rows_max // MOE_ITEM_ROWS
    n_rows = rows_max + MOE_ITEM_ROWS

    flat_e = top_idx.reshape(-1).astype(jnp.int32)
    assign = jnp.arange(n_assign, dtype=jnp.int32)
    e_sorted, order = lax.sort_key_val(flat_e, assign)
    experts = jnp.arange(n_exp, dtype=jnp.int32)
    first = jnp.sum(e_sorted[:, None] < experts[None, :], axis=0, dtype=jnp.int32)
    counts = jnp.concatenate([first[1:], jnp.full((1,), n_assign, jnp.int32)]) - first
    padded = (counts + g - 1) // g * g
    pad_end = jnp.cumsum(padded)
    pad_start = pad_end - padded
    dest_sorted = pad_start[e_sorted] + assign - first[e_sorted]
    _, dest = lax.sort_key_val(order, dest_sorted)
    assert n_assign % SMEM_TILE == 0
    order2d = jnp.concatenate([order, jnp.zeros((MOE_ITEM_ROWS + SMEM_TILE,), jnp.int32)])

    chunks = (padded + MOE_ITEM_ROWS - 1) // MOE_ITEM_ROWS
    chunk_end = jnp.cumsum(chunks)
    items = jnp.arange(n_items, dtype=jnp.int32)
    n_used = chunk_end[-1]
    last_e = jnp.max(jnp.where(chunks > 0, jnp.arange(n_exp, dtype=jnp.int32), 0))
    e_of = jnp.minimum(jnp.searchsorted(chunk_end, items, side="right"), n_exp - 1).astype(jnp.int32)
    local = items - (chunk_end[e_of] - chunks[e_of])
    used = items < n_used
    item_e = jnp.where(used, e_of, last_e).astype(jnp.int32)
    left = padded[e_of] - local * MOE_ITEM_ROWS
    item_nsub = jnp.where(used, jnp.clip(left, 0, MOE_ITEM_ROWS) // g, 0).astype(jnp.int32)
    item_start = jnp.where(used, (pad_start[e_of] + local * MOE_ITEM_ROWS) // LANES, 0).astype(jnp.int32)
    item_src = jnp.where(used, first[e_of] + local * MOE_ITEM_ROWS, 0).astype(jnp.int32)
    tail = (jnp.sum(padded) // g).astype(jnp.int32).reshape(1)
    return dest, order2d, (item_e, item_start, item_nsub, item_src, tail), n_rows


def _combine_kernel(dest_ref, ys_hbm, wgt_ref, x1_ref, g2_ref, fg_ref, o_ref, gbuf, sem, *, tm):
    d_model = gbuf.shape[-1]

    def row_copy(trip, j, k):
        d = dest_ref[trip * (F32_SUBLANES * TOP_K) + j * TOP_K + k]
        return pltpu.make_async_copy(ys_hbm.at[pl.ds(d, 1), :], gbuf.at[k, trip, pl.ds(j, 1), :], sem)

    def issue(trip, c):
        for j in range(F32_SUBLANES):
            for k in range(TOP_K):
                row_copy(trip, j, k).start()
        return c

    def drain(trip, c):
        for j in range(F32_SUBLANES):
            for k in range(TOP_K):
                row_copy(trip, j, k).wait()
        return c

    lax.fori_loop(0, tm // F32_SUBLANES, issue, 0)
    lax.fori_loop(0, tm // F32_SUBLANES, drain, 0)

    w = wgt_ref[...]
    y = w[:, 0:1] * gbuf[0].reshape(tm, d_model)
    for k in range(1, TOP_K):
        y = y + w[:, k:k + 1] * gbuf[k].reshape(tm, d_model)
    x = x1_ref[...] + g2_ref[0] * y
    ms = jnp.mean(x * x, axis=-1, keepdims=True)
    o_ref[...] = x * lax.rsqrt(ms + EPS) * fg_ref[...]


def _combine(dest2d, ys, wgt, x1, mod3, final_g, n_ctx_rows, t_lat, *, first_row, n_out_rows, name):
    d = x1.shape[1]
    tm = _tile(min(n_ctx_rows, t_lat), ROW_TILE)
    row = _mod_row_fn(tm, n_ctx_rows, t_lat)
    assert (tm * TOP_K) % SMEM_TILE == 0
    t0 = first_row // tm
    return pl.pallas_call(
        functools.partial(_combine_kernel, tm=tm),
        grid=(n_out_rows // tm,),
        in_specs=[pl.BlockSpec((tm * TOP_K,), lambda i: (t0 + i,), memory_space=pltpu.SMEM),
                  pl.BlockSpec(memory_space=pl.ANY),
                  pl.BlockSpec((tm, LANES), lambda i: (t0 + i, 0)),
                  pl.BlockSpec((tm, d), lambda i: (t0 + i, 0)),
                  pl.BlockSpec((1, 1, d), lambda i: (row(t0 + i) * N_MOD + 5, 0, 0)),
                  pl.BlockSpec((1, d), lambda i: (0, 0))],
        out_specs=pl.BlockSpec((tm, d), lambda i: (i, 0)),
        out_shape=jax.ShapeDtypeStruct((n_out_rows, d), F32),
        scratch_shapes=[pltpu.VMEM((TOP_K, tm // F32_SUBLANES, F32_SUBLANES, d), F32),
                        pltpu.SemaphoreType.DMA(())],
        compiler_params=_params("arbitrary"),
        name=name,
    )(dest2d, ys, wgt, x1, mod3, final_g.reshape(1, d))


def kernel(x_prompt, x_sample, state_hgrn_fwd, state_hgrn_bwd, state_ret_fwd, state_ret_bwd, c, c_ctx, ada_w, ada_b, norm1_g, norm2_g, final_norm_g, w_in, hg_lb_fwd, hg_lb_bwd, hg_norm_g, ret_log2_fwd, ret_log2_bwd, ret_norm_g, w_proj_hgrn, w_proj_ret, w_out, router_w, router_b, moe_w_gu, moe_b_gu, moe_w_dn, moe_b_dn):
    assert ada_w.shape[0] == 1, "one trunk layer"
    b_ctx, t_ctx, d = x_prompt.shape
    b_lat, t_lat, _ = x_sample.shape
    n_ctx_rows = b_ctx * t_ctx
    assert n_ctx_rows % t_lat == 0 and b_lat + 1 <= COND_ROWS
    n_exp = router_w.shape[2]

    x_ctx = x_prompt.reshape(n_ctx_rows, d)
    x_lat = x_sample.reshape(b_lat * t_lat, d)
    cond = jnp.zeros((COND_ROWS, d), F32).at[0].set(c_ctx).at[1:1 + b_lat].set(c)
    mod3 = _adaln(cond, ada_w[0], ada_b[0]).reshape(COND_ROWS * N_MOD, 1, d)

    proj = _inproj(x_ctx, x_lat, mod3, norm1_g[0], w_in[0], t_lat)

    hg_args = (proj, hg_lb_fwd, hg_lb_bwd, hg_norm_g[0])
    oa_ctx, new_hf, new_hb = _hgrn(*hg_args, None, n_seq=b_ctx, t=t_ctx, row_off=0, emit_state=True)
    (oa_lat,) = _hgrn(*hg_args, (state_hgrn_fwd, state_hgrn_bwd), n_seq=b_lat, t=t_lat,
                      row_off=n_ctx_rows // t_lat, emit_state=False)

    p_f = jnp.broadcast_to(ret_log2_fwd[0][:, None, None], (RET_HEADS, 1, RET_DV))
    p_b = jnp.broadcast_to(ret_log2_bwd[0][:, None, None], (RET_HEADS, 1, RET_DV))
    ret_args = (proj, p_f, p_b, ret_norm_g[0])
    ob_ctx, new_rf, new_rb = _ret(*ret_args, None, None, n_seq=b_ctx, t=t_ctx, row_off=0, emit_state=True)
    (ob_lat,) = _ret(*ret_args, _rope_tables(t_lat), (state_ret_fwd, state_ret_bwd), n_seq=b_lat, t=t_lat,
                     row_off=n_ctx_rows // t_lat, emit_state=False)

    merged = _merge(oa_ctx, oa_lat, ob_ctx, ob_lat, proj, w_proj_hgrn[0].astype(BF16), w_proj_ret[0].astype(BF16))

    rw_pad = jnp.zeros((d, LANES), F32).at[:, :n_exp].set(router_w[0])
    rb_pad = jnp.full((1, LANES), -1e30, F32).at[0, :n_exp].set(router_b[0])
    x1, xp, top_idx, top_w = _router(merged, x_ctx, x_lat, mod3, norm2_g[0], w_out[0].astype(BF16),
                                     rw_pad, rb_pad, t_lat)

    dest, order2d, plan, n_rows = _moe_plan(top_idx[:, :TOP_K], n_exp)
    ff = moe_w_dn.shape[2]
    b_gu_perm = moe_b_gu[0].reshape(n_exp, ff // LANES, LANES, 2).transpose(0, 1, 3, 2).reshape(n_exp, 2 * ff)
    src = jnp.arange(MXU_DIM)
    perm = jnp.zeros((MXU_DIM, MXU_DIM), BF16).at[src, (src % 2) * LANES + src // 2].set(1)
    xp = xp.reshape(xp.shape[0] // (d // 2 // LANES), d // 2 // LANES, LANES)
    ys = _moe(xp, order2d, plan, moe_w_gu[0], moe_w_dn[0], b_gu_perm, moe_b_dn[0], perm, n_rows)

    comb_args = (dest, ys, top_w, x1, mod3, final_norm_g, n_ctx_rows, t_lat)
    y_prompt = _combine(*comb_args, first_row=0, n_out_rows=n_ctx_rows, name="combine_ctx")
    y_sample = _combine(*comb_args, first_row=n_ctx_rows, n_out_rows=b_lat * t_lat, name="combine_lat")
    return (y_prompt.reshape(b_ctx, t_ctx, d), y_sample.reshape(b_lat, t_lat, d), new_hf, new_hb, new_rf, new_rb)
```
